```python
import math
import jax, jax.numpy as jnp
from jax import lax
import numpy as np

D_MODEL = 2048
BATCH = 2
SEQ = 4096
DEPTH = 4
DEC_BATCH = 8
DEC_SEQ = 8
PAST_LEN = 16384
PAGE_SIZE = 128

POOL_WIDTH = D_MODEL
POOL_WINDOWS = (2, 4, 8, 16)
N_POOL_GROUPS = len(POOL_WINDOWS)
POOL_GROUP = POOL_WIDTH // N_POOL_GROUPS
POOL_BUF = max(POOL_WINDOWS) - 1
SSD_INNER = 2 * D_MODEL
SSD_HEADDIM = 64
SSD_HEADS = SSD_INNER // SSD_HEADDIM
SSD_GROUPS = 8
SSD_STATE = 128
SSD_CONV = 4
SSD_CONV_DIM = SSD_INNER + 2 * SSD_GROUPS * SSD_STATE
SSD_CHUNK = 128
ATTN_HEADS = 16
ATTN_KV_HEADS = 8
HEAD_DIM = D_MODEL // ATTN_HEADS
ATTN_GROUP = ATTN_HEADS // ATTN_KV_HEADS
ATTN_WIDTH = ATTN_HEADS * HEAD_DIM
KV_WIDTH = ATTN_KV_HEADS * HEAD_DIM
MOBA_BLOCK = 256
MOBA_TOPK = 3
MOBA_QCHUNK = 16
N_BRANCHES = 3
D_FF = -(-(8 * D_MODEL) // (3 * 256)) * 256
IN_SPLITS = (POOL_WIDTH, SSD_INNER, SSD_CONV_DIM, SSD_HEADS, ATTN_WIDTH, KV_WIDTH, KV_WIDTH, N_BRANCHES * D_MODEL)
IN_WIDTH = sum(IN_SPLITS)
NORM_EPS = 1e-6

kernel_name = 'hybrid_pool_ssd_moba_decoder_step'


def rms_norm(x, g):
    xf = x.astype(jnp.float32)
    y = xf * lax.rsqrt(jnp.mean(xf * xf, axis=-1, keepdims=True) + NORM_EPS)
    return (y * g.astype(jnp.float32)).astype(x.dtype)


def split_cols(proj, sizes):
    cuts, acc = [], 0
    for s in sizes[:-1]:
        acc += s
        cuts.append(acc)
    return jnp.split(proj, cuts, axis=-1)


def pool_mixer(p, buf, start, w_grp, scale):
    bsz, t, c = p.shape
    ext = jnp.concatenate([buf.astype(p.dtype), p], axis=1)
    extf = ext.astype(jnp.float32)
    cs = jnp.concatenate([jnp.zeros((bsz, 1, c), jnp.float32), jnp.cumsum(extf, axis=1)], axis=1)
    n_valid = start + jnp.arange(t) + 1
    pooled = []
    for gi, w in enumerate(POOL_WINDOWS):
        sl = slice(gi * POOL_GROUP, (gi + 1) * POOL_GROUP)
        win_sum = cs[:, POOL_BUF + 1:, sl] - cs[:, POOL_BUF + 1 - w:POOL_BUF + 1 - w + t, sl]
        cnt = jnp.minimum(n_valid, w).astype(jnp.float32)
        pooled.append(win_sum / cnt[None, :, None])
    mixed = (jnp.concatenate(pooled, axis=-1) - extf[:, POOL_BUF:]).astype(p.dtype)
    mixed = jnp.einsum('btgi,gio->btgo', mixed.reshape(bsz, t, N_POOL_GROUPS, POOL_GROUP), w_grp)
    return mixed.reshape(bsz, t, c) * scale, ext[:, t:]


def causal_conv(u, buf, w, b):
    t = u.shape[1]
    ext = jnp.concatenate([buf.astype(u.dtype), u], axis=1)
    out = b
    for j in range(SSD_CONV):
        out = out + ext[:, j:j + t] * w[j]
    return jax.nn.silu(out), ext[:, t:]


def ssd_scan(x, dt, a, bmat, cmat, h0):
    bsz, t, nh, hp = x.shape
    ng, ns = bmat.shape[2], bmat.shape[3]
    hg = nh // ng
    f32 = jnp.float32
    lc = min(SSD_CHUNK, t)
    nc = -(-t // lc)
    pad = nc * lc - t
    x, bmat, cmat, dt = x.astype(f32), bmat.astype(f32), cmat.astype(f32), dt.astype(f32)
    if pad:
        p4 = ((0, 0), (0, pad), (0, 0), (0, 0))
        x, bmat, cmat = jnp.pad(x, p4), jnp.pad(bmat, p4), jnp.pad(cmat, p4)
        dt = jnp.pad(dt, ((0, 0), (0, pad), (0, 0)))
    xc = x.reshape(bsz, nc, lc, ng, hg, hp)
    dtc = dt.reshape(bsz, nc, lc, ng, hg)
    bc = bmat.reshape(bsz, nc, lc, ng, ns)
    cc = cmat.reshape(bsz, nc, lc, ng, ns)
    a_cs = jnp.cumsum(dtc * a.reshape(ng, hg), axis=2)
    causal = jnp.tril(jnp.ones((lc, lc), bool))[:, :, None, None]
    seg = a_cs[:, :, :, None] - a_cs[:, :, None]
    decay = jnp.exp(jnp.where(causal, seg, -jnp.inf))
    xdt = xc * dtc[..., None]
    cb = jnp.einsum('bclgn,bcsgn->bclsg', cc, bc)
    y_diag = jnp.einsum('bclsg,bclsgh,bcsghp->bclghp', cb, decay, xdt)
    decay_end = jnp.exp(a_cs[:, :, -1:] - a_cs)
    chunk_states = jnp.einsum('bclgn,bclgh,bclghp->bcghpn', bc, decay_end, xdt)
    chunk_decay = jnp.exp(a_cs[:, :, -1])

    def step(h, inp):
        s_c, d_c = inp
        return h * d_c[..., None, None] + s_c, h

    h_init = h0.astype(f32).reshape(bsz, ng, hg, hp, ns)
    h_fin, h_in = lax.scan(step, h_init, (jnp.moveaxis(chunk_states, 1, 0), jnp.moveaxis(chunk_decay, 1, 0)))
    h_in = jnp.moveaxis(h_in, 0, 1)
    y_off = jnp.einsum('bclgn,bcghpn,bclgh->bclghp', cc, h_in, jnp.exp(a_cs))
    y = (y_diag + y_off).reshape(bsz, nc * lc, nh, hp)[:, :t]
    return y, h_fin.reshape(bsz, nh, hp, ns).astype(h0.dtype)


def moba_attention(q, k, v, q_start):
    bsz, tq = q.shape[:2]
    s_len = k.shape[1]
    qb = min(MOBA_QCHUNK, tq)
    nqc = -(-tq // qb)
    tq_pad = nqc * qb
    nb = -(-(q_start + tq_pad) // MOBA_BLOCK)
    kpad = ((0, 0), (0, nb * MOBA_BLOCK - s_len), (0, 0), (0, 0))
    kb = jnp.pad(k, kpad).reshape(bsz, nb, MOBA_BLOCK, ATTN_KV_HEADS, HEAD_DIM).transpose(0, 1, 3, 2, 4)
    vb = jnp.pad(v, kpad).reshape(bsz, nb, MOBA_BLOCK, ATTN_KV_HEADS, HEAD_DIM).transpose(0, 1, 3, 2, 4)
    kmean = jnp.mean(kb.astype(jnp.float32), axis=3)
    q_all = jnp.pad(q, ((0, 0), (0, tq_pad - tq), (0, 0), (0, 0)))
    q_all = jnp.moveaxis(q_all.reshape(bsz, nqc, qb, ATTN_KV_HEADS, ATTN_GROUP, HEAD_DIM), 1, 0)
    starts = q_start + qb * jnp.arange(nqc, dtype=jnp.int32)
    n_sel = min(MOBA_TOPK, nb)
    n_gat = n_sel * MOBA_BLOCK
    scale = HEAD_DIM ** -0.5
    b_idx = jnp.arange(bsz)[:, None, None, None, None]
    h_idx = jnp.arange(ATTN_KV_HEADS)[None, None, :, None, None]
    blk_ids = jnp.arange(nb)
    offs = jnp.arange(MOBA_BLOCK)

    def one_chunk(args):
        qc, c0 = args
        qpos = c0 + jnp.arange(qb)
        own = c0 // MOBA_BLOCK
        gate = jnp.einsum('bqkgd,bnkd->bqkgn', qc.astype(jnp.float32), kmean)
        gate = jnp.where(blk_ids < own, gate, -jnp.inf)
        _, sel = lax.top_k(gate, n_sel)
        valid = sel < own
        k_sel = kb[b_idx, sel, h_idx]
        v_sel = vb[b_idx, sel, h_idx]
        s_sel = jnp.einsum('bqkgd,bqkgjsd->bqkgjs', qc, k_sel).astype(jnp.float32) * scale
        s_sel = jnp.where(valid[..., None], s_sel, -jnp.inf).reshape(bsz, qb, ATTN_KV_HEADS, ATTN_GROUP, n_gat)
        k_own = lax.dynamic_index_in_dim(kb, own, axis=1, keepdims=False)
        v_own = lax.dynamic_index_in_dim(vb, own, axis=1, keepdims=False)
        s_own = jnp.einsum('bqkgd,bksd->bqkgs', qc, k_own).astype(jnp.float32) * scale
        causal = (own * MOBA_BLOCK + offs)[None, :] <= qpos[:, None]
        s_own = jnp.where(causal[None, :, None, None, :], s_own, -jnp.inf)
        probs = jax.nn.softmax(jnp.concatenate([s_sel, s_own], axis=-1), axis=-1).astype(v.dtype)
        p_sel = probs[..., :n_gat].reshape(bsz, qb, ATTN_KV_HEADS, ATTN_GROUP, n_sel, MOBA_BLOCK)
        return (jnp.einsum('bqkgjs,bqkgjsd->bqkgd', p_sel, v_sel)
                + jnp.einsum('bqkgs,bksd->bqkgd', probs[..., n_gat:], v_own))

    o = lax.map(one_chunk, (q_all, starts))
    return jnp.moveaxis(o, 0, 1).reshape(bsz, tq_pad, ATTN_WIDTH)[:, :tq]


def token_mixers(u, lp, conv_buf, pool_buf, ssm_h0, past_k, past_v, start):
    bsz, t, _ = u.shape
    f32 = jnp.float32
    proj = u @ lp['w_in']
    p_in, z, xbc, dt_raw, q, k, v, gate = split_cols(proj, IN_SPLITS)
    pool_o, pool_new = pool_mixer(p_in, pool_buf, start, lp['w_pool_grp'], lp['pool_scale'])
    xbc, conv_new = causal_conv(xbc, conv_buf, lp['conv_w'], lp['conv_b'])
    xs, bm, cm = jnp.split(xbc, [SSD_INNER, SSD_INNER + SSD_GROUPS * SSD_STATE], axis=-1)
    xs = xs.reshape(bsz, t, SSD_HEADS, SSD_HEADDIM)
    dt = jax.nn.softplus(dt_raw.astype(f32) + lp['dt_bias'].astype(f32))
    a = -jnp.exp(lp['a_log'].astype(f32))
    y, ssm_new = ssd_scan(xs, dt, a, bm.reshape(bsz, t, SSD_GROUPS, SSD_STATE),
                          cm.reshape(bsz, t, SSD_GROUPS, SSD_STATE), ssm_h0)
    y = y + lp['d_skip'].astype(f32)[:, None] * xs.astype(f32)
    y = y.reshape(bsz, t, SSD_INNER) * jax.nn.silu(z.astype(f32))
    yg = y.reshape(bsz, t, SSD_GROUPS, SSD_INNER // SSD_GROUPS)
    yg = yg * lax.rsqrt(jnp.mean(yg * yg, axis=-1, keepdims=True) + NORM_EPS)
    ssd_o = (yg.reshape(bsz, t, SSD_INNER) * lp['ssd_norm'].astype(f32)).astype(u.dtype)
    q = q.reshape(bsz, t, ATTN_HEADS, HEAD_DIM)
    k = k.reshape(bsz, t, ATTN_KV_HEADS, HEAD_DIM)
    v = v.reshape(bsz, t, ATTN_KV_HEADS, HEAD_DIM)
    k_all = k if past_k is None else jnp.concatenate([past_k.astype(k.dtype), k], axis=1)
    v_all = v if past_v is None else jnp.concatenate([past_v.astype(v.dtype), v], axis=1)
    attn_o = moba_attention(q, k_all, v_all, start)
    g = jax.nn.sigmoid(gate.astype(f32)).astype(u.dtype).reshape(bsz, t, N_BRANCHES, D_MODEL)
    merged = (g[:, :, 0] * (pool_o @ lp['w_pool_br'])
              + g[:, :, 1] * (ssd_o @ lp['w_ssd_br'])
              + g[:, :, 2] * (attn_o @ lp['w_attn_br']))
    return merged @ lp['w_out'], (k, v, ssm_new, conv_new, pool_new)


def trunk_layer(x, c, lp, conv_buf, pool_buf, ssm_h0, past_k, past_v, start):
    mod = jax.nn.silu(c) @ lp['w_ada'] + lp['b_ada']
    sh1, sc1, gt1, sh2, sc2, gt2 = [m[:, None, :] for m in jnp.split(mod, 6, axis=-1)]
    u = rms_norm(x, lp['g_pre_mix']) * (1.0 + sc1) + sh1
    o, st = token_mixers(u, lp, conv_buf, pool_buf, ssm_h0, past_k, past_v, start)
    x = x + gt1 * rms_norm(o, lp['g_post_mix'])
    u = rms_norm(x, lp['g_pre_ffn']) * (1.0 + sc2) + sh2
    f = (jax.nn.silu(u @ lp['w_gate']) * (u @ lp['w_up'])) @ lp['w_down']
    x = x + gt2 * rms_norm(f, lp['g_post_ffn'])
    return x, st


def setup_inputs(seed: int = 0) -> dict:
    key = jax.random.key(seed)
    keys = jax.random.split(key, 40)
    counter = [0]

    def nk():
        counter[0] += 1
        return keys[counter[0] - 1]

    def nrm(shape, s):
        return jax.random.normal(nk(), shape, jnp.float32) * s

    def gain(shape):
        return 1.0 + nrm(shape, 0.05)

    n_pages = PAST_LEN // PAGE_SIZE
    n_used = DEC_BATCH * n_pages
    n_pool = n_used + n_used // 4
    page_table = jax.random.permutation(nk(), n_pool)[:n_used].reshape(DEC_BATCH, n_pages).astype(jnp.int32)
    dt0 = jnp.exp(jax.random.uniform(nk(), (DEPTH, SSD_HEADS), jnp.float32, math.log(1e-3), math.log(1e-1)))
    dt_bias = dt0 + jnp.log(-jnp.expm1(-dt0))
    a_log = jnp.log(jax.random.uniform(nk(), (DEPTH, SSD_HEADS), jnp.float32, 1.0, 16.0))
    d_in = D_MODEL ** -0.5
    return {
        'x_prompt': nrm((BATCH, SEQ, D_MODEL), 1.0),
        'x_sample': nrm((DEC_BATCH, DEC_SEQ, D_MODEL), 1.0),
        'cache_k': nrm((DEPTH, n_pool, PAGE_SIZE, ATTN_KV_HEADS, HEAD_DIM), 1.0),
        'cache_v': nrm((DEPTH, n_pool, PAGE_SIZE, ATTN_KV_HEADS, HEAD_DIM), 1.0),
        'state_ssm': nrm((DEPTH, DEC_BATCH, SSD_HEADS, SSD_HEADDIM, SSD_STATE), 0.5),
        'state_conv': nrm((DEPTH, DEC_BATCH, SSD_CONV - 1, SSD_CONV_DIM), 1.0),
        'state_pool': nrm((DEPTH, DEC_BATCH, POOL_BUF, POOL_WIDTH), 1.0),
        'page_table': page_table,
        'c_prompt': nrm((BATCH, D_MODEL), 1.0),
        'c_sample': nrm((DEC_BATCH, D_MODEL), 1.0),
        'w_ada': nrm((DEPTH, D_MODEL, 6 * D_MODEL), 0.5 * d_in),
        'b_ada': nrm((DEPTH, 6 * D_MODEL), 0.02),
        'g_pre_mix': gain((DEPTH, D_MODEL)),
        'g_post_mix': gain((DEPTH, D_MODEL)),
        'g_pre_ffn': gain((DEPTH, D_MODEL)),
        'g_post_ffn': gain((DEPTH, D_MODEL)),
        'w_in': nrm((DEPTH, D_MODEL, IN_WIDTH), d_in),
        'w_pool_grp': nrm((DEPTH, N_POOL_GROUPS, POOL_GROUP, POOL_GROUP), POOL_GROUP ** -0.5),
        'pool_scale': 1.0 + nrm((DEPTH, POOL_WIDTH), 0.1),
        'conv_w': nrm((DEPTH, SSD_CONV, SSD_CONV_DIM), SSD_CONV ** -0.5),
        'conv_b': nrm((DEPTH, SSD_CONV_DIM), 0.02),
        'dt_bias': dt_bias,
        'a_log': a_log,
        'd_skip': 1.0 + nrm((DEPTH, SSD_HEADS), 0.1),
        'ssd_norm': gain((DEPTH, SSD_INNER)),
        'w_pool_br': nrm((DEPTH, POOL_WIDTH, D_MODEL), POOL_WIDTH ** -0.5),
        'w_ssd_br': nrm((DEPTH, SSD_INNER, D_MODEL), SSD_INNER ** -0.5),
        'w_attn_br': nrm((DEPTH, ATTN_WIDTH, D_MODEL), ATTN_WIDTH ** -0.5),
        'w_out': nrm((DEPTH, D_MODEL, D_MODEL), d_in),
        'w_gate': nrm((DEPTH, D_MODEL, D_FF), d_in),
        'w_up': nrm((DEPTH, D_MODEL, D_FF), d_in),
        'w_down': nrm((DEPTH, D_FF, D_MODEL), D_FF ** -0.5),
    }


def reference(x_prompt, x_sample, cache_k, cache_v, state_ssm, state_conv, state_pool, page_table,
              c_prompt, c_sample, w_ada, b_ada, g_pre_mix, g_post_mix, g_pre_ffn, g_post_ffn,
              w_in, w_pool_grp, pool_scale, conv_w, conv_b, dt_bias, a_log, d_skip, ssd_norm,
              w_pool_br, w_ssd_br, w_attn_br, w_out, w_gate, w_up, w_down):
    bsz = x_prompt.shape[0]
    dec_b = x_sample.shape[0]
    past_len = page_table.shape[1] * cache_k.shape[2]
    xp, xs = x_prompt, x_sample
    kp, vp, hp_, cp, pp = [], [], [], [], []
    ks, vs, hs, cs_, ps = [], [], [], [], []
    for l in range(DEPTH):
        lp = {
            'w_ada': w_ada[l], 'b_ada': b_ada[l],
            'g_pre_mix': g_pre_mix[l], 'g_post_mix': g_post_mix[l],
            'g_pre_ffn': g_pre_ffn[l], 'g_post_ffn': g_post_ffn[l],
            'w_in': w_in[l], 'w_pool_grp': w_pool_grp[l], 'pool_scale': pool_scale[l],
            'conv_w': conv_w[l], 'conv_b': conv_b[l], 'dt_bias': dt_bias[l], 'a_log': a_log[l],
            'd_skip': d_skip[l], 'ssd_norm': ssd_norm[l],
            'w_pool_br': w_pool_br[l], 'w_ssd_br': w_ssd_br[l], 'w_attn_br': w_attn_br[l],
            'w_out': w_out[l], 'w_gate': w_gate[l], 'w_up': w_up[l], 'w_down': w_down[l],
        }
        xp, st_p = trunk_layer(
            xp, c_prompt, lp,
            jnp.zeros((bsz, SSD_CONV - 1, SSD_CONV_DIM), xp.dtype),
            jnp.zeros((bsz, POOL_BUF, POOL_WIDTH), xp.dtype),
            jnp.zeros((bsz, SSD_HEADS, SSD_HEADDIM, SSD_STATE), state_ssm.dtype),
            None, None, 0)
        past_k = cache_k[l, page_table].reshape(dec_b, past_len, ATTN_KV_HEADS, HEAD_DIM)
        past_v = cache_v[l, page_table].reshape(dec_b, past_len, ATTN_KV_HEADS, HEAD_DIM)
        xs, st_s = trunk_layer(xs, c_sample, lp, state_conv[l], state_pool[l], state_ssm[l],
                               past_k, past_v, past_len)
        kp.append(st_p[0]); vp.append(st_p[1]); hp_.append(st_p[2]); cp.append(st_p[3]); pp.append(st_p[4])
        ks.append(st_s[0]); vs.append(st_s[1]); hs.append(st_s[2]); cs_.append(st_s[3]); ps.append(st_s[4])
    k_prompt, v_prompt = jnp.stack(kp), jnp.stack(vp)
    ssm_prompt, conv_prompt, pool_prompt = jnp.stack(hp_), jnp.stack(cp), jnp.stack(pp)
    k_sample, v_sample = jnp.stack(ks), jnp.stack(vs)
    ssm_sample, conv_sample, pool_sample = jnp.stack(hs), jnp.stack(cs_), jnp.stack(ps)
    return (xp, xs, k_prompt, v_prompt, ssm_prompt, conv_prompt, pool_prompt,
            k_sample, v_sample, ssm_sample, conv_sample, pool_sample)
```

```python
import functools

import jax
import jax.numpy as jnp
from jax import lax
from jax.experimental import pallas as pl
from jax.experimental.pallas import tpu as pltpu

F32 = jnp.float32
BF16 = jnp.bfloat16

NORM_EPS = 1e-6
POOL_WINDOWS = (2, 4, 8, 16)
POOL_HIST = 16
SSD_HEADDIM = 64
SSD_GROUPS = 8
SSD_STATE = 128
SSD_CONV = 4
SSD_CHUNK = 128
CONV_HIST = 8
ATTN_HEADS = 16
ATTN_KV_HEADS = 8
ATTN_GROUP = ATTN_HEADS // ATTN_KV_HEADS
MOBA_BLOCK = 256
MOBA_TOPK = 3
LANES = 128
VMEM_LIMIT_BYTES = 56 * 1024 * 1024
NEG_INF = float("-inf")


def _cparams(*sem):
    return pltpu.CompilerParams(dimension_semantics=sem, vmem_limit_bytes=VMEM_LIMIT_BYTES)


def _sigmoid(x):
    return 1.0 / (1.0 + jnp.exp(-x))


def _silu(x):
    return x * _sigmoid(x)


def _dot(a, b):
    return jnp.dot(a, b, preferred_element_type=F32)


def _dot_nt(a, b):
    return lax.dot_general(a, b, (((1,), (1,)), ((), ())), preferred_element_type=F32)


def _dot_tn(a, b):
    return lax.dot_general(a, b, (((0,), (0,)), ((), ())), preferred_element_type=F32)


def _rms(x):
    return x * lax.rsqrt(jnp.mean(x * x, axis=-1, keepdims=True) + NORM_EPS)


def _pick(total, pref):
    if total <= pref:
        return total
    t = pref
    while total % t:
        t //= 2
    return t


def _ada_kernel(c_ref, w_ref, b_ref, o_ref):
    a = _silu(c_ref[...]).astype(BF16)
    o_ref[...] = _dot(a, w_ref[...].astype(BF16)) + b_ref[...]


def _ada(c_all, w_ada, b_ada):
    depth, d, n = w_ada.shape
    rows = c_all.shape[0]
    tn = _pick(n, 1024)
    return pl.pallas_call(
        _ada_kernel,
        grid=(depth, n // tn),
        in_specs=[pl.BlockSpec((rows, d), lambda l, j: (0, 0)),
                  pl.BlockSpec((None, d, tn), lambda l, j: (l, 0, j)),
                  pl.BlockSpec((None, 1, tn), lambda l, j: (l, 0, j))],
        out_specs=pl.BlockSpec((None, rows, tn), lambda l, j: (l, 0, j)),
        out_shape=jax.ShapeDtypeStruct((depth, rows, n), F32),
        compiler_params=_cparams("parallel", "parallel"),
        name="ada_mod",
    )(c_all, w_ada, b_ada.reshape(depth, 1, n))


def _norm_mod_kernel(x_ref, g_ref, sc_ref, sh_ref, u_ref):
    y = _rms(x_ref[...]) * g_ref[...]
    u_ref[...] = (y * (1.0 + sc_ref[...]) + sh_ref[...]).astype(BF16)


def _norm_mod(x, g, sc, sh):
    b, t, d = x.shape
    r = sc.shape[1]
    tt = _pick(t, 512)
    mod_spec = pl.BlockSpec((None, r if r == 1 else tt, d),
                            (lambda i, j: (i, j, 0)) if r == t else (lambda i, j: (i, 0, 0)))
    return pl.pallas_call(
        _norm_mod_kernel,
        grid=(b, t // tt),
        in_specs=[pl.BlockSpec((None, tt, d), lambda i, j: (i, j, 0)),
                  pl.BlockSpec((1, d), lambda i, j: (0, 0)),
                  mod_spec, mod_spec],
        out_specs=pl.BlockSpec((None, tt, d), lambda i, j: (i, j, 0)),
        out_shape=jax.ShapeDtypeStruct((b, t, d), BF16),
        compiler_params=_cparams("parallel", "parallel"),
        name="norm_mod",
    )(x, g, sc, sh)


def _mm_kernel(a_ref, w_ref, o_ref):
    o_ref[...] = _dot(a_ref[...], w_ref[...])


def _matmul(a, w, layer):
    m, k = a.shape
    n = w.shape[2]
    tm = _pick(m, 1024)
    tn = _pick(n, 1024)
    return pl.pallas_call(
        _mm_kernel,
        grid=(n // tn, m // tm),
        in_specs=[pl.BlockSpec((tm, k), lambda j, i: (i, 0)),
                  pl.BlockSpec((None, k, tn), lambda j, i: (layer, 0, j))],
        out_specs=pl.BlockSpec((tm, tn), lambda j, i: (i, j)),
        out_shape=jax.ShapeDtypeStruct((m, n), F32),
        compiler_params=_cparams("parallel", "parallel"),
        name="proj_matmul",
    )(a, w)


def _pool_kernel(p_ref, buf_ref, w_ref, sc_ref, o_ref, ext_ref, *, tt, start, pg):
    t = pl.program_id(1)

    @pl.when(t == 0)
    def _():
        ext_ref[0:POOL_HIST, :] = buf_ref[...]

    ext_ref[POOL_HIST:POOL_HIST + tt, :] = p_ref[...]
    n_valid = lax.broadcasted_iota(jnp.int32, (tt, 1), 0) + (t * tt + start + 1)
    for gi, w in enumerate(POOL_WINDOWS):
        c0 = gi * pg
        x = ext_ref[POOL_HIST:POOL_HIST + tt, c0:c0 + pg]
        s = x
        for k in range(1, w):
            s = s + ext_ref[POOL_HIST - k:POOL_HIST - k + tt, c0:c0 + pg]
        cnt = jnp.minimum(n_valid, w).astype(F32)
        mixed = (s / cnt - x).astype(BF16)
        y = _dot(mixed, w_ref[gi]) * sc_ref[:, c0:c0 + pg]
        o_ref[:, c0:c0 + pg] = y.astype(BF16)
    ext_ref[0:POOL_HIST, :] = ext_ref[tt:tt + POOL_HIST, :]


def _pool(proj3, buf, w_grp, scale, layer, start):
    b, t, _ = proj3.shape
    c = scale.shape[1]
    pg = c // len(POOL_WINDOWS)
    tt = _pick(t, 256)
    return pl.pallas_call(
        functools.partial(_pool_kernel, tt=tt, start=start, pg=pg),
        grid=(b, t // tt),
        in_specs=[pl.BlockSpec((None, tt, c), lambda i, j: (i, j, 0)),
                  pl.BlockSpec((None, POOL_HIST, c), lambda i, j: (i, 0, 0)),
                  pl.BlockSpec((None, len(POOL_WINDOWS), pg, pg), lambda i, j: (layer, 0, 0, 0)),
                  pl.BlockSpec((1, c), lambda i, j: (0, 0))],
        out_specs=pl.BlockSpec((None, tt, c), lambda i, j: (i, j, 0)),
        out_shape=jax.ShapeDtypeStruct((b, t, c), BF16),
        scratch_shapes=[pltpu.VMEM((POOL_HIST + tt, c), F32)],
        compiler_params=_cparams("parallel", "arbitrary"),
        name="pool_mixer",
    )(proj3, buf, w_grp, scale)


def _ssd_kernel(z_ref, x_ref, b_ref, c_ref, dt_ref, cpx_ref, cpb_ref, cpc_ref,
                cwx_ref, cwb_ref, cwc_ref, cbx_ref, cbb_ref, cbc_ref,
                dtb_ref, alog_ref, dsk_ref, nrm_ref, h0_ref,
                y_ref, hout_ref, extx, extb, extc, ht, *, t_valid, n_chunks, hg):
    g = pl.program_id(1)
    c = pl.program_id(2)
    lc = SSD_CHUNK
    gw = hg * SSD_HEADDIM

    @pl.when(c == 0)
    def _():
        extx[0:CONV_HIST, :] = cpx_ref[...]
        extb[0:CONV_HIST, :] = cpb_ref[...]
        extc[0:CONV_HIST, :] = cpc_ref[...]
        ht[...] = h0_ref[...].T

    def conv(ext, raw_ref, w_ref, bias_ref):
        ext[CONV_HIST:CONV_HIST + lc, :] = raw_ref[...]
        acc = bias_ref[...]
        for j in range(SSD_CONV):
            r0 = CONV_HIST - (SSD_CONV - 1) + j
            acc = acc + ext[r0:r0 + lc, :] * w_ref[j:j + 1, :]
        ext[0:CONV_HIST, :] = ext[lc:lc + CONV_HIST, :]
        return _silu(acc)

    xs = conv(extx, x_ref, cwx_ref, cbx_ref)
    bm = conv(extb, b_ref, cwb_ref, cbb_ref)
    cm = conv(extc, c_ref, cwc_ref, cbc_ref)

    row = lax.broadcasted_iota(jnp.int32, (lc, LANES), 0)
    col = lax.broadcasted_iota(jnp.int32, (lc, LANES), 1)
    xdt = dt_ref[...] + dtb_ref[...]
    dt = jnp.maximum(xdt, 0.0) + jnp.log1p(jnp.exp(-jnp.abs(xdt)))
    dt = jnp.where(c * lc + row < t_valid, dt, 0.0)
    dta = dt * (-jnp.exp(alog_ref[...]))
    acs = dta
    k = 1
    while k < lc:
        acs = acs + jnp.where(row >= k, pltpu.roll(acs, k, axis=0), 0.0)
        k *= 2
    a_last = acs[lc - 1:lc, :]
    w_end = dt * jnp.exp(a_last - acs)
    exp_a = jnp.exp(acs)

    hrow = lax.broadcasted_iota(jnp.int32, (LANES, gw), 0)
    hcol = lax.broadcasted_iota(jnp.int32, (LANES, gw), 1)
    expand = hrow == g * hg + hcol // SSD_HEADDIM
    stacked = jnp.concatenate([dt, w_end, exp_a], axis=0).astype(BF16)
    wide = _dot(stacked, jnp.where(expand, 1.0, 0.0).astype(BF16))
    dt_e, w_e, ea_e = wide[0:lc], wide[lc:2 * lc], wide[2 * lc:3 * lc]
    cd_e = jnp.dot(jnp.broadcast_to(jnp.exp(a_last), (8, LANES)), jnp.where(expand, 1.0, 0.0),
                   precision=lax.Precision.HIGHEST, preferred_element_type=F32)[0:1, :]

    acs_g = pltpu.roll(acs, (LANES - g * hg) % LANES, axis=1)
    acs_gt = acs_g.T

    bm16 = bm.astype(BF16)
    cm16 = cm.astype(BF16)
    cb = _dot_nt(cm16, bm16)
    causal = row >= col
    xdt_e = xs * dt_e
    y_parts = []
    for pr in range(gw // LANES):
        xp = xdt_e[:, pr * LANES:(pr + 1) * LANES]
        acc = None
        heads_here = LANES // SSD_HEADDIM
        for hh in range(heads_here):
            h = pr * heads_here + hh
            seg = acs_g[:, h:h + 1] - acs_gt[h:h + 1, :]
            decay = jnp.exp(jnp.where(causal, seg, NEG_INF))
            mh = (cb * decay).astype(BF16)
            in_head = (col >= hh * SSD_HEADDIM) & (col < (hh + 1) * SSD_HEADDIM)
            part = _dot(mh, jnp.where(in_head, xp, 0.0).astype(BF16))
            acc = part if acc is None else acc + part
        y_parts.append(acc)
    y = jnp.concatenate(y_parts, axis=1)

    h_in = ht[...]
    y = y + _dot(cm16, h_in.astype(BF16)) * ea_e
    h_new = h_in * cd_e + _dot(bm.T.astype(BF16), (xs * w_e).astype(BF16))
    ht[...] = h_new

    y = y + dsk_ref[...] * xs
    y = y * _silu(z_ref[...])
    y_ref[...] = (_rms(y) * nrm_ref[...]).astype(BF16)

    @pl.when(c == n_chunks - 1)
    def _():
        hout_ref[...] = h_new.T


def _ssd(proj3, dt3, conv_prev, h0, conv_w, conv_b, dt_bias, a_log, d_skip_e, ssd_norm, col_z, col_xbc, t_valid):
    b, tp, _ = proj3.shape
    inner = ssd_norm.shape[1]
    n_heads = inner // SSD_HEADDIM
    hg = n_heads // SSD_GROUPS
    gw = hg * SSD_HEADDIM
    ns = SSD_STATE
    lc = SSD_CHUNK
    n_chunks = tp // lc
    zb, xb = col_z // gw, col_xbc // gw
    bb = (col_xbc + inner) // ns
    cb_ = (col_xbc + inner + SSD_GROUPS * ns) // ns
    cxb, cbb, ccb = 0, inner // ns, (inner + SSD_GROUPS * ns) // ns

    def seq(width, off):
        return pl.BlockSpec((None, lc, width), lambda i, g, c: (i, c, off + g))

    def prev(width, off):
        return pl.BlockSpec((None, CONV_HIST, width), lambda i, g, c: (i, 0, off + g))

    def par(rows, width, off):
        return pl.BlockSpec((rows, width), lambda i, g, c: (0, off + g))

    state_spec = pl.BlockSpec((None, None, gw, ns), lambda i, g, c: (i, g, 0, 0))
    kernel = functools.partial(_ssd_kernel, t_valid=t_valid, n_chunks=n_chunks, hg=hg)
    return pl.pallas_call(
        kernel,
        grid=(b, SSD_GROUPS, n_chunks),
        in_specs=[seq(gw, zb), seq(gw, xb), seq(ns, bb), seq(ns, cb_),
                  pl.BlockSpec((None, lc, LANES), lambda i, g, c: (i, c, 0)),
                  prev(gw, cxb), prev(ns, cbb), prev(ns, ccb),
                  par(SSD_CONV, gw, cxb), par(SSD_CONV, ns, cbb), par(SSD_CONV, ns, ccb),
                  par(1, gw, cxb), par(1, ns, cbb), par(1, ns, ccb),
                  pl.BlockSpec((1, LANES), lambda i, g, c: (0, 0)),
                  pl.BlockSpec((1, LANES), lambda i, g, c: (0, 0)),
                  par(1, gw, 0), par(1, gw, 0),
                  state_spec],
        out_specs=[pl.BlockSpec((None, lc, gw), lambda i, g, c: (i, c, g)), state_spec],
        out_shape=[jax.ShapeDtypeStruct((b, tp, inner), BF16),
                   jax.ShapeDtypeStruct((b, SSD_GROUPS, gw, ns), F32)],
        scratch_shapes=[pltpu.VMEM((CONV_HIST + lc, gw), F32),
                        pltpu.VMEM((CONV_HIST + lc, ns), F32),
                        pltpu.VMEM((CONV_HIST + lc, ns), F32),
                        pltpu.VMEM((ns, gw), F32)],
        compiler_params=_cparams("parallel", "parallel", "arbitrary"),
        name="ssd_mixer",
    )(proj3, proj3, proj3, proj3, dt3, conv_prev, conv_prev, conv_prev,
      conv_w, conv_w, conv_w, conv_b, conv_b, conv_b, dt_bias, a_log, d_skip_e, ssd_norm, h0)


def _top_k_mask(gate, valid, ids, axis):
    big = jnp.int32(2 ** 30)
    remaining = valid
    sel = jnp.zeros(gate.shape, F32)
    for _ in range(MOBA_TOPK):
        gm = jnp.where(remaining > 0.5, gate, NEG_INF)
        m = jnp.max(gm, axis=axis, keepdims=True)
        cand = jnp.where(gm == m, remaining, 0.0)
        first = jnp.min(jnp.where(cand > 0.5, ids, big), axis=axis, keepdims=True)
        pick = jnp.where(ids == first, 1.0, 0.0)
        sel = sel + pick
        remaining = remaining - pick
    return sel


def _moba_prompt_kernel(q_ref, k_ref, v_ref, o_ref, kb, vb, km, sel_ref, *, n_blocks, scale):
    i = pl.program_id(2)
    blk = MOBA_BLOCK
    hd = LANES

    @pl.when(i == 0)
    def _():
        kb[...] = k_ref[...].astype(BF16)
        vb[...] = v_ref[...].astype(BF16)
        km[...] = jnp.zeros(km.shape, F32)
        for j in range(n_blocks):
            km[j:j + 1, :] = jnp.mean(k_ref[j * blk:(j + 1) * blk, :], axis=0, keepdims=True)

    q = q_ref[...]
    qs = jnp.concatenate([q[:, g * hd:(g + 1) * hd] for g in range(ATTN_GROUP)], axis=0).astype(BF16)
    rows = ATTN_GROUP * blk
    gate = _dot_nt(qs, km[...].astype(BF16))
    lane = lax.broadcasted_iota(jnp.int32, (rows, LANES), 1)
    valid = jnp.where(lane < i, 1.0, 0.0)
    sel_ref[...] = _top_k_mask(gate, valid, lane, axis=1)

    start = pl.multiple_of(i * blk, blk)
    s = _dot_nt(qs, kb[pl.ds(start, blk), :]) * scale
    r = lax.broadcasted_iota(jnp.int32, (rows, blk), 0)
    cidx = lax.broadcasted_iota(jnp.int32, (rows, blk), 1)
    s = jnp.where(cidx <= (r & (blk - 1)), s, NEG_INF)
    m = jnp.max(s, axis=-1, keepdims=True)
    p = jnp.exp(s - m)
    l = jnp.sum(p, axis=-1, keepdims=True)
    acc = _dot(p.astype(BF16), vb[pl.ds(start, blk), :])

    def body(j, carry):
        m, l, acc = carry
        off = pl.multiple_of(j * blk, blk)
        s = _dot_nt(qs, kb[pl.ds(off, blk), :]) * scale
        picked = jnp.sum(jnp.where(lane == j, sel_ref[...], 0.0), axis=-1, keepdims=True)
        s = jnp.where(picked > 0.5, s, NEG_INF)
        m_new = jnp.maximum(m, jnp.max(s, axis=-1, keepdims=True))
        alpha = jnp.exp(m - m_new)
        p = jnp.exp(s - m_new)
        l = alpha * l + jnp.sum(p, axis=-1, keepdims=True)
        acc = alpha * acc + _dot(p.astype(BF16), vb[pl.ds(off, blk), :])
        return m_new, l, acc

    m, l, acc = lax.fori_loop(0, i, body, (m, l, acc))
    o = acc / l
    o_ref[...] = jnp.concatenate([o[g * blk:(g + 1) * blk] for g in range(ATTN_GROUP)], axis=1).astype(BF16)


def _moba_prompt(proj3, k3, v3, col_q):
    b, t, _ = proj3.shape
    hd = LANES
    blk = MOBA_BLOCK
    n_blocks = t // blk
    qw = ATTN_GROUP * hd
    qb = col_q // qw
    kernel = functools.partial(_moba_prompt_kernel, n_blocks=n_blocks, scale=hd ** -0.5)
    return pl.pallas_call(
        kernel,
        grid=(b, ATTN_KV_HEADS, n_blocks),
        in_specs=[pl.BlockSpec((None, blk, qw), lambda bi, h, i: (bi, i, qb + h)),
                  pl.BlockSpec((None, t, hd), lambda bi, h, i: (bi, 0, h)),
                  pl.BlockSpec((None, t, hd), lambda bi, h, i: (bi, 0, h))],
        out_specs=pl.BlockSpec((None, blk, qw), lambda bi, h, i: (bi, i, h)),
        out_shape=jax.ShapeDtypeStruct((b, t, ATTN_HEADS * hd), BF16),
        scratch_shapes=[pltpu.VMEM((t, hd), BF16), pltpu.VMEM((t, hd), BF16),
                        pltpu.VMEM((LANES, hd), F32), pltpu.VMEM((ATTN_GROUP * blk, LANES), F32)],
        compiler_params=_cparams("parallel", "parallel", "arbitrary"),
        name="moba_prompt",
    )(proj3, k3, v3)


def _moba_scores_kernel(pt_ref, wq_ref, *refs, pps, page):
    del pt_ref
    k_refs, o_ref = refs[:pps], refs[pps]
    hd = LANES
    for r in range(pps):
        acc = None
        for h in range(ATTN_KV_HEADS):
            kh = k_refs[r][pl.ds(h, page, stride=ATTN_KV_HEADS), :].astype(BF16)
            part = _dot(kh, wq_ref[h * hd:(h + 1) * hd, :])
            acc = part if acc is None else acc + part
        o_ref[r * page:(r + 1) * page, :] = acc


def _moba_scores(cache4, layer, page_table, wq_t, pps):
    b, n_pages = page_table.shape
    rows, hd = cache4.shape[2], cache4.shape[3]
    page, width = rows // ATTN_KV_HEADS, hd * ATTN_KV_HEADS

    def page_spec(r):
        return pl.BlockSpec((None, None, rows, hd), lambda bi, p, pt: (layer, pt[bi, p * pps + r], 0, 0))

    grid_spec = pltpu.PrefetchScalarGridSpec(
        num_scalar_prefetch=1,
        grid=(b, n_pages // pps),
        in_specs=[pl.BlockSpec((None, width, LANES), lambda bi, p, pt: (bi, 0, 0))]
                 + [page_spec(r) for r in range(pps)],
        out_specs=pl.BlockSpec((None, pps * page, LANES), lambda bi, p, pt: (bi, p, 0)),
    )
    return pl.pallas_call(
        functools.partial(_moba_scores_kernel, pps=pps, page=page),
        grid_spec=grid_spec,
        out_shape=jax.ShapeDtypeStruct((b, n_pages * page, LANES), F32),
        compiler_params=_cparams("parallel", "arbitrary"),
        name="moba_sample_scores",
    )(page_table, wq_t, *([cache4] * pps))


def _moba_softmax_kernel(st_ref, kn_ref, wq_ref, p_ref, gate_ref, sel_ref, *, n_blocks, tq, scale):
    blk = MOBA_BLOCK
    past = n_blocks * blk
    gate_ref[...] = jnp.zeros(gate_ref.shape, F32)

    def gate_body(j, carry):
        off = pl.multiple_of(j * blk, blk)
        gate_ref[pl.ds(j, 1), :] = jnp.mean(st_ref[pl.ds(off, blk), :], axis=0, keepdims=True)
        return carry

    lax.fori_loop(0, n_blocks, gate_body, 0)
    nrow = gate_ref.shape[0]
    ids = lax.broadcasted_iota(jnp.int32, (nrow, LANES), 0)
    valid = jnp.where(ids < n_blocks, 1.0, 0.0)
    sel_ref[...] = _top_k_mask(gate_ref[...], valid, ids, axis=0)

    s_own = _dot(kn_ref[...].astype(BF16), wq_ref[...]) * scale
    key_t = lax.broadcasted_iota(jnp.int32, (tq, LANES), 0)
    qry_t = lax.broadcasted_iota(jnp.int32, (tq, LANES), 1) & (tq - 1)
    s_own = jnp.where(key_t <= qry_t, s_own, NEG_INF)
    m0 = jnp.max(s_own, axis=0, keepdims=True)

    def max_body(j, m):
        off = pl.multiple_of(j * blk, blk)
        bm = jnp.max(st_ref[pl.ds(off, blk), :], axis=0, keepdims=True) * scale
        return jnp.maximum(m, jnp.where(sel_ref[pl.ds(j, 1), :] > 0.5, bm, NEG_INF))

    m = lax.fori_loop(0, n_blocks, max_body, m0)
    p_own = jnp.exp(s_own - m)
    l0 = jnp.sum(p_own, axis=0, keepdims=True)

    def exp_body(j, l):
        off = pl.multiple_of(j * blk, blk)
        s = st_ref[pl.ds(off, blk), :] * scale
        p = jnp.where(sel_ref[pl.ds(j, 1), :] > 0.5, jnp.exp(s - m), 0.0)
        p_ref[pl.ds(off, blk), :] = p
        return l + jnp.sum(p, axis=0, keepdims=True)

    l = lax.fori_loop(0, n_blocks, exp_body, l0)
    inv = 1.0 / l

    def norm_body(j, carry):
        off = pl.multiple_of(j * blk, blk)
        p_ref[pl.ds(off, blk), :] = p_ref[pl.ds(off, blk), :] * inv
        return carry

    lax.fori_loop(0, n_blocks, norm_body, 0)
    p_ref[past:past + LANES, :] = jnp.zeros((LANES, LANES), F32)
    p_ref[past:past + tq, :] = p_own * inv


def _moba_softmax(scores, k_new, wq_t, tq):
    b, past, _ = scores.shape
    width = k_new.shape[2]
    n_blocks = past // MOBA_BLOCK
    nrow = -(-n_blocks // 8) * 8
    kernel = functools.partial(_moba_softmax_kernel, n_blocks=n_blocks, tq=tq, scale=LANES ** -0.5)
    return pl.pallas_call(
        kernel,
        grid=(b,),
        in_specs=[pl.BlockSpec((None, past, LANES), lambda bi: (bi, 0, 0)),
                  pl.BlockSpec((None, tq, width), lambda bi: (bi, 0, 0)),
                  pl.BlockSpec((None, width, LANES), lambda bi: (bi, 0, 0))],
        out_specs=pl.BlockSpec((None, past + LANES, LANES), lambda bi: (bi, 0, 0)),
        out_shape=jax.ShapeDtypeStruct((b, past + LANES, LANES), F32),
        scratch_shapes=[pltpu.VMEM((nrow, LANES), F32), pltpu.VMEM((nrow, LANES), F32)],
        compiler_params=_cparams("parallel"),
        name="moba_sample_softmax",
    )(scores, k_new, wq_t)


def _moba_pv_kernel(pt_ref, p_ref, pown_ref, vnew_ref, *refs, pps, page):
    del pt_ref
    v_refs, o_ref = refs[:pps], refs[pps]

    @pl.when(pl.program_id(1) == 0)
    def _():
        o_ref[...] = _dot_tn(pown_ref[...].astype(BF16), vnew_ref[...].astype(BF16))

    hd = LANES
    for h in range(ATTN_KV_HEADS):
        acc = o_ref[:, h * hd:(h + 1) * hd]
        for r in range(pps):
            vh = v_refs[r][pl.ds(h, page, stride=ATTN_KV_HEADS), :].astype(BF16)
            acc = acc + _dot_tn(p_ref[r * page:(r + 1) * page, :].astype(BF16), vh)
        o_ref[:, h * hd:(h + 1) * hd] = acc


def _moba_pv(cache4, layer, page_table, probs, v_new_pad, pps):
    b, n_pages = page_table.shape
    rows, hd = cache4.shape[2], cache4.shape[3]
    page, width = rows // ATTN_KV_HEADS, hd * ATTN_KV_HEADS
    own_blk = (n_pages * page) // LANES

    def page_spec(r):
        return pl.BlockSpec((None, None, rows, hd), lambda bi, p, pt: (layer, pt[bi, p * pps + r], 0, 0))

    grid_spec = pltpu.PrefetchScalarGridSpec(
        num_scalar_prefetch=1,
        grid=(b, n_pages // pps),
        in_specs=[pl.BlockSpec((None, pps * page, LANES), lambda bi, p, pt: (bi, p, 0)),
                  pl.BlockSpec((None, LANES, LANES), lambda bi, p, pt: (bi, own_blk, 0)),
                  pl.BlockSpec((None, LANES, width), lambda bi, p, pt: (bi, 0, 0))]
                 + [page_spec(r) for r in range(pps)],
        out_specs=pl.BlockSpec((None, LANES, width), lambda bi, p, pt: (bi, 0, 0)),
    )
    return pl.pallas_call(
        functools.partial(_moba_pv_kernel, pps=pps, page=page),
        grid_spec=grid_spec,
        out_shape=jax.ShapeDtypeStruct((b, LANES, width), F32),
        compiler_params=_cparams("parallel", "arbitrary"),
        name="moba_sample_pv",
    )(page_table, probs, probs, v_new_pad, *([cache4] * pps))


def _merge_kernel(ap_ref, as_ref, aa_ref, wp_ref, ws_ref, wa_ref, g0_ref, g1_ref, g2_ref, o_ref):
    acc = _sigmoid(g0_ref[...]) * _dot(ap_ref[...], wp_ref[...])
    acc = acc + _sigmoid(g1_ref[...]) * _dot(as_ref[...], ws_ref[...])
    acc = acc + _sigmoid(g2_ref[...]) * _dot(aa_ref[...], wa_ref[...])
    o_ref[...] = acc.astype(BF16)


def _merge(pool_o, ssd_o, attn_o, w_pool, w_ssd, w_attn, proj, col_gate, layer):
    m, d = pool_o.shape
    tm = _pick(m, 512)
    tn = _pick(d, 512)
    gb = col_gate // tn
    nb = d // tn

    def act(width):
        return pl.BlockSpec((tm, width), lambda j, i: (i, 0))

    def wgt(kdim):
        return pl.BlockSpec((None, kdim, tn), lambda j, i: (layer, 0, j))

    def gate(which):
        return pl.BlockSpec((tm, tn), lambda j, i: (i, gb + which * nb + j))

    return pl.pallas_call(
        _merge_kernel,
        grid=(d // tn, m // tm),
        in_specs=[act(pool_o.shape[1]), act(ssd_o.shape[1]), act(attn_o.shape[1]),
                  wgt(w_pool.shape[1]), wgt(w_ssd.shape[1]), wgt(w_attn.shape[1]),
                  gate(0), gate(1), gate(2)],
        out_specs=pl.BlockSpec((tm, tn), lambda j, i: (i, j)),
        out_shape=jax.ShapeDtypeStruct((m, d), BF16),
        compiler_params=_cparams("parallel", "parallel"),
        name="gated_merge",
    )(pool_o, ssd_o, attn_o, w_pool, w_ssd, w_attn, proj, proj, proj)


def _mm_res_kernel(a_ref, w_ref, x_ref, gt_ref, gpost_ref, gpre_ref, sc_ref, sh_ref, xo_ref, u_ref, acc_ref, *, nk):
    k = pl.program_id(2)

    @pl.when(k == 0)
    def _():
        acc_ref[...] = jnp.zeros(acc_ref.shape, F32)

    acc_ref[...] += _dot(a_ref[...], w_ref[...])

    @pl.when(k == nk - 1)
    def _():
        xn = x_ref[...] + gt_ref[...] * (_rms(acc_ref[...]) * gpost_ref[...])
        xo_ref[...] = xn
        y = _rms(xn) * gpre_ref[...]
        u_ref[...] = (y * (1.0 + sc_ref[...]) + sh_ref[...]).astype(BF16)


def _mm_res(a3, w, layer, x3, gt, g_post, g_pre, sc, sh):
    b, t, kdim = a3.shape
    d = x3.shape[2]
    r = gt.shape[1]
    tm = _pick(t, 512)
    tk = _pick(kdim, 512)
    nk = kdim // tk
    mod_spec = pl.BlockSpec((None, r if r == 1 else tm, d),
                            (lambda bi, i, k: (bi, i, 0)) if r == t else (lambda bi, i, k: (bi, 0, 0)))
    vec_spec = pl.BlockSpec((1, d), lambda bi, i, k: (0, 0))
    row_spec = pl.BlockSpec((None, tm, d), lambda bi, i, k: (bi, i, 0))
    return pl.pallas_call(
        functools.partial(_mm_res_kernel, nk=nk),
        grid=(b, t // tm, nk),
        in_specs=[pl.BlockSpec((None, tm, tk), lambda bi, i, k: (bi, i, k)),
                  pl.BlockSpec((None, tk, d), lambda bi, i, k: (layer, k, 0)),
                  row_spec, mod_spec, vec_spec, vec_spec, mod_spec, mod_spec],
        out_specs=[row_spec, row_spec],
        out_shape=[jax.ShapeDtypeStruct((b, t, d), F32), jax.ShapeDtypeStruct((b, t, d), BF16)],
        scratch_shapes=[pltpu.VMEM((tm, d), F32)],
        compiler_params=_cparams("parallel", "parallel", "arbitrary"),
        name="matmul_residual_norm",
    )(a3, w, x3, gt, g_post, g_pre, sc, sh)


def _ffn_kernel(u_ref, wg_ref, wu_ref, o_ref):
    u = u_ref[...]
    o_ref[...] = (_silu(_dot(u, wg_ref[...])) * _dot(u, wu_ref[...])).astype(BF16)


def _ffn_hidden(u, w_gate, w_up, layer):
    m, k = u.shape
    n = w_gate.shape[2]
    tm = _pick(m, 1024)
    tn = _pick(n, 512)
    w_spec = pl.BlockSpec((None, k, tn), lambda j, i: (layer, 0, j))
    return pl.pallas_call(
        _ffn_kernel,
        grid=(n // tn, m // tm),
        in_specs=[pl.BlockSpec((tm, k), lambda j, i: (i, 0)), w_spec, w_spec],
        out_specs=pl.BlockSpec((tm, tn), lambda j, i: (i, j)),
        out_shape=jax.ShapeDtypeStruct((m, n), BF16),
        compiler_params=_cparams("parallel", "parallel"),
        name="ffn_hidden",
    )(u, w_gate, w_up)


def kernel(x_prompt, x_sample, cache_k, cache_v, state_ssm, state_conv, state_pool, page_table, c_prompt, c_sample, w_ada, b_ada, g_pre_mix, g_post_mix, g_pre_ffn, g_post_ffn, w_in, w_pool_grp, pool_scale, conv_w, conv_b, dt_bias, a_log, d_skip, ssd_norm, w_pool_br, w_ssd_br, w_attn_br, w_out, w_gate, w_up, w_down):
    bp, tp, d = x_prompt.shape
    bs, ts, _ = x_sample.shape
    depth = w_ada.shape[0]
    hd = d // ATTN_HEADS
    assert hd == LANES and ATTN_KV_HEADS * ATTN_GROUP * ts == LANES
    inner = ssd_norm.shape[1]
    n_heads = inner // SSD_HEADDIM
    conv_dim = conv_w.shape[2]
    attn_w = ATTN_HEADS * hd
    kv_w = ATTN_KV_HEADS * hd
    page = cache_k.shape[2]
    n_pages = page_table.shape[1]
    past = n_pages * page
    assert tp % MOBA_BLOCK == 0 and past % MOBA_BLOCK == 0 and n_heads <= LANES

    o_z = d
    o_xbc = o_z + inner
    o_dt = o_xbc + conv_dim
    o_q = o_dt + n_heads
    o_k = o_q + attn_w
    o_v = o_k + kv_w
    o_g = o_v + kv_w
    w_main = jnp.concatenate([w_in[:, :, :o_dt], w_in[:, :, o_q:o_k], w_in[:, :, o_g:]], axis=2).astype(BF16)
    w_k = w_in[:, :, o_k:o_v].astype(BF16)
    w_v = w_in[:, :, o_v:o_g].astype(BF16)
    w_dt = jnp.pad(w_in[:, :, o_dt:o_q], ((0, 0), (0, 0), (0, LANES - n_heads))).astype(BF16)
    col_z, col_xbc, col_q = o_z, o_xbc, o_dt
    col_gate = col_q + attn_w
    w_grp16 = w_pool_grp.astype(BF16)
    w_pool16, w_ssd16, w_attn16 = w_pool_br.astype(BF16), w_ssd_br.astype(BF16), w_attn_br.astype(BF16)
    w_out16, w_gate16, w_up16, w_down16 = w_out.astype(BF16), w_gate.astype(BF16), w_up.astype(BF16), w_down.astype(BF16)

    pad_h = LANES - n_heads
    dt_bias_p = jnp.pad(dt_bias, ((0, 0), (0, pad_h)))
    a_log_p = jnp.pad(a_log, ((0, 0), (0, pad_h)))
    d_skip_e = jnp.repeat(d_skip, SSD_HEADDIM, axis=1)

    n_c = bp + bs
    c_all = jnp.pad(jnp.concatenate([c_prompt, c_sample], axis=0), ((0, -n_c % 8), (0, 0)))
    mod = _ada(c_all, w_ada, b_ada)

    def mods(layer, which):
        m = mod[layer, :, which * d:(which + 1) * d]
        m_s = jnp.broadcast_to(m[bp:n_c, None, :], (bs, ts, d)).reshape(1, bs * ts, d)
        return m[:bp, None, :], m_s

    hg_w = (n_heads // SSD_GROUPS) * SSD_HEADDIM
    ts_pad = SSD_CHUNK
    pps = 8 if n_pages % 8 == 0 else 1
    eye_kv = jnp.eye(ATTN_KV_HEADS, dtype=F32)

    xp, xs_ = x_prompt, x_sample.reshape(1, bs * ts, d)
    sh1 = mods(0, 0)
    sc1 = mods(0, 1)
    up = _norm_mod(xp, g_pre_mix[0:1], sc1[0], sh1[0])
    us = _norm_mod(xs_, g_pre_mix[0:1], sc1[1], sh1[1])

    outs = {n: [] for n in ("kp", "vp", "hp", "cp", "pp", "ks", "vs", "hs", "cs", "ps")}
    for layer in range(depth):
        gt1, sh2, sc2, gt2 = mods(layer, 2), mods(layer, 3), mods(layer, 4), mods(layer, 5)
        nxt = (layer + 1) % depth
        sh1n, sc1n = mods(nxt, 0), mods(nxt, 1)
        cw, cbias = conv_w[layer], conv_b[layer:layer + 1]

        u2 = up.reshape(bp * tp, d)
        proj = _matmul(u2, w_main, layer).reshape(bp, tp, -1)
        k_p = _matmul(u2, w_k, layer).reshape(bp, tp, kv_w)
        v_p = _matmul(u2, w_v, layer).reshape(bp, tp, kv_w)
        dt_p = _matmul(u2, w_dt, layer).reshape(bp, tp, LANES)
        pool_o = _pool(proj, jnp.zeros((bp, POOL_HIST, d), F32), w_grp16, pool_scale[layer:layer + 1], layer, 0)
        ssd_o, h_p = _ssd(proj, dt_p, jnp.zeros((bp, CONV_HIST, conv_dim), F32),
                          jnp.zeros((bp, SSD_GROUPS, hg_w, SSD_STATE), F32), cw, cbias,
                          dt_bias_p[layer:layer + 1], a_log_p[layer:layer + 1], d_skip_e[layer:layer + 1],
                          ssd_norm[layer:layer + 1], col_z, col_xbc, tp)
        attn_o = _moba_prompt(proj, k_p, v_p, col_q)
        merged = _merge(pool_o.reshape(bp * tp, d), ssd_o.reshape(bp * tp, inner), attn_o.reshape(bp * tp, attn_w),
                        w_pool16, w_ssd16, w_attn16, proj.reshape(bp * tp, -1), col_gate, layer)
        xp, u_ffn = _mm_res(merged.reshape(bp, tp, d), w_out16, layer, xp, gt1[0], g_post_mix[layer:layer + 1],
                            g_pre_ffn[layer:layer + 1], sc2[0], sh2[0])
        hid = _ffn_hidden(u_ffn.reshape(bp * tp, d), w_gate16, w_up16, layer)
        xp, up = _mm_res(hid.reshape(bp, tp, -1), w_down16, layer, xp, gt2[0], g_post_ffn[layer:layer + 1],
                         g_pre_mix[nxt:nxt + 1], sc1n[0], sh1n[0])
        outs["kp"].append(k_p.reshape(bp, tp, ATTN_KV_HEADS, hd))
        outs["vp"].append(v_p.reshape(bp, tp, ATTN_KV_HEADS, hd))
        outs["hp"].append(h_p.reshape(bp, n_heads, SSD_HEADDIM, SSD_STATE))
        outs["cp"].append(proj[:, tp - (SSD_CONV - 1):, col_xbc:col_xbc + conv_dim])
        outs["pp"].append(proj[:, tp - (POOL_HIST - 1):, :d])

        u2 = us.reshape(bs * ts, d)
        proj = _matmul(u2, w_main, layer).reshape(bs, ts, -1)
        k_s = _matmul(u2, w_k, layer).reshape(bs, ts, kv_w)
        v_s = _matmul(u2, w_v, layer).reshape(bs, ts, kv_w)
        dt_s = _matmul(u2, w_dt, layer).reshape(bs, ts, LANES)
        pool_buf = jnp.pad(state_pool[layer], ((0, 0), (1, 0), (0, 0)))
        pool_o = _pool(proj, pool_buf, w_grp16, pool_scale[layer:layer + 1], layer, past)
        tpad = ((0, 0), (0, ts_pad - ts), (0, 0))
        conv_prev = jnp.pad(state_conv[layer], ((0, 0), (CONV_HIST - (SSD_CONV - 1), 0), (0, 0)))
        ssd_o, h_s = _ssd(jnp.pad(proj[:, :, :col_q], tpad), jnp.pad(dt_s, tpad), conv_prev,
                          state_ssm[layer].reshape(bs, SSD_GROUPS, hg_w, SSD_STATE), cw, cbias,
                          dt_bias_p[layer:layer + 1], a_log_p[layer:layer + 1], d_skip_e[layer:layer + 1],
                          ssd_norm[layer:layer + 1], col_z, col_xbc, ts)
        ssd_o = ssd_o[:, :ts]
        q5 = proj[:, :, col_q:col_q + attn_w].reshape(bs, ts, ATTN_KV_HEADS, ATTN_GROUP, hd)
        wq_t = jnp.einsum("btkgd,kj->bjdkgt", q5, eye_kv).reshape(bs, kv_w, LANES).astype(BF16)
        cache_k4 = cache_k.reshape(depth, cache_k.shape[1], page * ATTN_KV_HEADS, hd)
        cache_v4 = cache_v.reshape(depth, cache_v.shape[1], page * ATTN_KV_HEADS, hd)
        scores = _moba_scores(cache_k4, layer, page_table, wq_t, pps)
        probs = _moba_softmax(scores, k_s, wq_t, ts)
        o_full = _moba_pv(cache_v4, layer, page_table, probs, jnp.pad(v_s, ((0, 0), (0, LANES - ts), (0, 0))), pps)
        o6 = o_full.reshape(bs, ATTN_KV_HEADS, ATTN_GROUP, ts, ATTN_KV_HEADS, hd)
        attn_o = jnp.einsum("bkgtjd,kj->btkgd", o6, eye_kv).reshape(bs * ts, attn_w).astype(BF16)
        merged = _merge(pool_o.reshape(bs * ts, d), ssd_o.reshape(bs * ts, inner), attn_o,
                        w_pool16, w_ssd16, w_attn16, proj.reshape(bs * ts, -1), col_gate, layer)
        xs_, u_ffn = _mm_res(merged.reshape(1, bs * ts, d), w_out16, layer, xs_, gt1[1], g_post_mix[layer:layer + 1],
                             g_pre_ffn[layer:layer + 1], sc2[1], sh2[1])
        hid = _ffn_hidden(u_ffn.reshape(bs * ts, d), w_gate16, w_up16, layer)
        xs_, us = _mm_res(hid.reshape(1, bs * ts, -1), w_down16, layer, xs_, gt2[1], g_post_ffn[layer:layer + 1],
                          g_pre_mix[nxt:nxt + 1], sc1n[1], sh1n[1])
        outs["ks"].append(k_s.reshape(bs, ts, ATTN_KV_HEADS, hd))
        outs["vs"].append(v_s.reshape(bs, ts, ATTN_KV_HEADS, hd))
        outs["hs"].append(h_s.reshape(bs, n_heads, SSD_HEADDIM, SSD_STATE))
        conv_ext = jnp.concatenate([state_conv[layer], proj[:, :, col_xbc:col_xbc + conv_dim]], axis=1)
        outs["cs"].append(conv_ext[:, ts:])
        pool_ext = jnp.concatenate([state_pool[layer], proj[:, :, :d]], axis=1)
        outs["ps"].append(pool_ext[:, ts:])

    st = {n: jnp.stack(v) for n, v in outs.items()}
    return (xp, xs_.reshape(bs, ts, d), st["kp"], st["vp"], st["hp"], st["cp"], st["pp"],
            st["ks"], st["vs"], st["hs"], st["cs"], st["ps"])
```

```python
import functools

import jax
import jax.numpy as jnp
from jax import lax
from jax.experimental import pallas as pl
from jax.experimental.pallas import tpu as pltpu

F32 = jnp.float32
BF16 = jnp.bfloat16

NORM_EPS = 1e-6
POOL_WINDOWS = (2, 4, 8, 16)
POOL_HIST = 16
SSD_HEADDIM = 64
SSD_GROUPS = 8
SSD_STATE = 128
SSD_CONV = 4
SSD_CHUNK = 128
CONV_HIST = 8
ATTN_HEADS = 16
ATTN_KV_HEADS = 8
ATTN_GROUP = ATTN_HEADS // ATTN_KV_HEADS
MOBA_BLOCK = 256
MOBA_TOPK = 3
PAST_GROUP = 4
LANES = 128
VMEM_LIMIT_BYTES = 56 * 1024 * 1024
NEG_INF = float("-inf")
LOG2E = 1.4426950408889634


def _cparams(*sem):
    return pltpu.CompilerParams(dimension_semantics=sem, vmem_limit_bytes=VMEM_LIMIT_BYTES)


def _sigmoid(x):
    return 1.0 / (1.0 + jnp.exp(-x))


def _silu(x):
    return x * _sigmoid(x)


def _dot(a, b):
    return jnp.dot(a, b, preferred_element_type=F32)


def _dot_nt(a, b):
    return lax.dot_general(a, b, (((1,), (1,)), ((), ())), preferred_element_type=F32)


def _dot_tn(a, b):
    return lax.dot_general(a, b, (((0,), (0,)), ((), ())), preferred_element_type=F32)


def _rms(x):
    return x * lax.rsqrt(jnp.mean(x * x, axis=-1, keepdims=True) + NORM_EPS)


def _pick(total, pref):
    if total <= pref:
        return total
    t = pref
    while total % t:
        t //= 2
    return t


def _ada_kernel(c_ref, w_ref, b_ref, o_ref):
    a = _silu(c_ref[...]).astype(BF16)
    o_ref[...] = _dot(a, w_ref[...].astype(BF16)) + b_ref[...]


def _ada(c_all, w_ada, b_ada):
    depth, d, n = w_ada.shape
    rows = c_all.shape[0]
    tn = _pick(n, 1024)
    return pl.pallas_call(
        _ada_kernel,
        grid=(depth, n // tn),
        in_specs=[pl.BlockSpec((rows, d), lambda l, j: (0, 0)),
                  pl.BlockSpec((None, d, tn), lambda l, j: (l, 0, j)),
                  pl.BlockSpec((None, 1, tn), lambda l, j: (l, 0, j))],
        out_specs=pl.BlockSpec((None, rows, tn), lambda l, j: (l, 0, j)),
        out_shape=jax.ShapeDtypeStruct((depth, rows, n), F32),
        compiler_params=_cparams("parallel", "parallel"),
        name="ada_mod",
    )(c_all, w_ada, b_ada.reshape(depth, 1, n))


def _norm_mod_kernel(x_ref, g_ref, sc_ref, sh_ref, u_ref):
    y = _rms(x_ref[...]) * g_ref[...]
    u_ref[...] = (y * (1.0 + sc_ref[...]) + sh_ref[...]).astype(BF16)


def _norm_mod(x, g, sc, sh):
    b, t, d = x.shape
    r = sc.shape[1]
    tt = _pick(t, 512)
    mod_spec = pl.BlockSpec((None, r if r == 1 else tt, d),
                            (lambda i, j: (i, j, 0)) if r == t else (lambda i, j: (i, 0, 0)))
    return pl.pallas_call(
        _norm_mod_kernel,
        grid=(b, t // tt),
        in_specs=[pl.BlockSpec((None, tt, d), lambda i, j: (i, j, 0)),
                  pl.BlockSpec((1, d), lambda i, j: (0, 0)),
                  mod_spec, mod_spec],
        out_specs=pl.BlockSpec((None, tt, d), lambda i, j: (i, j, 0)),
        out_shape=jax.ShapeDtypeStruct((b, t, d), BF16),
        compiler_params=_cparams("parallel", "parallel"),
        name="norm_mod",
    )(x, g, sc, sh)


def _mm_kernel(a_ref, w_ref, o_ref, w16_ref):
    @pl.when(pl.program_id(1) == 0)
    def _():
        w16_ref[...] = w_ref[...].astype(BF16)

    o_ref[...] = _dot(a_ref[...], w16_ref[...])


def _matmul(a, w, layer, col0, n):
    m, k = a.shape
    tm = _pick(m, 1024)
    tn = _pick(n, 1024)
    assert col0 % tn == 0
    cb = col0 // tn
    return pl.pallas_call(
        _mm_kernel,
        grid=(n // tn, m // tm),
        in_specs=[pl.BlockSpec((tm, k), lambda j, i: (i, 0)),
                  pl.BlockSpec((None, k, tn), lambda j, i: (layer, 0, cb + j))],
        out_specs=pl.BlockSpec((tm, tn), lambda j, i: (i, j)),
        out_shape=jax.ShapeDtypeStruct((m, n), F32),
        scratch_shapes=[pltpu.VMEM((k, tn), BF16)],
        compiler_params=_cparams("parallel", "arbitrary"),
        name="proj_matmul",
    )(a, w)


def _pool_kernel(p_ref, buf_ref, w_ref, sc_ref, o_ref, ext_ref, *, tt, start, pg):
    t = pl.program_id(1)

    @pl.when(t == 0)
    def _():
        ext_ref[0:POOL_HIST, :] = buf_ref[...]

    ext_ref[POOL_HIST:POOL_HIST + tt, :] = p_ref[...]
    n_valid = lax.broadcasted_iota(jnp.int32, (tt, 1), 0) + (t * tt + start + 1)
    for gi, w in enumerate(POOL_WINDOWS):
        c0 = gi * pg
        x = ext_ref[POOL_HIST:POOL_HIST + tt, c0:c0 + pg]
        s = x
        for k in range(1, w):
            s = s + ext_ref[POOL_HIST - k:POOL_HIST - k + tt, c0:c0 + pg]
        cnt = jnp.minimum(n_valid, w).astype(F32)
        mixed = (s / cnt - x).astype(BF16)
        y = _dot(mixed, w_ref[gi]) * sc_ref[:, c0:c0 + pg]
        o_ref[:, c0:c0 + pg] = y.astype(BF16)
    ext_ref[0:POOL_HIST, :] = ext_ref[tt:tt + POOL_HIST, :]


def _pool(proj3, buf, w_grp, scale, layer, start):
    b, t, _ = proj3.shape
    c = scale.shape[1]
    pg = c // len(POOL_WINDOWS)
    tt = _pick(t, 256)
    return pl.pallas_call(
        functools.partial(_pool_kernel, tt=tt, start=start, pg=pg),
        grid=(b, t // tt),
        in_specs=[pl.BlockSpec((None, tt, c), lambda i, j: (i, j, 0)),
                  pl.BlockSpec((None, POOL_HIST, c), lambda i, j: (i, 0, 0)),
                  pl.BlockSpec((None, len(POOL_WINDOWS), pg, pg), lambda i, j: (layer, 0, 0, 0)),
                  pl.BlockSpec((1, c), lambda i, j: (0, 0))],
        out_specs=pl.BlockSpec((None, tt, c), lambda i, j: (i, j, 0)),
        out_shape=jax.ShapeDtypeStruct((b, t, c), BF16),
        scratch_shapes=[pltpu.VMEM((POOL_HIST + tt, c), F32)],
        compiler_params=_cparams("parallel", "arbitrary"),
        name="pool_mixer",
    )(proj3, buf, w_grp, scale)


def _ssd_kernel(z_ref, x_ref, b_ref, c_ref, dt_ref, cpx_ref, cpb_ref, cpc_ref,
                cwx_ref, cwb_ref, cwc_ref, cbx_ref, cbb_ref, cbc_ref,
                dtb_ref, alog_ref, dsk_ref, nrm_ref, exp_ref, h0_ref,
                y_ref, hout_ref, extx, extb, extc, ht, *, t_valid, n_chunks, hg):
    g = pl.program_id(1)
    c = pl.program_id(2)
    lc = SSD_CHUNK
    gw = hg * SSD_HEADDIM

    @pl.when(c == 0)
    def _():
        extx[0:CONV_HIST, :] = cpx_ref[...]
        extb[0:CONV_HIST, :] = cpb_ref[...]
        extc[0:CONV_HIST, :] = cpc_ref[...]
        ht[...] = h0_ref[...].T

    def conv(ext, raw_ref, w_ref, bias_ref):
        ext[CONV_HIST:CONV_HIST + lc, :] = raw_ref[...]
        acc = bias_ref[...]
        for j in range(SSD_CONV):
            r0 = CONV_HIST - (SSD_CONV - 1) + j
            acc = acc + ext[r0:r0 + lc, :] * w_ref[j:j + 1, :]
        ext[0:CONV_HIST, :] = ext[lc:lc + CONV_HIST, :]
        return _silu(acc)

    xs = conv(extx, x_ref, cwx_ref, cbx_ref)
    bm = conv(extb, b_ref, cwb_ref, cbb_ref)
    cm = conv(extc, c_ref, cwc_ref, cbc_ref)

    row = lax.broadcasted_iota(jnp.int32, (lc, LANES), 0)
    col = lax.broadcasted_iota(jnp.int32, (lc, LANES), 1)
    xdt = dt_ref[...] + dtb_ref[...]
    dt = jnp.maximum(xdt, 0.0) + jnp.log1p(jnp.exp(-jnp.abs(xdt)))
    dt = jnp.where(c * lc + row < t_valid, dt, 0.0)
    dta = dt * (-jnp.exp(alog_ref[...]))
    acs = dta
    k = 1
    while k < lc:
        acs = acs + jnp.where(row >= k, pltpu.roll(acs, k, axis=0), 0.0)
        k *= 2
    a_last = acs[lc - 1:lc, :]
    w_end = dt * jnp.exp(a_last - acs)
    exp_a = jnp.exp(acs)

    cd = jnp.exp(a_last)
    cd_hi = cd.astype(BF16).astype(F32)
    cd_mid = (cd - cd_hi).astype(BF16).astype(F32)
    cd_lo = cd - cd_hi - cd_mid
    pieces = [jnp.broadcast_to(p, (16, LANES)) for p in (cd_hi, cd_mid, cd_lo)]
    stacked = jnp.concatenate([dt, w_end, exp_a] + pieces, axis=0).astype(BF16)
    wide = _dot(stacked, exp_ref[...])
    dt_e, w_e, ea_e = wide[0:lc], wide[lc:2 * lc], wide[2 * lc:3 * lc]
    r0 = 3 * lc
    cd_e = wide[r0:r0 + 1] + wide[r0 + 16:r0 + 17] + wide[r0 + 32:r0 + 33]

    acs_g = pltpu.roll(acs * LOG2E, (LANES - g * hg) % LANES, axis=1)
    acs_gt = acs_g.T

    bm16 = bm.astype(BF16)
    cm16 = cm.astype(BF16)
    causal = row >= col
    cb = jnp.where(causal, _dot_nt(cm16, bm16), 0.0)
    xdt_e = xs * dt_e
    heads_here = LANES // SSD_HEADDIM
    y_parts = []
    for pr in range(gw // LANES):
        xp = xdt_e[:, pr * LANES:(pr + 1) * LANES]
        lhs, rhs = [], []
        for hh in range(heads_here):
            h = pr * heads_here + hh
            seg = acs_g[:, h:h + 1] - acs_gt[h:h + 1, :]
            lhs.append((cb * jnp.exp2(jnp.minimum(seg, 0.0))).astype(BF16))
            in_head = (col >= hh * SSD_HEADDIM) & (col < (hh + 1) * SSD_HEADDIM)
            rhs.append(jnp.where(in_head, xp, 0.0).astype(BF16))
        y_parts.append(_dot(jnp.concatenate(lhs, axis=1), jnp.concatenate(rhs, axis=0)))
    y = jnp.concatenate(y_parts, axis=1)

    h_in = ht[...]
    y = y + _dot(cm16, h_in.astype(BF16)) * ea_e
    h_new = h_in * cd_e + _dot(bm.T.astype(BF16), (xs * w_e).astype(BF16))
    ht[...] = h_new

    y = y + dsk_ref[...] * xs
    y = y * _silu(z_ref[...])
    y_ref[...] = (_rms(y) * nrm_ref[...]).astype(BF16)

    @pl.when(c == n_chunks - 1)
    def _():
        hout_ref[...] = h_new.T


def _ssd(proj3, dt3, conv_prev, h0, conv_w, conv_b, dt_bias, a_log, d_skip_e, ssd_norm, col_z, col_xbc, t_valid):
    b, tp, _ = proj3.shape
    inner = ssd_norm.shape[1]
    n_heads = inner // SSD_HEADDIM
    hg = n_heads // SSD_GROUPS
    gw = hg * SSD_HEADDIM
    ns = SSD_STATE
    lc = SSD_CHUNK
    n_chunks = tp // lc
    zb, xb = col_z // gw, col_xbc // gw
    bb = (col_xbc + inner) // ns
    cb_ = (col_xbc + inner + SSD_GROUPS * ns) // ns
    cxb, cbb, ccb = 0, inner // ns, (inner + SSD_GROUPS * ns) // ns
    head_of = (jnp.arange(SSD_GROUPS)[:, None] * hg + jnp.arange(gw)[None, :] // SSD_HEADDIM)
    expand = (jnp.arange(LANES)[None, :, None] == head_of[:, None, :]).astype(BF16)

    def seq(width, off):
        return pl.BlockSpec((None, lc, width), lambda i, g, c: (i, c, off + g))

    def prev(width, off):
        return pl.BlockSpec((None, CONV_HIST, width), lambda i, g, c: (i, 0, off + g))

    def par(rows, width, off):
        return pl.BlockSpec((rows, width), lambda i, g, c: (0, off + g))

    state_spec = pl.BlockSpec((None, None, gw, ns), lambda i, g, c: (i, g, 0, 0))
    kernel = functools.partial(_ssd_kernel, t_valid=t_valid, n_chunks=n_chunks, hg=hg)
    return pl.pallas_call(
        kernel,
        grid=(b, SSD_GROUPS, n_chunks),
        in_specs=[seq(gw, zb), seq(gw, xb), seq(ns, bb), seq(ns, cb_),
                  pl.BlockSpec((None, lc, LANES), lambda i, g, c: (i, c, 0)),
                  prev(gw, cxb), prev(ns, cbb), prev(ns, ccb),
                  par(SSD_CONV, gw, cxb), par(SSD_CONV, ns, cbb), par(SSD_CONV, ns, ccb),
                  par(1, gw, cxb), par(1, ns, cbb), par(1, ns, ccb),
                  pl.BlockSpec((1, LANES), lambda i, g, c: (0, 0)),
                  pl.BlockSpec((1, LANES), lambda i, g, c: (0, 0)),
                  par(1, gw, 0), par(1, gw, 0),
                  pl.BlockSpec((None, LANES, gw), lambda i, g, c: (g, 0, 0)),
                  state_spec],
        out_specs=[pl.BlockSpec((None, lc, gw), lambda i, g, c: (i, c, g)), state_spec],
        out_shape=[jax.ShapeDtypeStruct((b, tp, inner), BF16),
                   jax.ShapeDtypeStruct((b, SSD_GROUPS, gw, ns), F32)],
        scratch_shapes=[pltpu.VMEM((CONV_HIST + lc, gw), F32),
                        pltpu.VMEM((CONV_HIST + lc, ns), F32),
                        pltpu.VMEM((CONV_HIST + lc, ns), F32),
                        pltpu.VMEM((ns, gw), F32)],
        compiler_params=_cparams("parallel", "parallel", "arbitrary"),
        name="ssd_mixer",
    )(proj3, proj3, proj3, proj3, dt3, conv_prev, conv_prev, conv_prev,
      conv_w, conv_w, conv_w, conv_b, conv_b, conv_b, dt_bias, a_log, d_skip_e, ssd_norm, expand, h0)


def _top_k_mask(gate, valid, ids, axis):
    big = jnp.int32(2 ** 30)
    remaining = valid
    sel = jnp.zeros(gate.shape, F32)
    for _ in range(MOBA_TOPK):
        gm = jnp.where(remaining > 0.5, gate, NEG_INF)
        m = jnp.max(gm, axis=axis, keepdims=True)
        cand = jnp.where(gm == m, remaining, 0.0)
        first = jnp.min(jnp.where(cand > 0.5, ids, big), axis=axis, keepdims=True)
        pick = jnp.where(ids == first, 1.0, 0.0)
        sel = sel + pick
        remaining = remaining - pick
    return sel


def _moba_prompt_kernel(q_ref, k_ref, v_ref, o_ref, kb, vtb, km, sel_ref, qs_ref, acc_ref, ml_ref, *, n_blocks, scale):
    i = pl.program_id(2)
    blk = MOBA_BLOCK
    hd = LANES
    nq = ATTN_GROUP * blk
    c2 = scale * LOG2E

    @pl.when(i == 0)
    def _():
        km[...] = jnp.zeros(km.shape, F32)
        for j in range(n_blocks):
            kj = k_ref[j * blk:(j + 1) * blk, :]
            kb[j] = kj.astype(BF16)
            km[j:j + 1, :] = jnp.mean(kj, axis=0, keepdims=True)
            vtb[j] = v_ref[j * blk:(j + 1) * blk, :].T.astype(BF16)

    q = q_ref[...]
    qs_ref[...] = jnp.concatenate([q[:, g * hd:(g + 1) * hd] for g in range(ATTN_GROUP)], axis=0).astype(BF16)
    gate_t = _dot_nt(km[...].astype(BF16), qs_ref[...])
    ids = lax.broadcasted_iota(jnp.int32, (km.shape[0], nq), 0)
    valid = jnp.where(ids < i, 1.0, 0.0)
    sel_ref[...] = _top_k_mask(gate_t, valid, ids, axis=0)

    key_r = lax.broadcasted_iota(jnp.int32, (blk, nq), 0)
    qry_c = lax.broadcasted_iota(jnp.int32, (blk, nq), 1) & (blk - 1)
    s = _dot_nt(kb[i], qs_ref[...])
    s = jnp.where(key_r <= qry_c, s, NEG_INF)
    m = jnp.max(s, axis=0, keepdims=True)
    p = jnp.exp2((s - m) * c2)
    ml_ref[0:1, :] = m
    ml_ref[1:2, :] = jnp.sum(p, axis=0, keepdims=True)
    acc_ref[...] = _dot(vtb[i], p.astype(BF16))

    def past_blocks(j0, nb):
        m_old, l_old = ml_ref[0:1, :], ml_ref[1:2, :]
        ss = [_dot_nt(kb[j0 + k], qs_ref[...]) for k in range(nb)]
        picked = [sel_ref[pl.ds(j0 + k, 1), :] > 0.5 for k in range(nb)]
        m_new = m_old
        for k in range(nb):
            m_new = jnp.maximum(m_new, jnp.where(picked[k], jnp.max(ss[k], axis=0, keepdims=True), NEG_INF))
        alpha = jnp.exp2((m_old - m_new) * c2)
        l_new = alpha * l_old
        pv = None
        for k in range(nb):
            p = jnp.exp2((ss[k] - jnp.where(picked[k], m_new, jnp.inf)) * c2)
            l_new = l_new + jnp.sum(p, axis=0, keepdims=True)
            d = _dot(vtb[j0 + k], p.astype(BF16))
            pv = d if pv is None else pv + d
        acc_ref[...] = alpha * acc_ref[...] + pv
        ml_ref[0:1, :] = m_new
        ml_ref[1:2, :] = l_new

    def group_body(t, carry):
        past_blocks(t * PAST_GROUP, PAST_GROUP)
        return carry

    lax.fori_loop(0, i // PAST_GROUP, group_body, 0)
    done = (i // PAST_GROUP) * PAST_GROUP
    nb = PAST_GROUP // 2
    while nb >= 1:
        @pl.when((i & nb) != 0)
        def _(nb=nb, done=done):
            past_blocks(done, nb)

        done = done + (i & nb)
        nb //= 2

    o = (acc_ref[...] / ml_ref[1:2, :]).T
    o_ref[...] = jnp.concatenate([o[g * blk:(g + 1) * blk] for g in range(ATTN_GROUP)], axis=1).astype(BF16)


def _moba_prompt(proj3, k3, v3, col_q):
    b, t, _ = proj3.shape
    hd = LANES
    blk = MOBA_BLOCK
    n_blocks = t // blk
    nb_pad = -(-n_blocks // 16) * 16
    qw = ATTN_GROUP * hd
    qb = col_q // qw
    kernel = functools.partial(_moba_prompt_kernel, n_blocks=n_blocks, scale=hd ** -0.5)
    return pl.pallas_call(
        kernel,
        grid=(b, ATTN_KV_HEADS, n_blocks),
        in_specs=[pl.BlockSpec((None, blk, qw), lambda bi, h, i: (bi, i, qb + h)),
                  pl.BlockSpec((None, t, hd), lambda bi, h, i: (bi, 0, h)),
                  pl.BlockSpec((None, t, hd), lambda bi, h, i: (bi, 0, h))],
        out_specs=pl.BlockSpec((None, blk, qw), lambda bi, h, i: (bi, i, h)),
        out_shape=jax.ShapeDtypeStruct((b, t, ATTN_HEADS * hd), BF16),
        scratch_shapes=[pltpu.VMEM((n_blocks, blk, hd), BF16), pltpu.VMEM((n_blocks, hd, blk), BF16),
                        pltpu.VMEM((nb_pad, hd), F32), pltpu.VMEM((nb_pad, ATTN_GROUP * blk), F32),
                        pltpu.VMEM((ATTN_GROUP * blk, hd), BF16), pltpu.VMEM((hd, ATTN_GROUP * blk), F32),
                        pltpu.VMEM((8, ATTN_GROUP * blk), F32)],
        compiler_params=_cparams("parallel", "parallel", "arbitrary"),
        name="moba_prompt",
    )(proj3, k3, v3)


def _moba_scores_kernel(pt_ref, wq_ref, *refs, pps, page):
    del pt_ref
    k_refs, o_ref = refs[:pps], refs[pps]
    hd = LANES
    for r in range(pps):
        acc = None
        for h in range(ATTN_KV_HEADS):
            kh = k_refs[r][pl.ds(h, page, stride=ATTN_KV_HEADS), :].astype(BF16)
            part = _dot(kh, wq_ref[h * hd:(h + 1) * hd, :])
            acc = part if acc is None else acc + part
        o_ref[r * page:(r + 1) * page, :] = acc


def _moba_scores(cache4, layer, page_table, wq_t, pps):
    b, n_pages = page_table.shape
    rows, hd = cache4.shape[2], cache4.shape[3]
    page, width = rows // ATTN_KV_HEADS, hd * ATTN_KV_HEADS

    def page_spec(r):
        return pl.BlockSpec((None, None, rows, hd), lambda bi, p, pt: (layer, pt[bi, p * pps + r], 0, 0))

    grid_spec = pltpu.PrefetchScalarGridSpec(
        num_scalar_prefetch=1,
        grid=(b, n_pages // pps),
        in_specs=[pl.BlockSpec((None, width, LANES), lambda bi, p, pt: (bi, 0, 0))]
                 + [page_spec(r) for r in range(pps)],
        out_specs=pl.BlockSpec((None, pps * page, LANES), lambda bi, p, pt: (bi, p, 0)),
    )
    return pl.pallas_call(
        functools.partial(_moba_scores_kernel, pps=pps, page=page),
        grid_spec=grid_spec,
        out_shape=jax.ShapeDtypeStruct((b, n_pages * page, LANES), F32),
        compiler_params=_cparams("parallel", "arbitrary"),
        name="moba_sample_scores",
    )(page_table, wq_t, *([cache4] * pps))


def _moba_softmax_kernel(st_ref, kn_ref, wq_ref, p_ref, gate_ref, sel_ref, *, n_blocks, tq, scale):
    blk = MOBA_BLOCK
    past = n_blocks * blk
    gate_ref[...] = jnp.zeros(gate_ref.shape, F32)

    def gate_body(j, carry):
        off = pl.multiple_of(j * blk, blk)
        gate_ref[pl.ds(j, 1), :] = jnp.mean(st_ref[pl.ds(off, blk), :], axis=0, keepdims=True)
        return carry

    lax.fori_loop(0, n_blocks, gate_body, 0)
    nrow = gate_ref.shape[0]
    ids = lax.broadcasted_iota(jnp.int32, (nrow, LANES), 0)
    valid = jnp.where(ids < n_blocks, 1.0, 0.0)
    sel_ref[...] = _top_k_mask(gate_ref[...], valid, ids, axis=0)

    s_own = _dot(kn_ref[...].astype(BF16), wq_ref[...]) * scale
    key_t = lax.broadcasted_iota(jnp.int32, (tq, LANES), 0)
    qry_t = lax.broadcasted_iota(jnp.int32, (tq, LANES), 1) & (tq - 1)
    s_own = jnp.where(key_t <= qry_t, s_own, NEG_INF)
    m0 = jnp.max(s_own, axis=0, keepdims=True)

    def max_body(j, m):
        off = pl.multiple_of(j * blk, blk)
        bm = jnp.max(st_ref[pl.ds(off, blk), :], axis=0, keepdims=True) * scale
        return jnp.maximum(m, jnp.where(sel_ref[pl.ds(j, 1), :] > 0.5, bm, NEG_INF))

    m = lax.fori_loop(0, n_blocks, max_body, m0)
    p_own = jnp.exp(s_own - m)
    l0 = jnp.sum(p_own, axis=0, keepdims=True)

    def exp_body(j, l):
        off = pl.multiple_of(j * blk, blk)
        s = st_ref[pl.ds(off, blk), :] * scale
        p = jnp.where(sel_ref[pl.ds(j, 1), :] > 0.5, jnp.exp(s - m), 0.0)
        p_ref[pl.ds(off, blk), :] = p
        return l + jnp.sum(p, axis=0, keepdims=True)

    l = lax.fori_loop(0, n_blocks, exp_body, l0)
    inv = 1.0 / l

    def norm_body(j, carry):
        off = pl.multiple_of(j * blk, blk)
        p_ref[pl.ds(off, blk), :] = p_ref[pl.ds(off, blk), :] * inv
        return carry

    lax.fori_loop(0, n_blocks, norm_body, 0)
    p_ref[past:past + LANES, :] = jnp.zeros((LANES, LANES), F32)
    p_ref[past:past + tq, :] = p_own * inv


def _moba_softmax(scores, k_new, wq_t, tq):
    b, past, _ = scores.shape
    width = k_new.shape[2]
    n_blocks = past // MOBA_BLOCK
    nrow = -(-n_blocks // 8) * 8
    kernel = functools.partial(_moba_softmax_kernel, n_blocks=n_blocks, tq=tq, scale=LANES ** -0.5)
    return pl.pallas_call(
        kernel,
        grid=(b,),
        in_specs=[pl.BlockSpec((None, past, LANES), lambda bi: (bi, 0, 0)),
                  pl.BlockSpec((None, tq, width), lambda bi: (bi, 0, 0)),
                  pl.BlockSpec((None, width, LANES), lambda bi: (bi, 0, 0))],
        out_specs=pl.BlockSpec((None, past + LANES, LANES), lambda bi: (bi, 0, 0)),
        out_shape=jax.ShapeDtypeStruct((b, past + LANES, LANES), F32),
        scratch_shapes=[pltpu.VMEM((nrow, LANES), F32), pltpu.VMEM((nrow, LANES), F32)],
        compiler_params=_cparams("parallel"),
        name="moba_sample_softmax",
    )(scores, k_new, wq_t)


def _moba_pv_kernel(pt_ref, p_ref, pown_ref, vnew_ref, *refs, pps, page):
    del pt_ref
    v_refs, o_ref = refs[:pps], refs[pps]

    @pl.when(pl.program_id(1) == 0)
    def _():
        o_ref[...] = _dot_tn(pown_ref[...].astype(BF16), vnew_ref[...].astype(BF16))

    hd = LANES
    for h in range(ATTN_KV_HEADS):
        acc = o_ref[:, h * hd:(h + 1) * hd]
        for r in range(pps):
            vh = v_refs[r][pl.ds(h, page, stride=ATTN_KV_HEADS), :].astype(BF16)
            acc = acc + _dot_tn(p_ref[r * page:(r + 1) * page, :].astype(BF16), vh)
        o_ref[:, h * hd:(h + 1) * hd] = acc


def _moba_pv(cache4, layer, page_table, probs, v_new_pad, pps):
    b, n_pages = page_table.shape
    rows, hd = cache4.shape[2], cache4.shape[3]
    page, width = rows // ATTN_KV_HEADS, hd * ATTN_KV_HEADS
    own_blk = (n_pages * page) // LANES

    def page_spec(r):
        return pl.BlockSpec((None, None, rows, hd), lambda bi, p, pt: (layer, pt[bi, p * pps + r], 0, 0))

    grid_spec = pltpu.PrefetchScalarGridSpec(
        num_scalar_prefetch=1,
        grid=(b, n_pages // pps),
        in_specs=[pl.BlockSpec((None, pps * page, LANES), lambda bi, p, pt: (bi, p, 0)),
                  pl.BlockSpec((None, LANES, LANES), lambda bi, p, pt: (bi, own_blk, 0)),
                  pl.BlockSpec((None, LANES, width), lambda bi, p, pt: (bi, 0, 0))]
                 + [page_spec(r) for r in range(pps)],
        out_specs=pl.BlockSpec((None, LANES, width), lambda bi, p, pt: (bi, 0, 0)),
    )
    return pl.pallas_call(
        functools.partial(_moba_pv_kernel, pps=pps, page=page),
        grid_spec=grid_spec,
        out_shape=jax.ShapeDtypeStruct((b, LANES, width), F32),
        compiler_params=_cparams("parallel", "arbitrary"),
        name="moba_sample_pv",
    )(page_table, probs, probs, v_new_pad, *([cache4] * pps))


def _merge_kernel(ap_ref, as_ref, aa_ref, wp_ref, ws_ref, wa_ref, g0_ref, g1_ref, g2_ref, o_ref):
    acc = _sigmoid(g0_ref[...]) * _dot(ap_ref[...], wp_ref[...])
    acc = acc + _sigmoid(g1_ref[...]) * _dot(as_ref[...], ws_ref[...])
    acc = acc + _sigmoid(g2_ref[...]) * _dot(aa_ref[...], wa_ref[...])
    o_ref[...] = acc.astype(BF16)


def _merge(pool_o, ssd_o, attn_o, w_pool, w_ssd, w_attn, proj, col_gate, layer):
    m, d = pool_o.shape
    tm = _pick(m, 512)
    tn = _pick(d, 512)
    gb = col_gate // tn
    nb = d // tn

    def act(width):
        return pl.BlockSpec((tm, width), lambda j, i: (i, 0))

    def wgt(kdim):
        return pl.BlockSpec((None, kdim, tn), lambda j, i: (layer, 0, j))

    def gate(which):
        return pl.BlockSpec((tm, tn), lambda j, i: (i, gb + which * nb + j))

    return pl.pallas_call(
        _merge_kernel,
        grid=(d // tn, m // tm),
        in_specs=[act(pool_o.shape[1]), act(ssd_o.shape[1]), act(attn_o.shape[1]),
                  wgt(w_pool.shape[1]), wgt(w_ssd.shape[1]), wgt(w_attn.shape[1]),
                  gate(0), gate(1), gate(2)],
        out_specs=pl.BlockSpec((tm, tn), lambda j, i: (i, j)),
        out_shape=jax.ShapeDtypeStruct((m, d), BF16),
        compiler_params=_cparams("parallel", "parallel"),
        name="gated_merge",
    )(pool_o, ssd_o, attn_o, w_pool, w_ssd, w_attn, proj, proj, proj)


def _mm_res_kernel(a_ref, w_ref, x_ref, gt_ref, gpost_ref, gpre_ref, sc_ref, sh_ref, xo_ref, u_ref, *scratch, nk):
    def finish(f):
        xn = x_ref[...] + gt_ref[...] * (_rms(f) * gpost_ref[...])
        xo_ref[...] = xn
        y = _rms(xn) * gpre_ref[...]
        u_ref[...] = (y * (1.0 + sc_ref[...]) + sh_ref[...]).astype(BF16)

    part = _dot(a_ref[...], w_ref[...])
    if nk == 1:
        finish(part)
        return
    acc_ref, = scratch
    k = pl.program_id(2)

    @pl.when(k == 0)
    def _():
        acc_ref[...] = part

    @pl.when(k > 0)
    def _():
        acc_ref[...] += part

    @pl.when(k == nk - 1)
    def _():
        finish(acc_ref[...])


def _pick_k(kdim, cap):
    for nk in range(1, kdim // LANES + 1):
        if kdim % nk == 0 and (kdim // nk) % LANES == 0 and kdim // nk <= cap:
            return kdim // nk
    return kdim


def _mm_res(a3, w, layer, x3, gt, g_post, g_pre, sc, sh):
    b, t, kdim = a3.shape
    d = x3.shape[2]
    r = gt.shape[1]
    tm = _pick(t, 512)
    tk = _pick_k(kdim, 2048)
    nk = kdim // tk
    mod_spec = pl.BlockSpec((None, r if r == 1 else tm, d),
                            (lambda bi, i, k: (bi, i, 0)) if r == t else (lambda bi, i, k: (bi, 0, 0)))
    vec_spec = pl.BlockSpec((1, d), lambda bi, i, k: (0, 0))
    row_spec = pl.BlockSpec((None, tm, d), lambda bi, i, k: (bi, i, 0))
    return pl.pallas_call(
        functools.partial(_mm_res_kernel, nk=nk),
        grid=(b, t // tm, nk),
        in_specs=[pl.BlockSpec((None, tm, tk), lambda bi, i, k: (bi, i, k)),
                  pl.BlockSpec((None, tk, d), lambda bi, i, k: (layer, k, 0)),
                  row_spec, mod_spec, vec_spec, vec_spec, mod_spec, mod_spec],
        out_specs=[row_spec, row_spec],
        out_shape=[jax.ShapeDtypeStruct((b, t, d), F32), jax.ShapeDtypeStruct((b, t, d), BF16)],
        scratch_shapes=[pltpu.VMEM((tm, d), F32)] if nk > 1 else [],
        compiler_params=_cparams("parallel", "parallel", "arbitrary"),
        name="matmul_residual_norm",
    )(a3, w, x3, gt, g_post, g_pre, sc, sh)


def _ffn_kernel(u_ref, wg_ref, wu_ref, o_ref):
    u = u_ref[...]
    o_ref[...] = (_silu(_dot(u, wg_ref[...])) * _dot(u, wu_ref[...])).astype(BF16)


def _ffn_hidden(u, w_gate, w_up, layer):
    m, k = u.shape
    n = w_gate.shape[2]
    tm = _pick(m, 1024)
    tn = _pick(n, 512)
    w_spec = pl.BlockSpec((None, k, tn), lambda j, i: (layer, 0, j))
    return pl.pallas_call(
        _ffn_kernel,
        grid=(n // tn, m // tm),
        in_specs=[pl.BlockSpec((tm, k), lambda j, i: (i, 0)), w_spec, w_spec],
        out_specs=pl.BlockSpec((tm, tn), lambda j, i: (i, j)),
        out_shape=jax.ShapeDtypeStruct((m, n), BF16),
        compiler_params=_cparams("parallel", "parallel"),
        name="ffn_hidden",
    )(u, w_gate, w_up)


def kernel(x_prompt, x_sample, cache_k, cache_v, state_ssm, state_conv, state_pool, page_table, c_prompt, c_sample, w_ada, b_ada, g_pre_mix, g_post_mix, g_pre_ffn, g_post_ffn, w_in, w_pool_grp, pool_scale, conv_w, conv_b, dt_bias, a_log, d_skip, ssd_norm, w_pool_br, w_ssd_br, w_attn_br, w_out, w_gate, w_up, w_down):
    bp, tp, d = x_prompt.shape
    bs, ts, _ = x_sample.shape
    depth = w_ada.shape[0]
    hd = d // ATTN_HEADS
    assert hd == LANES and ATTN_KV_HEADS * ATTN_GROUP * ts == LANES
    inner = ssd_norm.shape[1]
    n_heads = inner // SSD_HEADDIM
    conv_dim = conv_w.shape[2]
    attn_w = ATTN_HEADS * hd
    kv_w = ATTN_KV_HEADS * hd
    page = cache_k.shape[2]
    n_pages = page_table.shape[1]
    past = n_pages * page
    assert tp % MOBA_BLOCK == 0 and past % MOBA_BLOCK == 0 and n_heads <= LANES

    col_z = d
    col_xbc = col_z + inner
    o_dt = col_xbc + conv_dim
    o_q = o_dt + n_heads
    w_tail = w_in[:, :, o_q:]
    w_dt = jnp.pad(w_in[:, :, o_dt:o_q], ((0, 0), (0, 0), (0, LANES - n_heads)))
    t_k, t_v, t_g = attn_w, attn_w + kv_w, attn_w + 2 * kv_w
    n_gate = w_in.shape[2] - o_q - t_g

    def project(u2, layer):
        head = _matmul(u2, w_in, layer, 0, o_dt)
        q = _matmul(u2, w_tail, layer, 0, attn_w)
        k = _matmul(u2, w_tail, layer, t_k, kv_w)
        v = _matmul(u2, w_tail, layer, t_v, kv_w)
        gate = _matmul(u2, w_tail, layer, t_g, n_gate)
        dt = _matmul(u2, w_dt, layer, 0, LANES)
        return head, q, k, v, gate, dt

    w_grp16 = w_pool_grp.astype(BF16)
    w_pool16, w_ssd16, w_attn16 = w_pool_br.astype(BF16), w_ssd_br.astype(BF16), w_attn_br.astype(BF16)
    w_out16, w_gate16, w_up16, w_down16 = w_out.astype(BF16), w_gate.astype(BF16), w_up.astype(BF16), w_down.astype(BF16)

    pad_h = LANES - n_heads
    dt_bias_p = jnp.pad(dt_bias, ((0, 0), (0, pad_h)))
    a_log_p = jnp.pad(a_log, ((0, 0), (0, pad_h)))
    d_skip_e = jnp.repeat(d_skip, SSD_HEADDIM, axis=1)

    n_c = bp + bs
    c_all = jnp.pad(jnp.concatenate([c_prompt, c_sample], axis=0), ((0, -n_c % 8), (0, 0)))
    mod = _ada(c_all, w_ada, b_ada)

    def mods(layer, which):
        m = mod[layer, :, which * d:(which + 1) * d]
        m_s = jnp.broadcast_to(m[bp:n_c, None, :], (bs, ts, d)).reshape(1, bs * ts, d)
        return m[:bp, None, :], m_s

    hg_w = (n_heads // SSD_GROUPS) * SSD_HEADDIM
    ts_pad = SSD_CHUNK
    pps = _pick(n_pages, 16)
    eye_kv = jnp.eye(ATTN_KV_HEADS, dtype=F32)

    xp, xs_ = x_prompt, x_sample.reshape(1, bs * ts, d)
    sh1 = mods(0, 0)
    sc1 = mods(0, 1)
    up = _norm_mod(xp, g_pre_mix[0:1], sc1[0], sh1[0])
    us = _norm_mod(xs_, g_pre_mix[0:1], sc1[1], sh1[1])

    outs = {n: [] for n in ("kp", "vp", "hp", "cp", "pp", "ks", "vs", "hs", "cs", "ps")}
    for layer in range(depth):
        gt1, sh2, sc2, gt2 = mods(layer, 2), mods(layer, 3), mods(layer, 4), mods(layer, 5)
        nxt = (layer + 1) % depth
        sh1n, sc1n = mods(nxt, 0), mods(nxt, 1)
        cw, cbias = conv_w[layer], conv_b[layer:layer + 1]

        head, q_p, k_p, v_p, gate_p, dt_p = project(up.reshape(bp * tp, d), layer)
        proj = head.reshape(bp, tp, -1)
        k_p, v_p = k_p.reshape(bp, tp, kv_w), v_p.reshape(bp, tp, kv_w)
        pool_o = _pool(proj, jnp.zeros((bp, POOL_HIST, d), F32), w_grp16, pool_scale[layer:layer + 1], layer, 0)
        ssd_o, h_p = _ssd(proj, dt_p.reshape(bp, tp, LANES), jnp.zeros((bp, CONV_HIST, conv_dim), F32),
                          jnp.zeros((bp, SSD_GROUPS, hg_w, SSD_STATE), F32), cw, cbias,
                          dt_bias_p[layer:layer + 1], a_log_p[layer:layer + 1], d_skip_e[layer:layer + 1],
                          ssd_norm[layer:layer + 1], col_z, col_xbc, tp)
        attn_o = _moba_prompt(q_p.reshape(bp, tp, attn_w), k_p, v_p, 0)
        merged = _merge(pool_o.reshape(bp * tp, d), ssd_o.reshape(bp * tp, inner), attn_o.reshape(bp * tp, attn_w),
                        w_pool16, w_ssd16, w_attn16, gate_p, 0, layer)
        xp, u_ffn = _mm_res(merged.reshape(bp, tp, d), w_out16, layer, xp, gt1[0], g_post_mix[layer:layer + 1],
                            g_pre_ffn[layer:layer + 1], sc2[0], sh2[0])
        hid = _ffn_hidden(u_ffn.reshape(bp * tp, d), w_gate16, w_up16, layer)
        xp, up = _mm_res(hid.reshape(bp, tp, -1), w_down16, layer, xp, gt2[0], g_post_ffn[layer:layer + 1],
                         g_pre_mix[nxt:nxt + 1], sc1n[0], sh1n[0])
        outs["kp"].append(k_p.reshape(bp, tp, ATTN_KV_HEADS, hd))
        outs["vp"].append(v_p.reshape(bp, tp, ATTN_KV_HEADS, hd))
        outs["hp"].append(h_p.reshape(bp, n_heads, SSD_HEADDIM, SSD_STATE))
        outs["cp"].append(proj[:, tp - (SSD_CONV - 1):, col_xbc:col_xbc + conv_dim])
        outs["pp"].append(proj[:, tp - (POOL_HIST - 1):, :d])

        head, q_s, k_s, v_s, gate_s, dt_s = project(us.reshape(bs * ts, d), layer)
        proj = head.reshape(bs, ts, -1)
        k_s, v_s, dt_s = k_s.reshape(bs, ts, kv_w), v_s.reshape(bs, ts, kv_w), dt_s.reshape(bs, ts, LANES)
        pool_buf = jnp.pad(state_pool[layer], ((0, 0), (1, 0), (0, 0)))
        pool_o = _pool(proj, pool_buf, w_grp16, pool_scale[layer:layer + 1], layer, past)
        tpad = ((0, 0), (0, ts_pad - ts), (0, 0))
        conv_prev = jnp.pad(state_conv[layer], ((0, 0), (CONV_HIST - (SSD_CONV - 1), 0), (0, 0)))
        ssd_o, h_s = _ssd(jnp.pad(proj, tpad), jnp.pad(dt_s, tpad), conv_prev,
                          state_ssm[layer].reshape(bs, SSD_GROUPS, hg_w, SSD_STATE), cw, cbias,
                          dt_bias_p[layer:layer + 1], a_log_p[layer:layer + 1], d_skip_e[layer:layer + 1],
                          ssd_norm[layer:layer + 1], col_z, col_xbc, ts)
        ssd_o = ssd_o[:, :ts]
        q5 = q_s.reshape(bs, ts, ATTN_KV_HEADS, ATTN_GROUP, hd)
        wq_t = jnp.einsum("btkgd,kj->bjdkgt", q5, eye_kv).reshape(bs, kv_w, LANES).astype(BF16)
        cache_k4 = cache_k.reshape(depth, cache_k.shape[1], page * ATTN_KV_HEADS, hd)
        cache_v4 = cache_v.reshape(depth, cache_v.shape[1], page * ATTN_KV_HEADS, hd)
        scores = _moba_scores(cache_k4, layer, page_table, wq_t, pps)
        probs = _moba_softmax(scores, k_s, wq_t, ts)
        o_full = _moba_pv(cache_v4, layer, page_table, probs, jnp.pad(v_s, ((0, 0), (0, LANES - ts), (0, 0))), pps)
        o6 = o_full.reshape(bs, ATTN_KV_HEADS, ATTN_GROUP, ts, ATTN_KV_HEADS, hd)
        attn_o = jnp.einsum("bkgtjd,kj->btkgd", o6, eye_kv).reshape(bs * ts, attn_w).astype(BF16)
        merged = _merge(pool_o.reshape(bs * ts, d), ssd_o.reshape(bs * ts, inner), attn_o,
                        w_pool16, w_ssd16, w_attn16, gate_s, 0, layer)
        xs_, u_ffn = _mm_res(merged.reshape(1, bs * ts, d), w_out16, layer, xs_, gt1[1], g_post_mix[layer:layer + 1],
                             g_pre_ffn[layer:layer + 1], sc2[1], sh2[1])
        hid = _ffn_hidden(u_ffn.reshape(bs * ts, d), w_gate16, w_up16, layer)
        xs_, us = _mm_res(hid.reshape(1, bs * ts, -1), w_down16, layer, xs_, gt2[1], g_post_ffn[layer:layer + 1],
                          g_pre_mix[nxt:nxt + 1], sc1n[1], sh1n[1])
        outs["ks"].append(k_s.reshape(bs, ts, ATTN_KV_HEADS, hd))
        outs["vs"].append(v_s.reshape(bs, ts, ATTN_KV_HEADS, hd))
        outs["hs"].append(h_s.reshape(bs, n_heads, SSD_HEADDIM, SSD_STATE))
        conv_ext = jnp.concatenate([state_conv[layer], proj[:, :, col_xbc:col_xbc + conv_dim]], axis=1)
        outs["cs"].append(conv_ext[:, ts:])
        pool_ext = jnp.concatenate([state_pool[layer], proj[:, :, :d]], axis=1)
        outs["ps"].append(pool_ext[:, ts:])

    st = {n: jnp.stack(v) for n, v in outs.items()}
    return (xp, xs_.reshape(bs, ts, d), st["kp"], st["vp"], st["hp"], st["cp"], st["pp"],
            st["ks"], st["vs"], st["hs"], st["cs"], st["ps"])
```

```python
import functools

import jax
import jax.numpy as jnp
from jax import lax
from jax.experimental import pallas as pl
from jax.experimental.pallas import tpu as pltpu

F32 = jnp.float32
BF16 = jnp.bfloat16

NORM_EPS = 1e-6
POOL_WINDOWS = (2, 4, 8, 16)
POOL_HIST = 16
SSD_HEADDIM = 64
SSD_GROUPS = 8
SSD_STATE = 128
SSD_CONV = 4
SSD_CHUNK = 128
SSD_GROUPS_PER_STEP = 4
CONV_HIST = 8
ATTN_HEADS = 16
ATTN_KV_HEADS = 8
ATTN_GROUP = ATTN_HEADS // ATTN_KV_HEADS
MOBA_BLOCK = 256
MOBA_TOPK = 3
PAST_GROUP = 4
LANES = 128
VMEM_LIMIT_BYTES = 56 * 1024 * 1024
NEG_INF = float("-inf")
LOG2E = 1.4426950408889634


def _cparams(*sem):
    return pltpu.CompilerParams(dimension_semantics=sem, vmem_limit_bytes=VMEM_LIMIT_BYTES)


def _sigmoid(x):
    return 1.0 / (1.0 + jnp.exp(-x))


def _silu(x):
    return x * _sigmoid(x)


def _dot(a, b):
    return jnp.dot(a, b, preferred_element_type=F32)


def _dot_nt(a, b):
    return lax.dot_general(a, b, (((1,), (1,)), ((), ())), preferred_element_type=F32)


def _dot_tn(a, b):
    return lax.dot_general(a, b, (((0,), (0,)), ((), ())), preferred_element_type=F32)


def _rms(x):
    return x * lax.rsqrt(jnp.mean(x * x, axis=-1, keepdims=True) + NORM_EPS)


def _pick(total, pref):
    if total <= pref:
        return total
    t = pref
    while total % t:
        t //= 2
    return t


def _ada_kernel(c_ref, w_ref, b_ref, o_ref):
    a = _silu(c_ref[...]).astype(BF16)
    o_ref[...] = _dot(a, w_ref[...].astype(BF16)) + b_ref[...]


def _ada(c_all, w_ada, b_ada):
    depth, d, n = w_ada.shape
    rows = c_all.shape[0]
    tn = _pick(n, 1024)
    return pl.pallas_call(
        _ada_kernel,
        grid=(depth, n // tn),
        in_specs=[pl.BlockSpec((rows, d), lambda l, j: (0, 0)),
                  pl.BlockSpec((None, d, tn), lambda l, j: (l, 0, j)),
                  pl.BlockSpec((None, 1, tn), lambda l, j: (l, 0, j))],
        out_specs=pl.BlockSpec((None, rows, tn), lambda l, j: (l, 0, j)),
        out_shape=jax.ShapeDtypeStruct((depth, rows, n), F32),
        compiler_params=_cparams("parallel", "parallel"),
        name="ada_mod",
    )(c_all, w_ada, b_ada.reshape(depth, 1, n))


def _norm_mod_kernel(x_ref, g_ref, sc_ref, sh_ref, u_ref):
    y = _rms(x_ref[...]) * g_ref[...]
    u_ref[...] = (y * (1.0 + sc_ref[...]) + sh_ref[...]).astype(BF16)


def _norm_mod(x, g, sc, sh):
    b, t, d = x.shape
    r = sc.shape[1]
    tt = _pick(t, 512)
    mod_spec = pl.BlockSpec((None, r if r == 1 else tt, d),
                            (lambda i, j: (i, j, 0)) if r == t else (lambda i, j: (i, 0, 0)))
    return pl.pallas_call(
        _norm_mod_kernel,
        grid=(b, t // tt),
        in_specs=[pl.BlockSpec((None, tt, d), lambda i, j: (i, j, 0)),
                  pl.BlockSpec((1, d), lambda i, j: (0, 0)),
                  mod_spec, mod_spec],
        out_specs=pl.BlockSpec((None, tt, d), lambda i, j: (i, j, 0)),
        out_shape=jax.ShapeDtypeStruct((b, t, d), BF16),
        compiler_params=_cparams("parallel", "parallel"),
        name="norm_mod",
    )(x, g, sc, sh)


def _mm_kernel(a_ref, w_ref, o_ref, w16_ref):
    @pl.when(pl.program_id(1) == 0)
    def _():
        w16_ref[...] = w_ref[0].astype(BF16)

    o_ref[...] = _dot_nt(a_ref[...], w16_ref[...])


def _matmul_wt(a, w_t, layer, row0, n, skip_at, skip):
    m, k = a.shape
    tm = _pick(m, 1024)
    tn = _pick(n, 1024)
    assert skip_at % tn == 0
    hole = skip_at // tn

    def w_index(j, i):
        return layer, pl.multiple_of(row0 + j * tn + jnp.where(j >= hole, skip, 0), 8), 0

    return pl.pallas_call(
        _mm_kernel,
        grid=(n // tn, m // tm),
        in_specs=[pl.BlockSpec((tm, k), lambda j, i: (i, 0)),
                  pl.BlockSpec((pl.Element(1), pl.Element(tn), pl.Element(k)), w_index)],
        out_specs=pl.BlockSpec((tm, tn), lambda j, i: (i, j)),
        out_shape=jax.ShapeDtypeStruct((m, n), F32),
        scratch_shapes=[pltpu.VMEM((tn, k), BF16)],
        compiler_params=_cparams("parallel", "arbitrary"),
        name="proj_matmul",
    )(a, w_t)


def _pool_kernel(p_ref, buf_ref, w_ref, sc_ref, o_ref, ext_ref, *, tt, start, pg):
    t = pl.program_id(1)

    @pl.when(t == 0)
    def _():
        ext_ref[0:POOL_HIST, :] = buf_ref[...]

    ext_ref[POOL_HIST:POOL_HIST + tt, :] = p_ref[...]
    n_valid = lax.broadcasted_iota(jnp.int32, (tt, 1), 0) + (t * tt + start + 1)
    for gi, w in enumerate(POOL_WINDOWS):
        c0 = gi * pg
        x = ext_ref[POOL_HIST:POOL_HIST + tt, c0:c0 + pg]
        s = x
        for k in range(1, w):
            s = s + ext_ref[POOL_HIST - k:POOL_HIST - k + tt, c0:c0 + pg]
        cnt = jnp.minimum(n_valid, w).astype(F32)
        mixed = (s / cnt - x).astype(BF16)
        y = _dot(mixed, w_ref[gi]) * sc_ref[:, c0:c0 + pg]
        o_ref[:, c0:c0 + pg] = y.astype(BF16)
    ext_ref[0:POOL_HIST, :] = ext_ref[tt:tt + POOL_HIST, :]


def _pool(proj3, buf, w_grp, scale, layer, start):
    b, t, _ = proj3.shape
    c = scale.shape[1]
    pg = c // len(POOL_WINDOWS)
    tt = _pick(t, 256)
    return pl.pallas_call(
        functools.partial(_pool_kernel, tt=tt, start=start, pg=pg),
        grid=(b, t // tt),
        in_specs=[pl.BlockSpec((None, tt, c), lambda i, j: (i, j, 0)),
                  pl.BlockSpec((None, POOL_HIST, c), lambda i, j: (i, 0, 0)),
                  pl.BlockSpec((None, len(POOL_WINDOWS), pg, pg), lambda i, j: (layer, 0, 0, 0)),
                  pl.BlockSpec((1, c), lambda i, j: (0, 0))],
        out_specs=pl.BlockSpec((None, tt, c), lambda i, j: (i, j, 0)),
        out_shape=jax.ShapeDtypeStruct((b, t, c), BF16),
        scratch_shapes=[pltpu.VMEM((POOL_HIST + tt, c), F32)],
        compiler_params=_cparams("parallel", "arbitrary"),
        name="pool_mixer",
    )(proj3, buf, w_grp, scale)


def _ssd_kernel(z_ref, x_ref, b_ref, c_ref, dt_ref, cpx_ref, cpb_ref, cpc_ref,
                px_ref, pb_ref, pc_ref, pdt_ref, exp_ref, h0_ref,
                y_ref, hout_ref, histx, histb, histc, ht, *, t_valid, n_chunks, hg, gps):
    gq = pl.program_id(1)
    c = pl.program_id(2)
    lc = SSD_CHUNK
    gw = hg * SSD_HEADDIM
    ns = SSD_STATE

    @pl.when(c == 0)
    def _():
        histx[...] = cpx_ref[...]
        histb[...] = cpb_ref[...]
        histc[...] = cpc_ref[...]
        for s in range(gps):
            ht[s] = h0_ref[s].T

    def conv(hist, raw_ref, par_ref):
        x = raw_ref[...]
        prev = hist[...]
        hist[...] = x[lc - CONV_HIST:, :]
        row8 = lax.broadcasted_iota(jnp.int32, prev.shape, 0)
        acc = par_ref[SSD_CONV:SSD_CONV + 1, :] + x * par_ref[SSD_CONV - 1:SSD_CONV, :]
        for k in range(1, SSD_CONV):
            r = pltpu.roll(x, k, axis=0)
            first = jnp.where(row8 < k, pltpu.roll(prev, k, axis=0), r[0:CONV_HIST])
            shifted = jnp.concatenate([first, r[CONV_HIST:]], axis=0)
            acc = acc + shifted * par_ref[SSD_CONV - 1 - k:SSD_CONV - k, :]
        return _silu(acc)

    xs_all = conv(histx, x_ref, px_ref)
    bm_all = conv(histb, b_ref, pb_ref)
    cm_all = conv(histc, c_ref, pc_ref)

    row = lax.broadcasted_iota(jnp.int32, (lc, LANES), 0)
    col = lax.broadcasted_iota(jnp.int32, (lc, LANES), 1)
    xdt = dt_ref[...] + pdt_ref[0:1, :]
    dt = jnp.maximum(xdt, 0.0) + jnp.log1p(jnp.exp(-jnp.abs(xdt)))
    dt = jnp.where(c * lc + row < t_valid, dt, 0.0)
    dta = dt * (-jnp.exp(pdt_ref[1:2, :]))
    acs = dta
    k = 1
    while k < lc:
        acs = acs + jnp.where(row >= k, pltpu.roll(acs, k, axis=0), 0.0)
        k *= 2
    a_last = acs[lc - 1:lc, :]
    w_end = dt * jnp.exp(a_last - acs)
    exp_a = jnp.exp(acs)

    cd = jnp.exp(a_last)
    cd_hi = cd.astype(BF16).astype(F32)
    cd_mid = (cd - cd_hi).astype(BF16).astype(F32)
    cd_lo = cd - cd_hi - cd_mid
    pieces = [jnp.broadcast_to(p, (16, LANES)) for p in (cd_hi, cd_mid, cd_lo)]
    stacked = jnp.concatenate([dt, w_end, exp_a] + pieces, axis=0).astype(BF16)
    acs2 = acs * LOG2E
    causal = row >= col
    heads_here = LANES // SSD_HEADDIM

    for s in range(gps):
        xl, nl = slice(s * gw, (s + 1) * gw), slice(s * ns, (s + 1) * ns)
        xs, bm, cm = xs_all[:, xl], bm_all[:, nl], cm_all[:, nl]
        wide = _dot(stacked, exp_ref[s])
        dt_e, w_e, ea_e = wide[0:lc], wide[lc:2 * lc], wide[2 * lc:3 * lc]
        r0 = 3 * lc
        cd_e = wide[r0:r0 + 1] + wide[r0 + 16:r0 + 17] + wide[r0 + 32:r0 + 33]

        acs_g = pltpu.roll(acs2, (LANES - (gq * gps + s) * hg) % LANES, axis=1)
        acs_gt = acs_g.T

        bm16 = bm.astype(BF16)
        cm16 = cm.astype(BF16)
        cb = jnp.where(causal, _dot_nt(cm16, bm16), 0.0)
        xdt_e = xs * dt_e
        y_parts = []
        for pr in range(gw // LANES):
            xp = xdt_e[:, pr * LANES:(pr + 1) * LANES]
            lhs, rhs = [], []
            for hh in range(heads_here):
                h = pr * heads_here + hh
                seg = acs_g[:, h:h + 1] - acs_gt[h:h + 1, :]
                lhs.append((cb * jnp.exp2(jnp.minimum(seg, 0.0))).astype(BF16))
                in_head = (col >= hh * SSD_HEADDIM) & (col < (hh + 1) * SSD_HEADDIM)
                rhs.append(jnp.where(in_head, xp, 0.0).astype(BF16))
            y_parts.append(_dot(jnp.concatenate(lhs, axis=1), jnp.concatenate(rhs, axis=0)))
        y = jnp.concatenate(y_parts, axis=1)

        h_in = ht[s]
        y = y + _dot(cm16, h_in.astype(BF16)) * ea_e
        h_new = h_in * cd_e + _dot(bm.T.astype(BF16), (xs * w_e).astype(BF16))
        ht[s] = h_new

        y = y + px_ref[SSD_CONV + 1:SSD_CONV + 2, xl] * xs
        y = y * _silu(z_ref[:, xl])
        y_ref[:, xl] = (_rms(y) * px_ref[SSD_CONV + 2:SSD_CONV + 3, xl]).astype(BF16)

    @pl.when(c == n_chunks - 1)
    def _():
        for s in range(gps):
            hout_ref[s] = ht[s].T


def _ssd_params(conv_w, conv_b, dt_bias, a_log, d_skip, ssd_norm):
    depth, inner = ssd_norm.shape
    n_heads = dt_bias.shape[1]
    rows = lambda v: v[:, None, :]
    px = jnp.concatenate([conv_w[:, :, :inner], rows(conv_b[:, :inner]), rows(jnp.repeat(d_skip, SSD_HEADDIM, axis=1)),
                          rows(ssd_norm), jnp.zeros((depth, 8 - SSD_CONV - 3, inner), F32)], axis=1)
    pbc = jnp.concatenate([conv_w[:, :, inner:], rows(conv_b[:, inner:]),
                           jnp.zeros((depth, 8 - SSD_CONV - 1, conv_w.shape[2] - inner), F32)], axis=1)
    pdt = jnp.pad(jnp.stack([dt_bias, a_log], axis=1), ((0, 0), (0, 6), (0, LANES - n_heads)))
    return px, pbc, pdt


def _ssd(proj3, dt3, conv_prev, h0, params, layer, col_z, col_xbc, t_valid):
    b, tp, _ = proj3.shape
    px, pbc, pdt = params
    inner = px.shape[2]
    n_heads = inner // SSD_HEADDIM
    hg = n_heads // SSD_GROUPS
    gw = hg * SSD_HEADDIM
    ns = SSD_STATE
    lc = SSD_CHUNK
    n_chunks = tp // lc
    gps = SSD_GROUPS_PER_STEP
    xw, nw = gps * gw, gps * ns
    assert col_z % xw == 0 and col_xbc % xw == 0 and (col_xbc + inner) % nw == 0 and SSD_GROUPS % gps == 0
    zb, xb = col_z // xw, col_xbc // xw
    bb = (col_xbc + inner) // nw
    cb_ = (col_xbc + inner + SSD_GROUPS * ns) // nw
    cxb, cbb, ccb = 0, inner // nw, (inner + SSD_GROUPS * ns) // nw
    head_of = (jnp.arange(SSD_GROUPS)[:, None] * hg + jnp.arange(gw)[None, :] // SSD_HEADDIM)
    expand = (jnp.arange(LANES)[None, :, None] == head_of[:, None, :]).astype(BF16)

    def seq(width, off):
        return pl.BlockSpec((None, lc, width), lambda i, g, c: (i, c, off + g))

    def prev(width, off):
        return pl.BlockSpec((None, CONV_HIST, width), lambda i, g, c: (i, 0, off + g))

    def par(width, off):
        return pl.BlockSpec((None, 8, width), lambda i, g, c: (layer, 0, off + g))

    state_spec = pl.BlockSpec((None, gps, gw, ns), lambda i, g, c: (i, g, 0, 0))
    kernel = functools.partial(_ssd_kernel, t_valid=t_valid, n_chunks=n_chunks, hg=hg, gps=gps)
    return pl.pallas_call(
        kernel,
        grid=(b, SSD_GROUPS // gps, n_chunks),
        in_specs=[seq(xw, zb), seq(xw, xb), seq(nw, bb), seq(nw, cb_),
                  pl.BlockSpec((None, lc, LANES), lambda i, g, c: (i, c, 0)),
                  prev(xw, cxb), prev(nw, cbb), prev(nw, ccb),
                  par(xw, 0), par(nw, 0), par(nw, SSD_GROUPS // gps),
                  pl.BlockSpec((None, 8, LANES), lambda i, g, c: (layer, 0, 0)),
                  pl.BlockSpec((gps, LANES, gw), lambda i, g, c: (g, 0, 0)),
                  state_spec],
        out_specs=[pl.BlockSpec((None, lc, xw), lambda i, g, c: (i, c, g)), state_spec],
        out_shape=[jax.ShapeDtypeStruct((b, tp, inner), BF16),
                   jax.ShapeDtypeStruct((b, SSD_GROUPS, gw, ns), F32)],
        scratch_shapes=[pltpu.VMEM((CONV_HIST, xw), F32),
                        pltpu.VMEM((CONV_HIST, nw), F32),
                        pltpu.VMEM((CONV_HIST, nw), F32),
                        pltpu.VMEM((gps, ns, gw), F32)],
        compiler_params=_cparams("parallel", "parallel", "arbitrary"),
        name="ssd_mixer",
    )(proj3, proj3, proj3, proj3, dt3, conv_prev, conv_prev, conv_prev, px, pbc, pbc, pdt, expand, h0)


def _top_k_mask(gate, valid, ids, axis):
    big = jnp.int32(2 ** 30)
    remaining = valid
    sel = jnp.zeros(gate.shape, F32)
    for _ in range(MOBA_TOPK):
        gm = jnp.where(remaining > 0.5, gate, NEG_INF)
        m = jnp.max(gm, axis=axis, keepdims=True)
        cand = jnp.where(gm == m, remaining, 0.0)
        first = jnp.min(jnp.where(cand > 0.5, ids, big), axis=axis, keepdims=True)
        pick = jnp.where(ids == first, 1.0, 0.0)
        sel = sel + pick
        remaining = remaining - pick
    return sel


def _moba_prompt_kernel(q_ref, k_ref, v_ref, o_ref, kb, vtb, km, sel_ref, qs_ref, acc_ref, ml_ref, *, n_blocks, scale):
    i = pl.program_id(2)
    blk = MOBA_BLOCK
    hd = LANES
    nq = ATTN_GROUP * blk
    c2 = scale * LOG2E

    @pl.when(i == 0)
    def _():
        km[...] = jnp.zeros(km.shape, F32)
        for j in range(n_blocks):
            kj = k_ref[j * blk:(j + 1) * blk, :]
            kb[j] = kj.astype(BF16)
            km[j:j + 1, :] = jnp.mean(kj, axis=0, keepdims=True)
            vtb[j] = v_ref[j * blk:(j + 1) * blk, :].T.astype(BF16)

    q = q_ref[...]
    qs_ref[...] = jnp.concatenate([q[:, g * hd:(g + 1) * hd] for g in range(ATTN_GROUP)], axis=0).astype(BF16)
    gate_t = _dot_nt(km[...].astype(BF16), qs_ref[...])
    ids = lax.broadcasted_iota(jnp.int32, (km.shape[0], nq), 0)
    valid = jnp.where(ids < i, 1.0, 0.0)
    sel_ref[...] = _top_k_mask(gate_t, valid, ids, axis=0)

    key_r = lax.broadcasted_iota(jnp.int32, (blk, nq), 0)
    qry_c = lax.broadcasted_iota(jnp.int32, (blk, nq), 1) & (blk - 1)
    s = _dot_nt(kb[i], qs_ref[...])
    s = jnp.where(key_r <= qry_c, s, NEG_INF)
    m = jnp.max(s, axis=0, keepdims=True)
    p = jnp.exp2((s - m) * c2)
    ml_ref[0:1, :] = m
    ml_ref[1:2, :] = jnp.sum(p, axis=0, keepdims=True)
    acc_ref[...] = _dot(vtb[i], p.astype(BF16))

    def past_blocks(j0, nb):
        m_old, l_old = ml_ref[0:1, :], ml_ref[1:2, :]
        ss = [_dot_nt(kb[j0 + k], qs_ref[...]) for k in range(nb)]
        picked = [sel_ref[pl.ds(j0 + k, 1), :] > 0.5 for k in range(nb)]
        m_new = m_old
        for k in range(nb):
            m_new = jnp.maximum(m_new, jnp.where(picked[k], jnp.max(ss[k], axis=0, keepdims=True), NEG_INF))
        alpha = jnp.exp2((m_old - m_new) * c2)
        l_new = alpha * l_old
        pv = None
        for k in range(nb):
            p = jnp.exp2((ss[k] - jnp.where(picked[k], m_new, jnp.inf)) * c2)
            l_new = l_new + jnp.sum(p, axis=0, keepdims=True)
            d = _dot(vtb[j0 + k], p.astype(BF16))
            pv = d if pv is None else pv + d
        acc_ref[...] = alpha * acc_ref[...] + pv
        ml_ref[0:1, :] = m_new
        ml_ref[1:2, :] = l_new

    def group_body(t, carry):
        past_blocks(t * PAST_GROUP, PAST_GROUP)
        return carry

    lax.fori_loop(0, i // PAST_GROUP, group_body, 0)
    done = (i // PAST_GROUP) * PAST_GROUP
    nb = PAST_GROUP // 2
    while nb >= 1:
        @pl.when((i & nb) != 0)
        def _(nb=nb, done=done):
            past_blocks(done, nb)

        done = done + (i & nb)
        nb //= 2

    o = (acc_ref[...] / ml_ref[1:2, :]).T
    o_ref[...] = jnp.concatenate([o[g * blk:(g + 1) * blk] for g in range(ATTN_GROUP)], axis=1).astype(BF16)


def _moba_prompt(proj3, col_q, col_k, col_v):
    b, t, _ = proj3.shape
    hd = LANES
    blk = MOBA_BLOCK
    n_blocks = t // blk
    nb_pad = -(-n_blocks // 16) * 16
    qw = ATTN_GROUP * hd
    qb, kb0, vb0 = col_q // qw, col_k // hd, col_v // hd
    kernel = functools.partial(_moba_prompt_kernel, n_blocks=n_blocks, scale=hd ** -0.5)
    return pl.pallas_call(
        kernel,
        grid=(b, ATTN_KV_HEADS, n_blocks),
        in_specs=[pl.BlockSpec((None, blk, qw), lambda bi, h, i: (bi, i, qb + h)),
                  pl.BlockSpec((None, t, hd), lambda bi, h, i: (bi, 0, kb0 + h)),
                  pl.BlockSpec((None, t, hd), lambda bi, h, i: (bi, 0, vb0 + h))],
        out_specs=pl.BlockSpec((None, blk, qw), lambda bi, h, i: (bi, i, h)),
        out_shape=jax.ShapeDtypeStruct((b, t, ATTN_HEADS * hd), BF16),
        scratch_shapes=[pltpu.VMEM((n_blocks, blk, hd), BF16), pltpu.VMEM((n_blocks, hd, blk), BF16),
                        pltpu.VMEM((nb_pad, hd), F32), pltpu.VMEM((nb_pad, ATTN_GROUP * blk), F32),
                        pltpu.VMEM((ATTN_GROUP * blk, hd), BF16), pltpu.VMEM((hd, ATTN_GROUP * blk), F32),
                        pltpu.VMEM((8, ATTN_GROUP * blk), F32)],
        compiler_params=_cparams("parallel", "parallel", "arbitrary"),
        name="moba_prompt",
    )(proj3, proj3, proj3)


def _moba_scores_kernel(pt_ref, wq_ref, *refs, pps, page):
    del pt_ref
    k_refs, o_ref = refs[:pps], refs[pps]
    hd = LANES
    for r in range(pps):
        acc = None
        for h in range(ATTN_KV_HEADS):
            kh = k_refs[r][pl.ds(h, page, stride=ATTN_KV_HEADS), :].astype(BF16)
            part = _dot(kh, wq_ref[h * hd:(h + 1) * hd, :])
            acc = part if acc is None else acc + part
        o_ref[r * page:(r + 1) * page, :] = acc


def _moba_scores(cache4, layer, page_table, wq_t, pps):
    b, n_pages = page_table.shape
    rows, hd = cache4.shape[2], cache4.shape[3]
    page, width = rows // ATTN_KV_HEADS, hd * ATTN_KV_HEADS

    def page_spec(r):
        return pl.BlockSpec((None, None, rows, hd), lambda bi, p, pt: (layer, pt[bi, p * pps + r], 0, 0))

    grid_spec = pltpu.PrefetchScalarGridSpec(
        num_scalar_prefetch=1,
        grid=(b, n_pages // pps),
        in_specs=[pl.BlockSpec((None, width, LANES), lambda bi, p, pt: (bi, 0, 0))]
                 + [page_spec(r) for r in range(pps)],
        out_specs=pl.BlockSpec((None, pps * page, LANES), lambda bi, p, pt: (bi, p, 0)),
    )
    return pl.pallas_call(
        functools.partial(_moba_scores_kernel, pps=pps, page=page),
        grid_spec=grid_spec,
        out_shape=jax.ShapeDtypeStruct((b, n_pages * page, LANES), F32),
        compiler_params=_cparams("parallel", "arbitrary"),
        name="moba_sample_scores",
    )(page_table, wq_t, *([cache4] * pps))


def _moba_softmax_kernel(st_ref, kn_ref, wq_ref, p_ref, gate_ref, sel_ref, *, n_blocks, tq, scale):
    blk = MOBA_BLOCK
    past = n_blocks * blk
    gate_ref[...] = jnp.zeros(gate_ref.shape, F32)

    def gate_body(j, carry):
        off = pl.multiple_of(j * blk, blk)
        gate_ref[pl.ds(j, 1), :] = jnp.mean(st_ref[pl.ds(off, blk), :], axis=0, keepdims=True)
        return carry

    lax.fori_loop(0, n_blocks, gate_body, 0)
    nrow = gate_ref.shape[0]
    ids = lax.broadcasted_iota(jnp.int32, (nrow, LANES), 0)
    valid = jnp.where(ids < n_blocks, 1.0, 0.0)
    sel_ref[...] = _top_k_mask(gate_ref[...], valid, ids, axis=0)

    s_own = _dot(kn_ref[...].astype(BF16), wq_ref[...]) * scale
    key_t = lax.broadcasted_iota(jnp.int32, (tq, LANES), 0)
    qry_t = lax.broadcasted_iota(jnp.int32, (tq, LANES), 1) & (tq - 1)
    s_own = jnp.where(key_t <= qry_t, s_own, NEG_INF)
    m0 = jnp.max(s_own, axis=0, keepdims=True)

    def max_body(j, m):
        off = pl.multiple_of(j * blk, blk)
        bm = jnp.max(st_ref[pl.ds(off, blk), :], axis=0, keepdims=True) * scale
        return jnp.maximum(m, jnp.where(sel_ref[pl.ds(j, 1), :] > 0.5, bm, NEG_INF))

    m = lax.fori_loop(0, n_blocks, max_body, m0)
    p_own = jnp.exp(s_own - m)
    l0 = jnp.sum(p_own, axis=0, keepdims=True)

    def exp_body(j, l):
        off = pl.multiple_of(j * blk, blk)
        s = st_ref[pl.ds(off, blk), :] * scale
        p = jnp.where(sel_ref[pl.ds(j, 1), :] > 0.5, jnp.exp(s - m), 0.0)
        p_ref[pl.ds(off, blk), :] = p
        return l + jnp.sum(p, axis=0, keepdims=True)

    l = lax.fori_loop(0, n_blocks, exp_body, l0)
    inv = 1.0 / l

    def norm_body(j, carry):
        off = pl.multiple_of(j * blk, blk)
        p_ref[pl.ds(off, blk), :] = p_ref[pl.ds(off, blk), :] * inv
        return carry

    lax.fori_loop(0, n_blocks, norm_body, 0)
    p_ref[past:past + LANES, :] = jnp.zeros((LANES, LANES), F32)
    p_ref[past:past + tq, :] = p_own * inv


def _moba_softmax(scores, k_new, wq_t, tq):
    b, past, _ = scores.shape
    width = k_new.shape[2]
    n_blocks = past // MOBA_BLOCK
    nrow = -(-n_blocks // 8) * 8
    kernel = functools.partial(_moba_softmax_kernel, n_blocks=n_blocks, tq=tq, scale=LANES ** -0.5)
    return pl.pallas_call(
        kernel,
        grid=(b,),
        in_specs=[pl.BlockSpec((None, past, LANES), lambda bi: (bi, 0, 0)),
                  pl.BlockSpec((None, tq, width), lambda bi: (bi, 0, 0)),
                  pl.BlockSpec((None, width, LANES), lambda bi: (bi, 0, 0))],
        out_specs=pl.BlockSpec((None, past + LANES, LANES), lambda bi: (bi, 0, 0)),
        out_shape=jax.ShapeDtypeStruct((b, past + LANES, LANES), F32),
        scratch_shapes=[pltpu.VMEM((nrow, LANES), F32), pltpu.VMEM((nrow, LANES), F32)],
        compiler_params=_cparams("parallel"),
        name="moba_sample_softmax",
    )(scores, k_new, wq_t)


def _moba_pv_kernel(pt_ref, p_ref, pown_ref, vnew_ref, *refs, pps, page):
    del pt_ref
    v_refs, o_ref = refs[:pps], refs[pps]

    @pl.when(pl.program_id(1) == 0)
    def _():
        o_ref[...] = _dot_tn(pown_ref[...].astype(BF16), vnew_ref[...].astype(BF16))

    hd = LANES
    for h in range(ATTN_KV_HEADS):
        acc = o_ref[:, h * hd:(h + 1) * hd]
        for r in range(pps):
            vh = v_refs[r][pl.ds(h, page, stride=ATTN_KV_HEADS), :].astype(BF16)
            acc = acc + _dot_tn(p_ref[r * page:(r + 1) * page, :].astype(BF16), vh)
        o_ref[:, h * hd:(h + 1) * hd] = acc


def _moba_pv(cache4, layer, page_table, probs, v_new_pad, pps):
    b, n_pages = page_table.shape
    rows, hd = cache4.shape[2], cache4.shape[3]
    page, width = rows // ATTN_KV_HEADS, hd * ATTN_KV_HEADS
    own_blk = (n_pages * page) // LANES

    def page_spec(r):
        return pl.BlockSpec((None, None, rows, hd), lambda bi, p, pt: (layer, pt[bi, p * pps + r], 0, 0))

    grid_spec = pltpu.PrefetchScalarGridSpec(
        num_scalar_prefetch=1,
        grid=(b, n_pages // pps),
        in_specs=[pl.BlockSpec((None, pps * page, LANES), lambda bi, p, pt: (bi, p, 0)),
                  pl.BlockSpec((None, LANES, LANES), lambda bi, p, pt: (bi, own_blk, 0)),
                  pl.BlockSpec((None, LANES, width), lambda bi, p, pt: (bi, 0, 0))]
                 + [page_spec(r) for r in range(pps)],
        out_specs=pl.BlockSpec((None, LANES, width), lambda bi, p, pt: (bi, 0, 0)),
    )
    return pl.pallas_call(
        functools.partial(_moba_pv_kernel, pps=pps, page=page),
        grid_spec=grid_spec,
        out_shape=jax.ShapeDtypeStruct((b, LANES, width), F32),
        compiler_params=_cparams("parallel", "arbitrary"),
        name="moba_sample_pv",
    )(page_table, probs, probs, v_new_pad, *([cache4] * pps))


def _merge_kernel(ap_ref, as_ref, aa_ref, wp_ref, ws_ref, wa_ref, g0_ref, g1_ref, g2_ref, o_ref):
    acc = _sigmoid(g0_ref[...]) * _dot(ap_ref[...], wp_ref[...])
    acc = acc + _sigmoid(g1_ref[...]) * _dot(as_ref[...], ws_ref[...])
    acc = acc + _sigmoid(g2_ref[...]) * _dot(aa_ref[...], wa_ref[...])
    o_ref[...] = acc.astype(BF16)


def _merge(pool_o, ssd_o, attn_o, w_pool, w_ssd, w_attn, proj, col_gate, layer):
    m, d = pool_o.shape
    tm = _pick(m, 512)
    tn = _pick(d, 512)
    gb = col_gate // tn
    nb = d // tn

    def act(width):
        return pl.BlockSpec((tm, width), lambda j, i: (i, 0))

    def wgt(kdim):
        return pl.BlockSpec((None, kdim, tn), lambda j, i: (layer, 0, j))

    def gate(which):
        return pl.BlockSpec((tm, tn), lambda j, i: (i, gb + which * nb + j))

    return pl.pallas_call(
        _merge_kernel,
        grid=(d // tn, m // tm),
        in_specs=[act(pool_o.shape[1]), act(ssd_o.shape[1]), act(attn_o.shape[1]),
                  wgt(w_pool.shape[1]), wgt(w_ssd.shape[1]), wgt(w_attn.shape[1]),
                  gate(0), gate(1), gate(2)],
        out_specs=pl.BlockSpec((tm, tn), lambda j, i: (i, j)),
        out_shape=jax.ShapeDtypeStruct((m, d), BF16),
        compiler_params=_cparams("parallel", "parallel"),
        name="gated_merge",
    )(pool_o, ssd_o, attn_o, w_pool, w_ssd, w_attn, proj, proj, proj)


def _mm_res_kernel(a_ref, w_ref, x_ref, gt_ref, gpost_ref, gpre_ref, sc_ref, sh_ref, xo_ref, u_ref, *, nk):
    def finish(f):
        xn = x_ref[...] + gt_ref[...] * (_rms(f) * gpost_ref[...])
        xo_ref[...] = xn
        y = _rms(xn) * gpre_ref[...]
        u_ref[...] = (y * (1.0 + sc_ref[...]) + sh_ref[...]).astype(BF16)

    part = _dot(a_ref[...], w_ref[...])
    if nk == 1:
        finish(part)
        return
    k = pl.program_id(2)

    @pl.when(k == 0)
    def _():
        xo_ref[...] = part

    @pl.when((k > 0) & (k < nk - 1))
    def _():
        xo_ref[...] += part

    @pl.when(k == nk - 1)
    def _():
        finish(xo_ref[...] + part)


def _pick_k(kdim, cap):
    for nk in range(1, kdim // LANES + 1):
        if kdim % nk == 0 and (kdim // nk) % LANES == 0 and kdim // nk <= cap:
            return kdim // nk
    return kdim


def _mm_res(a3, w, layer, x3, gt, g_post, g_pre, sc, sh):
    b, t, kdim = a3.shape
    d = x3.shape[2]
    r = gt.shape[1]
    tm = _pick(t, 512)
    tk = _pick_k(kdim, 2048)
    nk = kdim // tk
    mod_spec = pl.BlockSpec((None, r if r == 1 else tm, d),
                            (lambda bi, i, k: (bi, i, 0)) if r == t else (lambda bi, i, k: (bi, 0, 0)))
    vec_spec = pl.BlockSpec((1, d), lambda bi, i, k: (0, 0))
    row_spec = pl.BlockSpec((None, tm, d), lambda bi, i, k: (bi, i, 0))
    return pl.pallas_call(
        functools.partial(_mm_res_kernel, nk=nk),
        grid=(b, t // tm, nk),
        in_specs=[pl.BlockSpec((None, tm, tk), lambda bi, i, k: (bi, i, k)),
                  pl.BlockSpec((None, tk, d), lambda bi, i, k: (layer, k, 0)),
                  row_spec, mod_spec, vec_spec, vec_spec, mod_spec, mod_spec],
        out_specs=[row_spec, row_spec],
        out_shape=[jax.ShapeDtypeStruct((b, t, d), F32), jax.ShapeDtypeStruct((b, t, d), BF16)],
        compiler_params=_cparams("parallel", "parallel", "arbitrary"),
        name="matmul_residual_norm",
    )(a3, w, x3, gt, g_post, g_pre, sc, sh)


def _ffn_kernel(u_ref, wg_ref, wu_ref, o_ref, wg16_ref, wu16_ref):
    @pl.when(pl.program_id(1) == 0)
    def _():
        wg16_ref[...] = wg_ref[...].astype(BF16)
        wu16_ref[...] = wu_ref[...].astype(BF16)

    u = u_ref[...]
    o_ref[...] = (_silu(_dot(u, wg16_ref[...])) * _dot(u, wu16_ref[...])).astype(BF16)


def _ffn_hidden(u, w_gate, w_up, layer):
    m, k = u.shape
    n = w_gate.shape[2]
    tm = _pick(m, 1024)
    tn = _pick(n, 512)
    w_spec = pl.BlockSpec((None, k, tn), lambda j, i: (layer, 0, j))
    return pl.pallas_call(
        _ffn_kernel,
        grid=(n // tn, m // tm),
        in_specs=[pl.BlockSpec((tm, k), lambda j, i: (i, 0)), w_spec, w_spec],
        out_specs=pl.BlockSpec((tm, tn), lambda j, i: (i, j)),
        out_shape=jax.ShapeDtypeStruct((m, n), BF16),
        scratch_shapes=[pltpu.VMEM((k, tn), BF16), pltpu.VMEM((k, tn), BF16)],
        compiler_params=_cparams("parallel", "arbitrary"),
        name="ffn_hidden",
    )(u, w_gate, w_up)


def kernel(x_prompt, x_sample, cache_k, cache_v, state_ssm, state_conv, state_pool, page_table, c_prompt, c_sample, w_ada, b_ada, g_pre_mix, g_post_mix, g_pre_ffn, g_post_ffn, w_in, w_pool_grp, pool_scale, conv_w, conv_b, dt_bias, a_log, d_skip, ssd_norm, w_pool_br, w_ssd_br, w_attn_br, w_out, w_gate, w_up, w_down):
    bp, tp, d = x_prompt.shape
    bs, ts, _ = x_sample.shape
    depth = w_ada.shape[0]
    hd = d // ATTN_HEADS
    assert hd == LANES and ATTN_KV_HEADS * ATTN_GROUP * ts == LANES
    inner = ssd_norm.shape[1]
    n_heads = inner // SSD_HEADDIM
    conv_dim = conv_w.shape[2]
    attn_w = ATTN_HEADS * hd
    kv_w = ATTN_KV_HEADS * hd
    page = cache_k.shape[2]
    n_pages = page_table.shape[1]
    past = n_pages * page
    assert tp % MOBA_BLOCK == 0 and past % MOBA_BLOCK == 0 and n_heads <= LANES

    w_t = jnp.transpose(w_in, (0, 2, 1))
    col_z = d
    col_xbc = col_z + inner
    col_q = col_xbc + conv_dim
    col_k = col_q + attn_w
    col_v = col_k + kv_w
    col_gate = col_v + kv_w
    n_main = w_in.shape[2] - n_heads

    def project(u2, layer):
        proj = _matmul_wt(u2, w_t, layer, 0, n_main, col_q, n_heads)
        dt = _matmul_wt(u2, w_t, layer, col_q, LANES, LANES, 0)
        return proj, dt

    w_grp16 = w_pool_grp.astype(BF16)
    w_pool16, w_ssd16, w_attn16 = w_pool_br.astype(BF16), w_ssd_br.astype(BF16), w_attn_br.astype(BF16)
    w_out16, w_down16 = w_out.astype(BF16), w_down.astype(BF16)

    ssd_par = _ssd_params(conv_w, conv_b, dt_bias, a_log, d_skip, ssd_norm)

    n_c = bp + bs
    c_all = jnp.pad(jnp.concatenate([c_prompt, c_sample], axis=0), ((0, -n_c % 8), (0, 0)))
    mod = _ada(c_all, w_ada, b_ada)

    def mods(layer, which):
        m = mod[layer, :, which * d:(which + 1) * d]
        m_s = jnp.broadcast_to(m[bp:n_c, None, :], (bs, ts, d)).reshape(1, bs * ts, d)
        return m[:bp, None, :], m_s

    hg_w = (n_heads // SSD_GROUPS) * SSD_HEADDIM
    ts_pad = SSD_CHUNK
    pps = _pick(n_pages, 16)
    eye_kv = jnp.eye(ATTN_KV_HEADS, dtype=F32)

    xp, xs_ = x_prompt, x_sample.reshape(1, bs * ts, d)
    sh1 = mods(0, 0)
    sc1 = mods(0, 1)
    up = _norm_mod(xp, g_pre_mix[0:1], sc1[0], sh1[0])
    us = _norm_mod(xs_, g_pre_mix[0:1], sc1[1], sh1[1])

    outs = {n: [] for n in ("kp", "vp", "hp", "cp", "pp", "ks", "vs", "hs", "cs", "ps")}
    for layer in range(depth):
        gt1, sh2, sc2, gt2 = mods(layer, 2), mods(layer, 3), mods(layer, 4), mods(layer, 5)
        nxt = (layer + 1) % depth
        sh1n, sc1n = mods(nxt, 0), mods(nxt, 1)

        proj2, dt_p = project(up.reshape(bp * tp, d), layer)
        proj = proj2.reshape(bp, tp, -1)
        k_p, v_p = proj[:, :, col_k:col_v], proj[:, :, col_v:col_gate]
        pool_o = _pool(proj, jnp.zeros((bp, POOL_HIST, d), F32), w_grp16, pool_scale[layer:layer + 1], layer, 0)
        ssd_o, h_p = _ssd(proj, dt_p.reshape(bp, tp, LANES), jnp.zeros((bp, CONV_HIST, conv_dim), F32),
                          jnp.zeros((bp, SSD_GROUPS, hg_w, SSD_STATE), F32), ssd_par, layer, col_z, col_xbc, tp)
        attn_o = _moba_prompt(proj, col_q, col_k, col_v)
        merged = _merge(pool_o.reshape(bp * tp, d), ssd_o.reshape(bp * tp, inner), attn_o.reshape(bp * tp, attn_w),
                        w_pool16, w_ssd16, w_attn16, proj2, col_gate, layer)
        xp, u_ffn = _mm_res(merged.reshape(bp, tp, d), w_out16, layer, xp, gt1[0], g_post_mix[layer:layer + 1],
                            g_pre_ffn[layer:layer + 1], sc2[0], sh2[0])
        hid = _ffn_hidden(u_ffn.reshape(bp * tp, d), w_gate, w_up, layer)
        xp, up = _mm_res(hid.reshape(bp, tp, -1), w_down16, layer, xp, gt2[0], g_post_ffn[layer:layer + 1],
                         g_pre_mix[nxt:nxt + 1], sc1n[0], sh1n[0])
        outs["kp"].append(k_p.reshape(bp, tp, ATTN_KV_HEADS, hd))
        outs["vp"].append(v_p.reshape(bp, tp, ATTN_KV_HEADS, hd))
        outs["hp"].append(h_p.reshape(bp, n_heads, SSD_HEADDIM, SSD_STATE))
        outs["cp"].append(proj[:, tp - (SSD_CONV - 1):, col_xbc:col_xbc + conv_dim])
        outs["pp"].append(proj[:, tp - (POOL_HIST - 1):, :d])

        proj2, dt_s = project(us.reshape(bs * ts, d), layer)
        proj = proj2.reshape(bs, ts, -1)
        k_s, v_s, dt_s = proj[:, :, col_k:col_v], proj[:, :, col_v:col_gate], dt_s.reshape(bs, ts, LANES)
        pool_buf = jnp.pad(state_pool[layer], ((0, 0), (1, 0), (0, 0)))
        pool_o = _pool(proj, pool_buf, w_grp16, pool_scale[layer:layer + 1], layer, past)
        tpad = ((0, 0), (0, ts_pad - ts), (0, 0))
        conv_prev = jnp.pad(state_conv[layer], ((0, 0), (CONV_HIST - (SSD_CONV - 1), 0), (0, 0)))
        ssd_o, h_s = _ssd(jnp.pad(proj[:, :, :col_q], tpad), jnp.pad(dt_s, tpad), conv_prev,
                          state_ssm[layer].reshape(bs, SSD_GROUPS, hg_w, SSD_STATE), ssd_par, layer,
                          col_z, col_xbc, ts)
        ssd_o = ssd_o[:, :ts]
        q5 = proj[:, :, col_q:col_k].reshape(bs, ts, ATTN_KV_HEADS, ATTN_GROUP, hd)
        wq_t = jnp.einsum("btkgd,kj->bjdkgt", q5, eye_kv).reshape(bs, kv_w, LANES).astype(BF16)
        cache_k4 = cache_k.reshape(depth, cache_k.shape[1], page * ATTN_KV_HEADS, hd)
        cache_v4 = cache_v.reshape(depth, cache_v.shape[1], page * ATTN_KV_HEADS, hd)
        scores = _moba_scores(cache_k4, layer, page_table, wq_t, pps)
        probs = _moba_softmax(scores, k_s, wq_t, ts)
        o_full = _moba_pv(cache_v4, layer, page_table, probs, jnp.pad(v_s, ((0, 0), (0, LANES - ts), (0, 0))), pps)
        o6 = o_full.reshape(bs, ATTN_KV_HEADS, ATTN_GROUP, ts, ATTN_KV_HEADS, hd)
        attn_o = jnp.einsum("bkgtjd,kj->btkgd", o6, eye_kv).reshape(bs * ts, attn_w).astype(BF16)
        merged = _merge(pool_o.reshape(bs * ts, d), ssd_o.reshape(bs * ts, inner), attn_o,
                        w_pool16, w_ssd16, w_attn16, proj2, col_gate, layer)
        xs_, u_ffn = _mm_res(merged.reshape(1, bs * ts, d), w_out16, layer, xs_, gt1[1], g_post_mix[layer:layer + 1],
                             g_pre_ffn[layer:layer + 1], sc2[1], sh2[1])
        hid = _ffn_hidden(u_ffn.reshape(bs * ts, d), w_gate, w_up, layer)
        xs_, us = _mm_res(hid.reshape(1, bs * ts, -1), w_down16, layer, xs_, gt2[1], g_post_ffn[layer:layer + 1],
                          g_pre_mix[nxt:nxt + 1], sc1n[1], sh1n[1])
        outs["ks"].append(k_s.reshape(bs, ts, ATTN_KV_HEADS, hd))
        outs["vs"].append(v_s.reshape(bs, ts, ATTN_KV_HEADS, hd))
        outs["hs"].append(h_s.reshape(bs, n_heads, SSD_HEADDIM, SSD_STATE))
        conv_ext = jnp.concatenate([state_conv[layer], proj[:, :, col_xbc:col_xbc + conv_dim]], axis=1)
        outs["cs"].append(conv_ext[:, ts:])
        pool_ext = jnp.concatenate([state_pool[layer], proj[:, :, :d]], axis=1)
        outs["ps"].append(pool_ext[:, ts:])

    st = {n: jnp.stack(v) for n, v in outs.items()}
    return (xp, xs_.reshape(bs, ts, d), st["kp"], st["vp"], st["hp"], st["cp"], st["pp"],
            st["ks"], st["vs"], st["hs"], st["cs"], st["ps"])
```

```python
import functools

import jax
import jax.numpy as jnp
from jax import lax
from jax.experimental import pallas as pl
from jax.experimental.pallas import tpu as pltpu

F32 = jnp.float32
BF16 = jnp.bfloat16

NORM_EPS = 1e-6
POOL_WINDOWS = (2, 4, 8, 16)
POOL_HIST = 16
SSD_HEADDIM = 64
SSD_GROUPS = 8
SSD_STATE = 128
SSD_CONV = 4
SSD_CHUNK = 128
SSD_GROUPS_PER_STEP = 4
CONV_HIST = 8
ATTN_HEADS = 16
ATTN_KV_HEADS = 8
ATTN_GROUP = ATTN_HEADS // ATTN_KV_HEADS
MOBA_BLOCK = 256
MOBA_TOPK = 3
PAST_GROUP = 2
LANES = 128
VMEM_LIMIT_BYTES = 56 * 1024 * 1024
NEG_INF = float("-inf")
LOG2E = 1.4426950408889634


def _cparams(*sem):
    return pltpu.CompilerParams(dimension_semantics=sem, vmem_limit_bytes=VMEM_LIMIT_BYTES)


def _sigmoid(x):
    return 1.0 / (1.0 + jnp.exp(-x))


def _silu(x):
    return x * _sigmoid(x)


def _dot(a, b):
    return jnp.dot(a, b, preferred_element_type=F32)


def _dot_nt(a, b):
    return lax.dot_general(a, b, (((1,), (1,)), ((), ())), preferred_element_type=F32)


def _dot_tn(a, b):
    return lax.dot_general(a, b, (((0,), (0,)), ((), ())), preferred_element_type=F32)


def _rms(x):
    return x * lax.rsqrt(jnp.mean(x * x, axis=-1, keepdims=True) + NORM_EPS)


def _pick(total, pref):
    if total <= pref:
        return total
    t = pref
    while total % t:
        t //= 2
    return t


def _ada_kernel(c_ref, w_ref, b_ref, o_ref):
    a = _silu(c_ref[...]).astype(BF16)
    o_ref[...] = _dot(a, w_ref[...].astype(BF16)) + b_ref[...]


def _ada(c_all, w_ada, b_ada):
    depth, d, n = w_ada.shape
    rows = c_all.shape[0]
    tn = _pick(n, 1024)
    return pl.pallas_call(
        _ada_kernel,
        grid=(depth, n // tn),
        in_specs=[pl.BlockSpec((rows, d), lambda l, j: (0, 0)),
                  pl.BlockSpec((None, d, tn), lambda l, j: (l, 0, j)),
                  pl.BlockSpec((None, 1, tn), lambda l, j: (l, 0, j))],
        out_specs=pl.BlockSpec((None, rows, tn), lambda l, j: (l, 0, j)),
        out_shape=jax.ShapeDtypeStruct((depth, rows, n), F32),
        compiler_params=_cparams("parallel", "parallel"),
        name="ada_mod",
    )(c_all, w_ada, b_ada.reshape(depth, 1, n))


def _norm_mod_kernel(x_ref, g_ref, sc_ref, sh_ref, u_ref):
    y = _rms(x_ref[...]) * g_ref[...]
    u_ref[...] = (y * (1.0 + sc_ref[...]) + sh_ref[...]).astype(BF16)


def _norm_mod(x, g, sc, sh):
    b, t, d = x.shape
    r = sc.shape[1]
    tt = _pick(t, 512)
    mod_spec = pl.BlockSpec((None, r if r == 1 else tt, d),
                            (lambda i, j: (i, j, 0)) if r == t else (lambda i, j: (i, 0, 0)))
    return pl.pallas_call(
        _norm_mod_kernel,
        grid=(b, t // tt),
        in_specs=[pl.BlockSpec((None, tt, d), lambda i, j: (i, j, 0)),
                  pl.BlockSpec((1, d), lambda i, j: (0, 0)),
                  mod_spec, mod_spec],
        out_specs=pl.BlockSpec((None, tt, d), lambda i, j: (i, j, 0)),
        out_shape=jax.ShapeDtypeStruct((b, t, d), BF16),
        compiler_params=_cparams("parallel", "parallel"),
        name="norm_mod",
    )(x, g, sc, sh)


def _mm_kernel(a_ref, w_ref, o_ref, w16_ref):
    @pl.when(pl.program_id(1) == 0)
    def _():
        w16_ref[...] = w_ref[0].astype(BF16)

    o_ref[...] = _dot_nt(a_ref[...], w16_ref[...])


def _matmul_wt(a, w_t, layer, row0, n, skip_at, skip):
    m, k = a.shape
    tm = _pick(m, 1024)
    tn = _pick(n, 1024)
    assert skip_at % tn == 0
    hole = skip_at // tn

    def w_index(j, i):
        return layer, pl.multiple_of(row0 + j * tn + jnp.where(j >= hole, skip, 0), 8), 0

    return pl.pallas_call(
        _mm_kernel,
        grid=(n // tn, m // tm),
        in_specs=[pl.BlockSpec((tm, k), lambda j, i: (i, 0)),
                  pl.BlockSpec((pl.Element(1), pl.Element(tn), pl.Element(k)), w_index)],
        out_specs=pl.BlockSpec((tm, tn), lambda j, i: (i, j)),
        out_shape=jax.ShapeDtypeStruct((m, n), F32),
        scratch_shapes=[pltpu.VMEM((tn, k), BF16)],
        compiler_params=_cparams("parallel", "arbitrary"),
        name="proj_matmul",
    )(a, w_t)


def _pool_kernel(p_ref, buf_ref, w_ref, sc_ref, o_ref, ext_ref, *, tt, start, pg):
    t = pl.program_id(1)

    @pl.when(t == 0)
    def _():
        ext_ref[0:POOL_HIST, :] = buf_ref[...]

    ext_ref[POOL_HIST:POOL_HIST + tt, :] = p_ref[...]
    n_valid = lax.broadcasted_iota(jnp.int32, (tt, 1), 0) + (t * tt + start + 1)
    for gi, w in enumerate(POOL_WINDOWS):
        c0 = gi * pg
        x = ext_ref[POOL_HIST:POOL_HIST + tt, c0:c0 + pg]
        s = x
        for k in range(1, w):
            s = s + ext_ref[POOL_HIST - k:POOL_HIST - k + tt, c0:c0 + pg]
        cnt = jnp.minimum(n_valid, w).astype(F32)
        mixed = (s / cnt - x).astype(BF16)
        y = _dot(mixed, w_ref[gi]) * sc_ref[:, c0:c0 + pg]
        o_ref[:, c0:c0 + pg] = y.astype(BF16)
    ext_ref[0:POOL_HIST, :] = ext_ref[tt:tt + POOL_HIST, :]


def _pool(proj3, buf, w_grp, scale, layer, start):
    b, t, _ = proj3.shape
    c = scale.shape[1]
    pg = c // len(POOL_WINDOWS)
    tt = _pick(t, 256)
    return pl.pallas_call(
        functools.partial(_pool_kernel, tt=tt, start=start, pg=pg),
        grid=(b, t // tt),
        in_specs=[pl.BlockSpec((None, tt, c), lambda i, j: (i, j, 0)),
                  pl.BlockSpec((None, POOL_HIST, c), lambda i, j: (i, 0, 0)),
                  pl.BlockSpec((None, len(POOL_WINDOWS), pg, pg), lambda i, j: (layer, 0, 0, 0)),
                  pl.BlockSpec((1, c), lambda i, j: (0, 0))],
        out_specs=pl.BlockSpec((None, tt, c), lambda i, j: (i, j, 0)),
        out_shape=jax.ShapeDtypeStruct((b, t, c), BF16),
        scratch_shapes=[pltpu.VMEM((POOL_HIST + tt, c), F32)],
        compiler_params=_cparams("parallel", "arbitrary"),
        name="pool_mixer",
    )(proj3, buf, w_grp, scale)


def _ssd_kernel(z_ref, x_ref, b_ref, c_ref, dt_ref, cpx_ref, cpb_ref, cpc_ref,
                px_ref, pb_ref, pc_ref, pdt_ref, exp_ref, h0_ref,
                y_ref, hout_ref, histx, histb, histc, ht, *, t_valid, n_chunks, hg, gps):
    gq = pl.program_id(1)
    c = pl.program_id(2)
    lc = SSD_CHUNK
    gw = hg * SSD_HEADDIM
    ns = SSD_STATE

    @pl.when(c == 0)
    def _():
        histx[...] = cpx_ref[...]
        histb[...] = cpb_ref[...]
        histc[...] = cpc_ref[...]
        for s in range(gps):
            ht[s] = h0_ref[s].T

    def conv(hist, raw_ref, par_ref):
        x = raw_ref[...]
        prev = hist[...]
        hist[...] = x[lc - CONV_HIST:, :]
        row8 = lax.broadcasted_iota(jnp.int32, prev.shape, 0)
        acc = par_ref[SSD_CONV:SSD_CONV + 1, :] + x * par_ref[SSD_CONV - 1:SSD_CONV, :]
        for k in range(1, SSD_CONV):
            r = pltpu.roll(x, k, axis=0)
            first = jnp.where(row8 < k, pltpu.roll(prev, k, axis=0), r[0:CONV_HIST])
            shifted = jnp.concatenate([first, r[CONV_HIST:]], axis=0)
            acc = acc + shifted * par_ref[SSD_CONV - 1 - k:SSD_CONV - k, :]
        return _silu(acc)

    xs_all = conv(histx, x_ref, px_ref)
    bm_all = conv(histb, b_ref, pb_ref)
    cm_all = conv(histc, c_ref, pc_ref)

    row = lax.broadcasted_iota(jnp.int32, (lc, LANES), 0)
    col = lax.broadcasted_iota(jnp.int32, (lc, LANES), 1)
    xdt = dt_ref[...] + pdt_ref[0:1, :]
    dt = jnp.maximum(xdt, 0.0) + jnp.log1p(jnp.exp(-jnp.abs(xdt)))
    dt = jnp.where(c * lc + row < t_valid, dt, 0.0)
    dta = dt * (-jnp.exp(pdt_ref[1:2, :]))
    acs = dta
    k = 1
    while k < lc:
        acs = acs + jnp.where(row >= k, pltpu.roll(acs, k, axis=0), 0.0)
        k *= 2
    a_last = acs[lc - 1:lc, :]
    w_end = dt * jnp.exp(a_last - acs)
    exp_a = jnp.exp(acs)

    cd = jnp.exp(a_last)
    cd_hi = cd.astype(BF16).astype(F32)
    cd_mid = (cd - cd_hi).astype(BF16).astype(F32)
    cd_lo = cd - cd_hi - cd_mid
    pieces = [jnp.broadcast_to(p, (16, LANES)) for p in (cd_hi, cd_mid, cd_lo)]
    stacked = jnp.concatenate([dt, w_end, exp_a] + pieces, axis=0).astype(BF16)
    acs2 = acs * LOG2E
    causal = row >= col
    heads_here = LANES // SSD_HEADDIM

    for s in range(gps):
        xl, nl = slice(s * gw, (s + 1) * gw), slice(s * ns, (s + 1) * ns)
        xs, bm, cm = xs_all[:, xl], bm_all[:, nl], cm_all[:, nl]
        wide = _dot(stacked, exp_ref[s])
        dt_e, w_e, ea_e = wide[0:lc], wide[lc:2 * lc], wide[2 * lc:3 * lc]
        r0 = 3 * lc
        cd_e = wide[r0:r0 + 1] + wide[r0 + 16:r0 + 17] + wide[r0 + 32:r0 + 33]

        acs_g = pltpu.roll(acs2, (LANES - (gq * gps + s) * hg) % LANES, axis=1)
        acs_gt = acs_g.T

        bm16 = bm.astype(BF16)
        cm16 = cm.astype(BF16)
        cb = jnp.where(causal, _dot_nt(cm16, bm16), 0.0)
        xdt_e = xs * dt_e
        y_parts = []
        for pr in range(gw // LANES):
            xp = xdt_e[:, pr * LANES:(pr + 1) * LANES]
            lhs, rhs = [], []
            for hh in range(heads_here):
                h = pr * heads_here + hh
                seg = acs_g[:, h:h + 1] - acs_gt[h:h + 1, :]
                lhs.append((cb * jnp.exp2(jnp.minimum(seg, 0.0))).astype(BF16))
                in_head = (col >= hh * SSD_HEADDIM) & (col < (hh + 1) * SSD_HEADDIM)
                rhs.append(jnp.where(in_head, xp, 0.0).astype(BF16))
            y_parts.append(_dot(jnp.concatenate(lhs, axis=1), jnp.concatenate(rhs, axis=0)))
        y = jnp.concatenate(y_parts, axis=1)

        h_in = ht[s]
        y = y + _dot(cm16, h_in.astype(BF16)) * ea_e
        h_new = h_in * cd_e + _dot(bm.T.astype(BF16), (xs * w_e).astype(BF16))
        ht[s] = h_new

        y = y + px_ref[SSD_CONV + 1:SSD_CONV + 2, xl] * xs
        y = y * _silu(z_ref[:, xl])
        y_ref[:, xl] = (_rms(y) * px_ref[SSD_CONV + 2:SSD_CONV + 3, xl]).astype(BF16)

    @pl.when(c == n_chunks - 1)
    def _():
        for s in range(gps):
            hout_ref[s] = ht[s].T


def _ssd_params(conv_w, conv_b, dt_bias, a_log, d_skip, ssd_norm):
    depth, inner = ssd_norm.shape
    n_heads = dt_bias.shape[1]
    rows = lambda v: v[:, None, :]
    px = jnp.concatenate([conv_w[:, :, :inner], rows(conv_b[:, :inner]), rows(jnp.repeat(d_skip, SSD_HEADDIM, axis=1)),
                          rows(ssd_norm), jnp.zeros((depth, 8 - SSD_CONV - 3, inner), F32)], axis=1)
    pbc = jnp.concatenate([conv_w[:, :, inner:], rows(conv_b[:, inner:]),
                           jnp.zeros((depth, 8 - SSD_CONV - 1, conv_w.shape[2] - inner), F32)], axis=1)
    pdt = jnp.pad(jnp.stack([dt_bias, a_log], axis=1), ((0, 0), (0, 6), (0, LANES - n_heads)))
    return px, pbc, pdt


def _ssd(proj3, dt3, conv_prev, h0, params, layer, col_z, col_xbc, t_valid):
    b, tp, _ = proj3.shape
    px, pbc, pdt = params
    inner = px.shape[2]
    n_heads = inner // SSD_HEADDIM
    hg = n_heads // SSD_GROUPS
    gw = hg * SSD_HEADDIM
    ns = SSD_STATE
    lc = SSD_CHUNK
    n_chunks = tp // lc
    gps = SSD_GROUPS_PER_STEP
    xw, nw = gps * gw, gps * ns
    assert col_z % xw == 0 and col_xbc % xw == 0 and (col_xbc + inner) % nw == 0 and SSD_GROUPS % gps == 0
    zb, xb = col_z // xw, col_xbc // xw
    bb = (col_xbc + inner) // nw
    cb_ = (col_xbc + inner + SSD_GROUPS * ns) // nw
    cxb, cbb, ccb = 0, inner // nw, (inner + SSD_GROUPS * ns) // nw
    head_of = (jnp.arange(SSD_GROUPS)[:, None] * hg + jnp.arange(gw)[None, :] // SSD_HEADDIM)
    expand = (jnp.arange(LANES)[None, :, None] == head_of[:, None, :]).astype(BF16)

    def seq(width, off):
        return pl.BlockSpec((None, lc, width), lambda i, g, c: (i, c, off + g))

    def prev(width, off):
        return pl.BlockSpec((None, CONV_HIST, width), lambda i, g, c: (i, 0, off + g))

    def par(width, off):
        return pl.BlockSpec((None, 8, width), lambda i, g, c: (layer, 0, off + g))

    state_spec = pl.BlockSpec((None, gps, gw, ns), lambda i, g, c: (i, g, 0, 0))
    kernel = functools.partial(_ssd_kernel, t_valid=t_valid, n_chunks=n_chunks, hg=hg, gps=gps)
    return pl.pallas_call(
        kernel,
        grid=(b, SSD_GROUPS // gps, n_chunks),
        in_specs=[seq(xw, zb), seq(xw, xb), seq(nw, bb), seq(nw, cb_),
                  pl.BlockSpec((None, lc, LANES), lambda i, g, c: (i, c, 0)),
                  prev(xw, cxb), prev(nw, cbb), prev(nw, ccb),
                  par(xw, 0), par(nw, 0), par(nw, SSD_GROUPS // gps),
                  pl.BlockSpec((None, 8, LANES), lambda i, g, c: (layer, 0, 0)),
                  pl.BlockSpec((gps, LANES, gw), lambda i, g, c: (g, 0, 0)),
                  state_spec],
        out_specs=[pl.BlockSpec((None, lc, xw), lambda i, g, c: (i, c, g)), state_spec],
        out_shape=[jax.ShapeDtypeStruct((b, tp, inner), BF16),
                   jax.ShapeDtypeStruct((b, SSD_GROUPS, gw, ns), F32)],
        scratch_shapes=[pltpu.VMEM((CONV_HIST, xw), F32),
                        pltpu.VMEM((CONV_HIST, nw), F32),
                        pltpu.VMEM((CONV_HIST, nw), F32),
                        pltpu.VMEM((gps, ns, gw), F32)],
        compiler_params=_cparams("parallel", "parallel", "arbitrary"),
        name="ssd_mixer",
    )(proj3, proj3, proj3, proj3, dt3, conv_prev, conv_prev, conv_prev, px, pbc, pbc, pdt, expand, h0)


def _top_k_mask(gate, valid, ids, axis):
    big = jnp.int32(2 ** 30)
    remaining = valid
    sel = jnp.zeros(gate.shape, F32)
    for _ in range(MOBA_TOPK):
        gm = jnp.where(remaining > 0.5, gate, NEG_INF)
        m = jnp.max(gm, axis=axis, keepdims=True)
        cand = jnp.where(gm == m, remaining, 0.0)
        first = jnp.min(jnp.where(cand > 0.5, ids, big), axis=axis, keepdims=True)
        pick = jnp.where(ids == first, 1.0, 0.0)
        sel = sel + pick
        remaining = remaining - pick
    return sel


def _moba_prompt_kernel(q_ref, k_ref, v_ref, o_ref, kb, vtb, km, sel_ref, qs_ref, acc_ref, ml_ref, sa_ref, sb_ref,
                        *, n_blocks, scale):
    i = pl.program_id(2)
    blk = MOBA_BLOCK
    hd = LANES
    nq = ATTN_GROUP * blk
    c2 = scale * LOG2E

    @pl.when(i == 0)
    def _():
        km[...] = jnp.zeros(km.shape, F32)
        for j in range(n_blocks):
            kj = k_ref[j * blk:(j + 1) * blk, :]
            kb[j] = kj.astype(BF16)
            km[j:j + 1, :] = jnp.mean(kj, axis=0, keepdims=True)
            vtb[j] = v_ref[j * blk:(j + 1) * blk, :].T.astype(BF16)

    q = q_ref[...]
    qs_ref[...] = jnp.concatenate([q[:, g * hd:(g + 1) * hd] for g in range(ATTN_GROUP)], axis=0).astype(BF16)
    gate_t = _dot_nt(km[...].astype(BF16), qs_ref[...])
    ids = lax.broadcasted_iota(jnp.int32, (km.shape[0], nq), 0)
    valid = jnp.where(ids < i, 1.0, 0.0)
    sel_ref[...] = _top_k_mask(gate_t, valid, ids, axis=0)

    key_r = lax.broadcasted_iota(jnp.int32, (blk, nq), 0)
    qry_c = lax.broadcasted_iota(jnp.int32, (blk, nq), 1) & (blk - 1)
    s = _dot_nt(kb[i], qs_ref[...])
    s = jnp.where(key_r <= qry_c, s, NEG_INF)
    m = jnp.max(s, axis=0, keepdims=True)
    p = jnp.exp2((s - m) * c2)
    ml_ref[0:1, :] = m
    ml_ref[1:2, :] = jnp.sum(p, axis=0, keepdims=True)
    acc_ref[...] = _dot(vtb[i], p.astype(BF16))

    def scores(s_ref, j0, nb):
        for k in range(nb):
            s_ref[k] = _dot_nt(kb[j0 + k], qs_ref[...])

    def update(s_ref, j0, nb):
        m_old, l_old = ml_ref[0:1, :], ml_ref[1:2, :]
        picked = [sel_ref[pl.ds(j0 + k, 1), :] > 0.5 for k in range(nb)]
        m_new = m_old
        for k in range(nb):
            m_new = jnp.maximum(m_new, jnp.where(picked[k], jnp.max(s_ref[k], axis=0, keepdims=True), NEG_INF))
        alpha = jnp.exp2((m_old - m_new) * c2)
        l_new = alpha * l_old
        pv = None
        for k in range(nb):
            p = jnp.exp2((s_ref[k] - jnp.where(picked[k], m_new, jnp.inf)) * c2)
            l_new = l_new + jnp.sum(p, axis=0, keepdims=True)
            d = _dot(vtb[j0 + k], p.astype(BF16))
            pv = d if pv is None else pv + d
        acc_ref[...] = alpha * acc_ref[...] + pv
        ml_ref[0:1, :] = m_new
        ml_ref[1:2, :] = l_new

    grp = PAST_GROUP
    n_groups = i // grp
    last_start = max(n_blocks - grp, 0)

    @pl.when(n_groups > 0)
    def _():
        scores(sa_ref, 0, grp)

    def pair_body(u, carry):
        j0 = 2 * u * grp
        scores(sb_ref, jnp.minimum(j0 + grp, last_start), grp)
        update(sa_ref, j0, grp)
        scores(sa_ref, jnp.minimum(j0 + 2 * grp, last_start), grp)
        update(sb_ref, j0 + grp, grp)
        return carry

    lax.fori_loop(0, n_groups // 2, pair_body, 0)

    @pl.when((n_groups & 1) != 0)
    def _():
        update(sa_ref, (n_groups - 1) * grp, grp)

    done = n_groups * grp
    nb = grp // 2
    while nb >= 1:
        @pl.when((i & nb) != 0)
        def _(nb=nb, done=done):
            scores(sa_ref, done, nb)
            update(sa_ref, done, nb)

        done = done + (i & nb)
        nb //= 2

    o = (acc_ref[...] / ml_ref[1:2, :]).T
    o_ref[...] = jnp.concatenate([o[g * blk:(g + 1) * blk] for g in range(ATTN_GROUP)], axis=1).astype(BF16)


def _moba_prompt(proj3, col_q, col_k, col_v):
    b, t, _ = proj3.shape
    hd = LANES
    blk = MOBA_BLOCK
    n_blocks = t // blk
    nb_pad = -(-n_blocks // 16) * 16
    qw = ATTN_GROUP * hd
    qb, kb0, vb0 = col_q // qw, col_k // hd, col_v // hd
    kernel = functools.partial(_moba_prompt_kernel, n_blocks=n_blocks, scale=hd ** -0.5)
    return pl.pallas_call(
        kernel,
        grid=(b, ATTN_KV_HEADS, n_blocks),
        in_specs=[pl.BlockSpec((None, blk, qw), lambda bi, h, i: (bi, i, qb + h)),
                  pl.BlockSpec((None, t, hd), lambda bi, h, i: (bi, 0, kb0 + h)),
                  pl.BlockSpec((None, t, hd), lambda bi, h, i: (bi, 0, vb0 + h))],
        out_specs=pl.BlockSpec((None, blk, qw), lambda bi, h, i: (bi, i, h)),
        out_shape=jax.ShapeDtypeStruct((b, t, ATTN_HEADS * hd), BF16),
        scratch_shapes=[pltpu.VMEM((n_blocks, blk, hd), BF16), pltpu.VMEM((n_blocks, hd, blk), BF16),
                        pltpu.VMEM((nb_pad, hd), F32), pltpu.VMEM((nb_pad, ATTN_GROUP * blk), F32),
                        pltpu.VMEM((ATTN_GROUP * blk, hd), BF16), pltpu.VMEM((hd, ATTN_GROUP * blk), F32),
                        pltpu.VMEM((8, ATTN_GROUP * blk), F32),
                        pltpu.VMEM((PAST_GROUP, blk, ATTN_GROUP * blk), F32),
                        pltpu.VMEM((PAST_GROUP, blk, ATTN_GROUP * blk), F32)],
        compiler_params=_cparams("parallel", "parallel", "arbitrary"),
        name="moba_prompt",
    )(proj3, proj3, proj3)


def _moba_scores_kernel(pt_ref, wq_ref, *refs, pps, page):
    del pt_ref
    k_refs, o_ref = refs[:pps], refs[pps]
    hd = LANES
    for r in range(pps):
        acc = None
        for h in range(ATTN_KV_HEADS):
            kh = k_refs[r][pl.ds(h, page, stride=ATTN_KV_HEADS), :].astype(BF16)
            part = _dot(kh, wq_ref[h * hd:(h + 1) * hd, :])
            acc = part if acc is None else acc + part
        o_ref[r * page:(r + 1) * page, :] = acc


def _moba_scores(cache4, layer, page_table, wq_t, pps):
    b, n_pages = page_table.shape
    rows, hd = cache4.shape[2], cache4.shape[3]
    page, width = rows // ATTN_KV_HEADS, hd * ATTN_KV_HEADS

    def page_spec(r):
        return pl.BlockSpec((None, None, rows, hd), lambda bi, p, pt: (layer, pt[bi, p * pps + r], 0, 0))

    grid_spec = pltpu.PrefetchScalarGridSpec(
        num_scalar_prefetch=1,
        grid=(b, n_pages // pps),
        in_specs=[pl.BlockSpec((None, width, LANES), lambda bi, p, pt: (bi, 0, 0))]
                 + [page_spec(r) for r in range(pps)],
        out_specs=pl.BlockSpec((None, pps * page, LANES), lambda bi, p, pt: (bi, p, 0)),
    )
    return pl.pallas_call(
        functools.partial(_moba_scores_kernel, pps=pps, page=page),
        grid_spec=grid_spec,
        out_shape=jax.ShapeDtypeStruct((b, n_pages * page, LANES), F32),
        compiler_params=_cparams("parallel", "arbitrary"),
        name="moba_sample_scores",
    )(page_table, wq_t, *([cache4] * pps))


def _moba_softmax_kernel(st_ref, kn_ref, wq_ref, p_ref, gate_ref, sel_ref, *, n_blocks, tq, scale):
    blk = MOBA_BLOCK
    past = n_blocks * blk
    gate_ref[...] = jnp.zeros(gate_ref.shape, F32)

    def gate_body(j, carry):
        off = pl.multiple_of(j * blk, blk)
        gate_ref[pl.ds(j, 1), :] = jnp.mean(st_ref[pl.ds(off, blk), :], axis=0, keepdims=True)
        return carry

    lax.fori_loop(0, n_blocks, gate_body, 0)
    nrow = gate_ref.shape[0]
    ids = lax.broadcasted_iota(jnp.int32, (nrow, LANES), 0)
    valid = jnp.where(ids < n_blocks, 1.0, 0.0)
    sel_ref[...] = _top_k_mask(gate_ref[...], valid, ids, axis=0)

    s_own = _dot(kn_ref[...].astype(BF16), wq_ref[...]) * scale
    key_t = lax.broadcasted_iota(jnp.int32, (tq, LANES), 0)
    qry_t = lax.broadcasted_iota(jnp.int32, (tq, LANES), 1) & (tq - 1)
    s_own = jnp.where(key_t <= qry_t, s_own, NEG_INF)
    m0 = jnp.max(s_own, axis=0, keepdims=True)

    def max_body(j, m):
        off = pl.multiple_of(j * blk, blk)
        bm = jnp.max(st_ref[pl.ds(off, blk), :], axis=0, keepdims=True) * scale
        return jnp.maximum(m, jnp.where(sel_ref[pl.ds(j, 1), :] > 0.5, bm, NEG_INF))

    m = lax.fori_loop(0, n_blocks, max_body, m0)
    p_own = jnp.exp(s_own - m)
    l0 = jnp.sum(p_own, axis=0, keepdims=True)

    def exp_body(j, l):
        off = pl.multiple_of(j * blk, blk)
        s = st_ref[pl.ds(off, blk), :] * scale
        p = jnp.where(sel_ref[pl.ds(j, 1), :] > 0.5, jnp.exp(s - m), 0.0)
        p_ref[pl.ds(off, blk), :] = p
        return l + jnp.sum(p, axis=0, keepdims=True)

    l = lax.fori_loop(0, n_blocks, exp_body, l0)
    inv = 1.0 / l

    def norm_body(j, carry):
        off = pl.multiple_of(j * blk, blk)
        p_ref[pl.ds(off, blk), :] = p_ref[pl.ds(off, blk), :] * inv
        return carry

    lax.fori_loop(0, n_blocks, norm_body, 0)
    p_ref[past:past + LANES, :] = jnp.zeros((LANES, LANES), F32)
    p_ref[past:past + tq, :] = p_own * inv


def _moba_softmax(scores, k_new, wq_t, tq):
    b, past, _ = scores.shape
    width = k_new.shape[2]
    n_blocks = past // MOBA_BLOCK
    nrow = -(-n_blocks // 8) * 8
    kernel = functools.partial(_moba_softmax_kernel, n_blocks=n_blocks, tq=tq, scale=LANES ** -0.5)
    return pl.pallas_call(
        kernel,
        grid=(b,),
        in_specs=[pl.BlockSpec((None, past, LANES), lambda bi: (bi, 0, 0)),
                  pl.BlockSpec((None, tq, width), lambda bi: (bi, 0, 0)),
                  pl.BlockSpec((None, width, LANES), lambda bi: (bi, 0, 0))],
        out_specs=pl.BlockSpec((None, past + LANES, LANES), lambda bi: (bi, 0, 0)),
        out_shape=jax.ShapeDtypeStruct((b, past + LANES, LANES), F32),
        scratch_shapes=[pltpu.VMEM((nrow, LANES), F32), pltpu.VMEM((nrow, LANES), F32)],
        compiler_params=_cparams("parallel"),
        name="moba_sample_softmax",
    )(scores, k_new, wq_t)


def _moba_pv_kernel(pt_ref, p_ref, pown_ref, vnew_ref, *refs, pps, page):
    del pt_ref
    v_refs, o_ref = refs[:pps], refs[pps]

    @pl.when(pl.program_id(1) == 0)
    def _():
        o_ref[...] = _dot_tn(pown_ref[...].astype(BF16), vnew_ref[...].astype(BF16))

    hd = LANES
    for h in range(ATTN_KV_HEADS):
        acc = o_ref[:, h * hd:(h + 1) * hd]
        for r in range(pps):
            vh = v_refs[r][pl.ds(h, page, stride=ATTN_KV_HEADS), :].astype(BF16)
            acc = acc + _dot_tn(p_ref[r * page:(r + 1) * page, :].astype(BF16), vh)
        o_ref[:, h * hd:(h + 1) * hd] = acc


def _moba_pv(cache4, layer, page_table, probs, v_new_pad, pps):
    b, n_pages = page_table.shape
    rows, hd = cache4.shape[2], cache4.shape[3]
    page, width = rows // ATTN_KV_HEADS, hd * ATTN_KV_HEADS
    own_blk = (n_pages * page) // LANES

    def page_spec(r):
        return pl.BlockSpec((None, None, rows, hd), lambda bi, p, pt: (layer, pt[bi, p * pps + r], 0, 0))

    grid_spec = pltpu.PrefetchScalarGridSpec(
        num_scalar_prefetch=1,
        grid=(b, n_pages // pps),
        in_specs=[pl.BlockSpec((None, pps * page, LANES), lambda bi, p, pt: (bi, p, 0)),
                  pl.BlockSpec((None, LANES, LANES), lambda bi, p, pt: (bi, own_blk, 0)),
                  pl.BlockSpec((None, LANES, width), lambda bi, p, pt: (bi, 0, 0))]
                 + [page_spec(r) for r in range(pps)],
        out_specs=pl.BlockSpec((None, LANES, width), lambda bi, p, pt: (bi, 0, 0)),
    )
    return pl.pallas_call(
        functools.partial(_moba_pv_kernel, pps=pps, page=page),
        grid_spec=grid_spec,
        out_shape=jax.ShapeDtypeStruct((b, LANES, width), F32),
        compiler_params=_cparams("parallel", "arbitrary"),
        name="moba_sample_pv",
    )(page_table, probs, probs, v_new_pad, *([cache4] * pps))


def _merge_kernel(ap_ref, as_ref, aa_ref, wp_ref, ws_ref, wa_ref, g0_ref, g1_ref, g2_ref, o_ref):
    acc = _sigmoid(g0_ref[...]) * _dot(ap_ref[...], wp_ref[...])
    acc = acc + _sigmoid(g1_ref[...]) * _dot(as_ref[...], ws_ref[...])
    acc = acc + _sigmoid(g2_ref[...]) * _dot(aa_ref[...], wa_ref[...])
    o_ref[...] = acc.astype(BF16)


def _merge(pool_o, ssd_o, attn_o, w_pool, w_ssd, w_attn, proj, col_gate, layer):
    m, d = pool_o.shape
    tm = _pick(m, 512)
    tn = _pick(d, 512)
    gb = col_gate // tn
    nb = d // tn

    def act(width):
        return pl.BlockSpec((tm, width), lambda j, i: (i, 0))

    def wgt(kdim):
        return pl.BlockSpec((None, kdim, tn), lambda j, i: (layer, 0, j))

    def gate(which):
        return pl.BlockSpec((tm, tn), lambda j, i: (i, gb + which * nb + j))

    return pl.pallas_call(
        _merge_kernel,
        grid=(d // tn, m // tm),
        in_specs=[act(pool_o.shape[1]), act(ssd_o.shape[1]), act(attn_o.shape[1]),
                  wgt(w_pool.shape[1]), wgt(w_ssd.shape[1]), wgt(w_attn.shape[1]),
                  gate(0), gate(1), gate(2)],
        out_specs=pl.BlockSpec((tm, tn), lambda j, i: (i, j)),
        out_shape=jax.ShapeDtypeStruct((m, d), BF16),
        compiler_params=_cparams("parallel", "parallel"),
        name="gated_merge",
    )(pool_o, ssd_o, attn_o, w_pool, w_ssd, w_attn, proj, proj, proj)


def _mm_res_kernel(a_ref, w_ref, x_ref, gt_ref, gpost_ref, gpre_ref, sc_ref, sh_ref, xo_ref, u_ref, *, nk, n_sub):
    tm = xo_ref.shape[0]
    slab = tm // n_sub

    def rows_of(ref, sl):
        return ref[...] if ref.shape[0] == 1 else ref[sl, :]

    def last_step(first):
        slabs = [slice(r * slab, (r + 1) * slab) for r in range(n_sub)]
        parts = [_dot(a_ref[sl, :], w_ref[...]) for sl in slabs]
        for sl, part in zip(slabs, parts):
            f = part if first else xo_ref[sl, :] + part
            xn = x_ref[sl, :] + rows_of(gt_ref, sl) * (_rms(f) * gpost_ref[...])
            xo_ref[sl, :] = xn
            y = _rms(xn) * gpre_ref[...]
            u_ref[sl, :] = (y * (1.0 + rows_of(sc_ref, sl)) + rows_of(sh_ref, sl)).astype(BF16)

    if nk == 1:
        last_step(True)
        return
    k = pl.program_id(2)

    @pl.when(k == 0)
    def _():
        xo_ref[...] = _dot(a_ref[...], w_ref[...])

    @pl.when((k > 0) & (k < nk - 1))
    def _():
        xo_ref[...] += _dot(a_ref[...], w_ref[...])

    @pl.when(k == nk - 1)
    def _():
        last_step(False)


def _pick_k(kdim, cap):
    for nk in range(1, kdim // LANES + 1):
        if kdim % nk == 0 and (kdim // nk) % LANES == 0 and kdim // nk <= cap:
            return kdim // nk
    return kdim


def _mm_res(a3, w, layer, x3, gt, g_post, g_pre, sc, sh):
    b, t, kdim = a3.shape
    d = x3.shape[2]
    r = gt.shape[1]
    tm = _pick(t, 512)
    tk = _pick_k(kdim, 2048)
    nk = kdim // tk
    mod_spec = pl.BlockSpec((None, r if r == 1 else tm, d),
                            (lambda bi, i, k: (bi, i, 0)) if r == t else (lambda bi, i, k: (bi, 0, 0)))
    vec_spec = pl.BlockSpec((1, d), lambda bi, i, k: (0, 0))
    row_spec = pl.BlockSpec((None, tm, d), lambda bi, i, k: (bi, i, 0))
    n_sub = 2 if tm % 512 == 0 else 1
    return pl.pallas_call(
        functools.partial(_mm_res_kernel, nk=nk, n_sub=n_sub),
        grid=(b, t // tm, nk),
        in_specs=[pl.BlockSpec((None, tm, tk), lambda bi, i, k: (bi, i, k)),
                  pl.BlockSpec((None, tk, d), lambda bi, i, k: (layer, k, 0)),
                  row_spec, mod_spec, vec_spec, vec_spec, mod_spec, mod_spec],
        out_specs=[row_spec, row_spec],
        out_shape=[jax.ShapeDtypeStruct((b, t, d), F32), jax.ShapeDtypeStruct((b, t, d), BF16)],
        compiler_params=_cparams("parallel", "parallel", "arbitrary"),
        name="matmul_residual_norm",
    )(a3, w, x3, gt, g_post, g_pre, sc, sh)


def _ffn_kernel(u_ref, wg_ref, wu_ref, o_ref, wg16_ref, wu16_ref):
    @pl.when(pl.program_id(1) == 0)
    def _():
        wg16_ref[...] = wg_ref[...].astype(BF16)
        wu16_ref[...] = wu_ref[...].astype(BF16)

    u = u_ref[...]
    o_ref[...] = (_silu(_dot(u, wg16_ref[...])) * _dot(u, wu16_ref[...])).astype(BF16)


def _ffn_hidden(u, w_gate, w_up, layer):
    m, k = u.shape
    n = w_gate.shape[2]
    tm = _pick(m, 1024)
    tn = _pick(n, 512)
    w_spec = pl.BlockSpec((None, k, tn), lambda j, i: (layer, 0, j))
    return pl.pallas_call(
        _ffn_kernel,
        grid=(n // tn, m // tm),
        in_specs=[pl.BlockSpec((tm, k), lambda j, i: (i, 0)), w_spec, w_spec],
        out_specs=pl.BlockSpec((tm, tn), lambda j, i: (i, j)),
        out_shape=jax.ShapeDtypeStruct((m, n), BF16),
        scratch_shapes=[pltpu.VMEM((k, tn), BF16), pltpu.VMEM((k, tn), BF16)],
        compiler_params=_cparams("parallel", "arbitrary"),
        name="ffn_hidden",
    )(u, w_gate, w_up)


def kernel(x_prompt, x_sample, cache_k, cache_v, state_ssm, state_conv, state_pool, page_table, c_prompt, c_sample, w_ada, b_ada, g_pre_mix, g_post_mix, g_pre_ffn, g_post_ffn, w_in, w_pool_grp, pool_scale, conv_w, conv_b, dt_bias, a_log, d_skip, ssd_norm, w_pool_br, w_ssd_br, w_attn_br, w_out, w_gate, w_up, w_down):
    bp, tp, d = x_prompt.shape
    bs, ts, _ = x_sample.shape
    depth = w_ada.shape[0]
    hd = d // ATTN_HEADS
    assert hd == LANES and ATTN_KV_HEADS * ATTN_GROUP * ts == LANES
    inner = ssd_norm.shape[1]
    n_heads = inner // SSD_HEADDIM
    conv_dim = conv_w.shape[2]
    attn_w = ATTN_HEADS * hd
    kv_w = ATTN_KV_HEADS * hd
    page = cache_k.shape[2]
    n_pages = page_table.shape[1]
    past = n_pages * page
    assert tp % MOBA_BLOCK == 0 and past % MOBA_BLOCK == 0 and n_heads <= LANES

    w_t = jnp.transpose(w_in, (0, 2, 1))
    col_z = d
    col_xbc = col_z + inner
    col_q = col_xbc + conv_dim
    col_k = col_q + attn_w
    col_v = col_k + kv_w
    col_gate = col_v + kv_w
    n_main = w_in.shape[2] - n_heads

    def project(u2, layer):
        proj = _matmul_wt(u2, w_t, layer, 0, n_main, col_q, n_heads)
        dt = _matmul_wt(u2, w_t, layer, col_q, LANES, LANES, 0)
        return proj, dt

    w_grp16 = w_pool_grp.astype(BF16)
    w_pool16, w_ssd16, w_attn16 = w_pool_br.astype(BF16), w_ssd_br.astype(BF16), w_attn_br.astype(BF16)
    w_out16, w_down16 = w_out.astype(BF16), w_down.astype(BF16)

    ssd_par = _ssd_params(conv_w, conv_b, dt_bias, a_log, d_skip, ssd_norm)

    n_c = bp + bs
    c_all = jnp.pad(jnp.concatenate([c_prompt, c_sample], axis=0), ((0, -n_c % 8), (0, 0)))
    mod = _ada(c_all, w_ada, b_ada)

    def mods(layer, which):
        m = mod[layer, :, which * d:(which + 1) * d]
        m_s = jnp.broadcast_to(m[bp:n_c, None, :], (bs, ts, d)).reshape(1, bs * ts, d)
        return m[:bp, None, :], m_s

    hg_w = (n_heads // SSD_GROUPS) * SSD_HEADDIM
    ts_pad = SSD_CHUNK
    pps = _pick(n_pages, 16)
    eye_kv = jnp.eye(ATTN_KV_HEADS, dtype=F32)

    xp, xs_ = x_prompt, x_sample.reshape(1, bs * ts, d)
    sh1 = mods(0, 0)
    sc1 = mods(0, 1)
    up = _norm_mod(xp, g_pre_mix[0:1], sc1[0], sh1[0])
    us = _norm_mod(xs_, g_pre_mix[0:1], sc1[1], sh1[1])

    outs = {n: [] for n in ("kp", "vp", "hp", "cp", "pp", "ks", "vs", "hs", "cs", "ps")}
    for layer in range(depth):
        gt1, sh2, sc2, gt2 = mods(layer, 2), mods(layer, 3), mods(layer, 4), mods(layer, 5)
        nxt = (layer + 1) % depth
        sh1n, sc1n = mods(nxt, 0), mods(nxt, 1)

        proj2, dt_p = project(up.reshape(bp * tp, d), layer)
        proj = proj2.reshape(bp, tp, -1)
        k_p, v_p = proj[:, :, col_k:col_v], proj[:, :, col_v:col_gate]
        pool_o = _pool(proj, jnp.zeros((bp, POOL_HIST, d), F32), w_grp16, pool_scale[layer:layer + 1], layer, 0)
        ssd_o, h_p = _ssd(proj, dt_p.reshape(bp, tp, LANES), jnp.zeros((bp, CONV_HIST, conv_dim), F32),
                          jnp.zeros((bp, SSD_GROUPS, hg_w, SSD_STATE), F32), ssd_par, layer, col_z, col_xbc, tp)
        attn_o = _moba_prompt(proj, col_q, col_k, col_v)
        merged = _merge(pool_o.reshape(bp * tp, d), ssd_o.reshape(bp * tp, inner), attn_o.reshape(bp * tp, attn_w),
                        w_pool16, w_ssd16, w_attn16, proj2, col_gate, layer)
        xp, u_ffn = _mm_res(merged.reshape(bp, tp, d), w_out16, layer, xp, gt1[0], g_post_mix[layer:layer + 1],
                            g_pre_ffn[layer:layer + 1], sc2[0], sh2[0])
        hid = _ffn_hidden(u_ffn.reshape(bp * tp, d), w_gate, w_up, layer)
        xp, up = _mm_res(hid.reshape(bp, tp, -1), w_down16, layer, xp, gt2[0], g_post_ffn[layer:layer + 1],
                         g_pre_mix[nxt:nxt + 1], sc1n[0], sh1n[0])
        outs["kp"].append(k_p.reshape(bp, tp, ATTN_KV_HEADS, hd))
        outs["vp"].append(v_p.reshape(bp, tp, ATTN_KV_HEADS, hd))
        outs["hp"].append(h_p.reshape(bp, n_heads, SSD_HEADDIM, SSD_STATE))
        outs["cp"].append(proj[:, tp - (SSD_CONV - 1):, col_xbc:col_xbc + conv_dim])
        outs["pp"].append(proj[:, tp - (POOL_HIST - 1):, :d])

        proj2, dt_s = project(us.reshape(bs * ts, d), layer)
        proj = proj2.reshape(bs, ts, -1)
        k_s, v_s, dt_s = proj[:, :, col_k:col_v], proj[:, :, col_v:col_gate], dt_s.reshape(bs, ts, LANES)
        pool_buf = jnp.pad(state_pool[layer], ((0, 0), (1, 0), (0, 0)))
        pool_o = _pool(proj, pool_buf, w_grp16, pool_scale[layer:layer + 1], layer, past)
        tpad = ((0, 0), (0, ts_pad - ts), (0, 0))
        conv_prev = jnp.pad(state_conv[layer], ((0, 0), (CONV_HIST - (SSD_CONV - 1), 0), (0, 0)))
        ssd_o, h_s = _ssd(jnp.pad(proj[:, :, :col_q], tpad), jnp.pad(dt_s, tpad), conv_prev,
                          state_ssm[layer].reshape(bs, SSD_GROUPS, hg_w, SSD_STATE), ssd_par, layer,
                          col_z, col_xbc, ts)
        ssd_o = ssd_o[:, :ts]
        q5 = proj[:, :, col_q:col_k].reshape(bs, ts, ATTN_KV_HEADS, ATTN_GROUP, hd)
        wq_t = jnp.einsum("btkgd,kj->bjdkgt", q5, eye_kv).reshape(bs, kv_w, LANES).astype(BF16)
        cache_k4 = cache_k.reshape(depth, cache_k.shape[1], page * ATTN_KV_HEADS, hd)
        cache_v4 = cache_v.reshape(depth, cache_v.shape[1], page * ATTN_KV_HEADS, hd)
        scores = _moba_scores(cache_k4, layer, page_table, wq_t, pps)
        probs = _moba_softmax(scores, k_s, wq_t, ts)
        o_full = _moba_pv(cache_v4, layer, page_table, probs, jnp.pad(v_s, ((0, 0), (0, LANES - ts), (0, 0))), pps)
        o6 = o_full.reshape(bs, ATTN_KV_HEADS, ATTN_GROUP, ts, ATTN_KV_HEADS, hd)
        attn_o = jnp.einsum("bkgtjd,kj->btkgd", o6, eye_kv).reshape(bs * ts, attn_w).astype(BF16)
        merged = _merge(pool_o.reshape(bs * ts, d), ssd_o.reshape(bs * ts, inner), attn_o,
                        w_pool16, w_ssd16, w_attn16, proj2, col_gate, layer)
        xs_, u_ffn = _mm_res(merged.reshape(1, bs * ts, d), w_out16, layer, xs_, gt1[1], g_post_mix[layer:layer + 1],
                             g_pre_ffn[layer:layer + 1], sc2[1], sh2[1])
        hid = _ffn_hidden(u_ffn.reshape(bs * ts, d), w_gate, w_up, layer)
        xs_, us = _mm_res(hid.reshape(1, bs * ts, -1), w_down16, layer, xs_, gt2[1], g_post_ffn[layer:layer + 1],
                          g_pre_mix[nxt:nxt + 1], sc1n[1], sh1n[1])
        outs["ks"].append(k_s.reshape(bs, ts, ATTN_KV_HEADS, hd))
        outs["vs"].append(v_s.reshape(bs, ts, ATTN_KV_HEADS, hd))
        outs["hs"].append(h_s.reshape(bs, n_heads, SSD_HEADDIM, SSD_STATE))
        conv_ext = jnp.concatenate([state_conv[layer], proj[:, :, col_xbc:col_xbc + conv_dim]], axis=1)
        outs["cs"].append(conv_ext[:, ts:])
        pool_ext = jnp.concatenate([state_pool[layer], proj[:, :, :d]], axis=1)
        outs["ps"].append(pool_ext[:, ts:])

    st = {n: jnp.stack(v) for n, v in outs.items()}
    return (xp, xs_.reshape(bs, ts, d), st["kp"], st["vp"], st["hp"], st["cp"], st["pp"],
            st["ks"], st["vs"], st["hs"], st["cs"], st["ps"])
```

```python
import functools

import jax
import jax.numpy as jnp
from jax import lax
from jax.experimental import pallas as pl
from jax.experimental.pallas import tpu as pltpu

F32 = jnp.float32
BF16 = jnp.bfloat16

NORM_EPS = 1e-6
POOL_WINDOWS = (2, 4, 8, 16)
POOL_HIST = 16
SSD_HEADDIM = 64
SSD_GROUPS = 8
SSD_STATE = 128
SSD_CONV = 4
SSD_CHUNK = 128
SSD_GROUPS_PER_STEP = 4
CONV_HIST = 8
ATTN_HEADS = 16
ATTN_KV_HEADS = 8
ATTN_GROUP = ATTN_HEADS // ATTN_KV_HEADS
MOBA_BLOCK = 256
MOBA_TOPK = 3
PAST_GROUP = 2
LANES = 128
VMEM_LIMIT_BYTES = 56 * 1024 * 1024
NEG_INF = float("-inf")
LOG2E = 1.4426950408889634


def _cparams(*sem):
    return pltpu.CompilerParams(dimension_semantics=sem, vmem_limit_bytes=VMEM_LIMIT_BYTES)


def _sigmoid(x):
    return 1.0 / (1.0 + jnp.exp(-x))


def _silu(x):
    return x * _sigmoid(x)


def _dot(a, b):
    return jnp.dot(a, b, preferred_element_type=F32)


def _dot_nt(a, b):
    return lax.dot_general(a, b, (((1,), (1,)), ((), ())), preferred_element_type=F32)


def _dot_tn(a, b):
    return lax.dot_general(a, b, (((0,), (0,)), ((), ())), preferred_element_type=F32)


def _rms(x):
    return x * lax.rsqrt(jnp.mean(x * x, axis=-1, keepdims=True) + NORM_EPS)


def _pick(total, pref):
    if total <= pref:
        return total
    t = pref
    while total % t:
        t //= 2
    return t


def _ada_kernel(c_ref, w_ref, b_ref, o_ref):
    a = _silu(c_ref[...]).astype(BF16)
    o_ref[...] = _dot(a, w_ref[...].astype(BF16)) + b_ref[...]


def _ada(c_all, w_ada, b_ada):
    depth, d, n = w_ada.shape
    rows = c_all.shape[0]
    tn = _pick(n, 1024)
    return pl.pallas_call(
        _ada_kernel,
        grid=(depth, n // tn),
        in_specs=[pl.BlockSpec((rows, d), lambda l, j: (0, 0)),
                  pl.BlockSpec((None, d, tn), lambda l, j: (l, 0, j)),
                  pl.BlockSpec((None, 1, tn), lambda l, j: (l, 0, j))],
        out_specs=pl.BlockSpec((None, rows, tn), lambda l, j: (l, 0, j)),
        out_shape=jax.ShapeDtypeStruct((depth, rows, n), F32),
        compiler_params=_cparams("parallel", "parallel"),
        name="ada_mod",
    )(c_all, w_ada, b_ada.reshape(depth, 1, n))


def _norm_mod_kernel(x_ref, g_ref, sc_ref, sh_ref, u_ref):
    y = _rms(x_ref[...]) * g_ref[...]
    u_ref[...] = (y * (1.0 + sc_ref[...]) + sh_ref[...]).astype(BF16)


def _norm_mod(x, g, sc, sh):
    b, t, d = x.shape
    r = sc.shape[1]
    tt = _pick(t, 512)
    mod_spec = pl.BlockSpec((None, r if r == 1 else tt, d),
                            (lambda i, j: (i, j, 0)) if r == t else (lambda i, j: (i, 0, 0)))
    return pl.pallas_call(
        _norm_mod_kernel,
        grid=(b, t // tt),
        in_specs=[pl.BlockSpec((None, tt, d), lambda i, j: (i, j, 0)),
                  pl.BlockSpec((1, d), lambda i, j: (0, 0)),
                  mod_spec, mod_spec],
        out_specs=pl.BlockSpec((None, tt, d), lambda i, j: (i, j, 0)),
        out_shape=jax.ShapeDtypeStruct((b, t, d), BF16),
        compiler_params=_cparams("parallel", "parallel"),
        name="norm_mod",
    )(x, g, sc, sh)


def _mm_kernel(a_ref, a2_ref, w_ref, o_ref, o2_ref, w16_ref, *, n_main):
    i = pl.program_id(1)

    @pl.when(i == 0)
    def _():
        w16_ref[...] = w_ref[0].astype(BF16)

    @pl.when(i < n_main)
    def _():
        o_ref[...] = _dot_nt(a_ref[...], w16_ref[...])

    @pl.when(i == n_main)
    def _():
        o2_ref[...] = _dot_nt(a2_ref[...], w16_ref[...])


def _matmul_wt(a, a2, w_t, layer, row0, n, skip_at, skip):
    m, k = a.shape
    m2 = a2.shape[0]
    tm = _pick(m, 1024)
    tn = _pick(n, 1024)
    assert skip_at % tn == 0
    hole = skip_at // tn
    n_main = m // tm

    def w_index(j, i):
        return layer, pl.multiple_of(row0 + j * tn + jnp.where(j >= hole, skip, 0), 8), 0

    return pl.pallas_call(
        functools.partial(_mm_kernel, n_main=n_main),
        grid=(n // tn, n_main + 1),
        in_specs=[pl.BlockSpec((tm, k), lambda j, i: (jnp.minimum(i, n_main - 1), 0)),
                  pl.BlockSpec((m2, k), lambda j, i: (0, 0)),
                  pl.BlockSpec((pl.Element(1), pl.Element(tn), pl.Element(k)), w_index)],
        out_specs=[pl.BlockSpec((tm, tn), lambda j, i: (jnp.minimum(i, n_main - 1), j)),
                   pl.BlockSpec((m2, tn), lambda j, i: (0, j))],
        out_shape=[jax.ShapeDtypeStruct((m, n), F32), jax.ShapeDtypeStruct((m2, n), F32)],
        scratch_shapes=[pltpu.VMEM((tn, k), BF16)],
        compiler_params=_cparams("parallel", "arbitrary"),
        name="proj_matmul",
    )(a, a2, w_t)


def _pool_kernel(p_ref, buf_ref, w_ref, sc_ref, o_ref, ext_ref, *, tt, start, pg):
    t = pl.program_id(1)

    @pl.when(t == 0)
    def _():
        ext_ref[0:POOL_HIST, :] = buf_ref[...]

    ext_ref[POOL_HIST:POOL_HIST + tt, :] = p_ref[...]
    n_valid = lax.broadcasted_iota(jnp.int32, (tt, 1), 0) + (t * tt + start + 1)
    for gi, w in enumerate(POOL_WINDOWS):
        c0 = gi * pg
        e = ext_ref[:, c0:c0 + pg]
        x = e[POOL_HIST:]
        s = e
        k = 1
        while k < w:
            s = s + pltpu.roll(s, k, axis=0)
            k *= 2
        s = s[POOL_HIST:]
        cnt = jnp.minimum(n_valid, w).astype(F32)
        mixed = (s / cnt - x).astype(BF16)
        y = _dot(mixed, w_ref[gi]) * sc_ref[:, c0:c0 + pg]
        o_ref[:, c0:c0 + pg] = y.astype(BF16)
    ext_ref[0:POOL_HIST, :] = ext_ref[tt:tt + POOL_HIST, :]


def _pool(proj3, buf, w_grp, scale, layer, start):
    b, t, _ = proj3.shape
    c = scale.shape[1]
    pg = c // len(POOL_WINDOWS)
    tt = _pick(t, 256)
    return pl.pallas_call(
        functools.partial(_pool_kernel, tt=tt, start=start, pg=pg),
        grid=(b, t // tt),
        in_specs=[pl.BlockSpec((None, tt, c), lambda i, j: (i, j, 0)),
                  pl.BlockSpec((None, POOL_HIST, c), lambda i, j: (i, 0, 0)),
                  pl.BlockSpec((None, len(POOL_WINDOWS), pg, pg), lambda i, j: (layer, 0, 0, 0)),
                  pl.BlockSpec((1, c), lambda i, j: (0, 0))],
        out_specs=pl.BlockSpec((None, tt, c), lambda i, j: (i, j, 0)),
        out_shape=jax.ShapeDtypeStruct((b, t, c), BF16),
        scratch_shapes=[pltpu.VMEM((POOL_HIST + tt, c), F32)],
        compiler_params=_cparams("parallel", "arbitrary"),
        name="pool_mixer",
    )(proj3, buf, w_grp, scale)


def _ssd_kernel(z_ref, x_ref, b_ref, c_ref, dt_ref, cpx_ref, cpb_ref, cpc_ref,
                px_ref, pb_ref, pc_ref, pdt_ref, exp_ref, h0_ref,
                y_ref, hout_ref, histx, histb, histc, ht, *, t_valid, n_chunks, hg, gps):
    gq = pl.program_id(1)
    c = pl.program_id(2)
    lc = SSD_CHUNK
    gw = hg * SSD_HEADDIM
    ns = SSD_STATE

    @pl.when(c == 0)
    def _():
        histx[...] = cpx_ref[...]
        histb[...] = cpb_ref[...]
        histc[...] = cpc_ref[...]
        for s in range(gps):
            ht[s] = h0_ref[s].T

    def conv(hist, raw_ref, par_ref):
        x = raw_ref[...]
        prev = hist[...]
        hist[...] = x[lc - CONV_HIST:, :]
        row8 = lax.broadcasted_iota(jnp.int32, prev.shape, 0)
        acc = par_ref[SSD_CONV:SSD_CONV + 1, :] + x * par_ref[SSD_CONV - 1:SSD_CONV, :]
        for k in range(1, SSD_CONV):
            r = pltpu.roll(x, k, axis=0)
            first = jnp.where(row8 < k, pltpu.roll(prev, k, axis=0), r[0:CONV_HIST])
            shifted = jnp.concatenate([first, r[CONV_HIST:]], axis=0)
            acc = acc + shifted * par_ref[SSD_CONV - 1 - k:SSD_CONV - k, :]
        return _silu(acc)

    xs_all = conv(histx, x_ref, px_ref)
    bm_all = conv(histb, b_ref, pb_ref)
    cm_all = conv(histc, c_ref, pc_ref)

    row = lax.broadcasted_iota(jnp.int32, (lc, LANES), 0)
    col = lax.broadcasted_iota(jnp.int32, (lc, LANES), 1)
    xdt = dt_ref[...] + pdt_ref[0:1, :]
    dt = jnp.maximum(xdt, 0.0) + jnp.log1p(jnp.exp(-jnp.abs(xdt)))
    dt = jnp.where(c * lc + row < t_valid, dt, 0.0)
    dta = dt * (-jnp.exp(pdt_ref[1:2, :]))
    acs = dta
    k = 1
    while k < lc:
        acs = acs + jnp.where(row >= k, pltpu.roll(acs, k, axis=0), 0.0)
        k *= 2
    a_last = acs[lc - 1:lc, :]
    w_end = dt * jnp.exp(a_last - acs)
    exp_a = jnp.exp(acs)

    cd = jnp.exp(a_last)
    cd_hi = cd.astype(BF16).astype(F32)
    cd_mid = (cd - cd_hi).astype(BF16).astype(F32)
    cd_lo = cd - cd_hi - cd_mid
    pieces = [jnp.broadcast_to(p, (16, LANES)) for p in (cd_hi, cd_mid, cd_lo)]
    stacked = jnp.concatenate([dt, w_end, exp_a] + pieces, axis=0).astype(BF16)
    acs2 = acs * LOG2E
    causal = row >= col
    heads_here = LANES // SSD_HEADDIM

    for s in range(gps):
        xl, nl = slice(s * gw, (s + 1) * gw), slice(s * ns, (s + 1) * ns)
        xs, bm, cm = xs_all[:, xl], bm_all[:, nl], cm_all[:, nl]
        wide = _dot(stacked, exp_ref[s])
        dt_e, w_e, ea_e = wide[0:lc], wide[lc:2 * lc], wide[2 * lc:3 * lc]
        r0 = 3 * lc
        cd_e = wide[r0:r0 + 1] + wide[r0 + 16:r0 + 17] + wide[r0 + 32:r0 + 33]

        acs_g = pltpu.roll(acs2, (LANES - (gq * gps + s) * hg) % LANES, axis=1)
        acs_gt = acs_g.T

        bm16 = bm.astype(BF16)
        cm16 = cm.astype(BF16)
        cb = jnp.where(causal, _dot_nt(cm16, bm16), 0.0)
        xdt_e = xs * dt_e
        y_parts = []
        for pr in range(gw // LANES):
            xp = xdt_e[:, pr * LANES:(pr + 1) * LANES]
            lhs, rhs = [], []
            for hh in range(heads_here):
                h = pr * heads_here + hh
                seg = acs_g[:, h:h + 1] - acs_gt[h:h + 1, :]
                lhs.append((cb * jnp.exp2(jnp.minimum(seg, 0.0))).astype(BF16))
                in_head = (col >= hh * SSD_HEADDIM) & (col < (hh + 1) * SSD_HEADDIM)
                rhs.append(jnp.where(in_head, xp, 0.0).astype(BF16))
            y_parts.append(_dot(jnp.concatenate(lhs, axis=1), jnp.concatenate(rhs, axis=0)))
        y = jnp.concatenate(y_parts, axis=1)

        h_in = ht[s]
        y = y + _dot(cm16, h_in.astype(BF16)) * ea_e
        h_new = h_in * cd_e + _dot(bm.T.astype(BF16), (xs * w_e).astype(BF16))
        ht[s] = h_new

        y = y + px_ref[SSD_CONV + 1:SSD_CONV + 2, xl] * xs
        y = y * _silu(z_ref[:, xl])
        y_ref[:, xl] = (_rms(y) * px_ref[SSD_CONV + 2:SSD_CONV + 3, xl]).astype(BF16)

    @pl.when(c == n_chunks - 1)
    def _():
        for s in range(gps):
            hout_ref[s] = ht[s].T


def _ssd_params(conv_w, conv_b, dt_bias, a_log, d_skip, ssd_norm):
    depth, inner = ssd_norm.shape
    n_heads = dt_bias.shape[1]
    rows = lambda v: v[:, None, :]
    px = jnp.concatenate([conv_w[:, :, :inner], rows(conv_b[:, :inner]), rows(jnp.repeat(d_skip, SSD_HEADDIM, axis=1)),
                          rows(ssd_norm), jnp.zeros((depth, 8 - SSD_CONV - 3, inner), F32)], axis=1)
    pbc = jnp.concatenate([conv_w[:, :, inner:], rows(conv_b[:, inner:]),
                           jnp.zeros((depth, 8 - SSD_CONV - 1, conv_w.shape[2] - inner), F32)], axis=1)
    pdt = jnp.pad(jnp.stack([dt_bias, a_log], axis=1), ((0, 0), (0, 6), (0, LANES - n_heads)))
    return px, pbc, pdt


def _ssd(proj3, dt3, conv_prev, h0, params, layer, col_z, col_xbc, t_valid):
    b, tp, _ = proj3.shape
    px, pbc, pdt = params
    inner = px.shape[2]
    n_heads = inner // SSD_HEADDIM
    hg = n_heads // SSD_GROUPS
    gw = hg * SSD_HEADDIM
    ns = SSD_STATE
    lc = SSD_CHUNK
    n_chunks = tp // lc
    gps = SSD_GROUPS_PER_STEP
    xw, nw = gps * gw, gps * ns
    assert col_z % xw == 0 and col_xbc % xw == 0 and (col_xbc + inner) % nw == 0 and SSD_GROUPS % gps == 0
    zb, xb = col_z // xw, col_xbc // xw
    bb = (col_xbc + inner) // nw
    cb_ = (col_xbc + inner + SSD_GROUPS * ns) // nw
    cxb, cbb, ccb = 0, inner // nw, (inner + SSD_GROUPS * ns) // nw
    head_of = (jnp.arange(SSD_GROUPS)[:, None] * hg + jnp.arange(gw)[None, :] // SSD_HEADDIM)
    expand = (jnp.arange(LANES)[None, :, None] == head_of[:, None, :]).astype(BF16)

    def seq(width, off):
        return pl.BlockSpec((None, lc, width), lambda i, g, c: (i, c, off + g))

    def prev(width, off):
        return pl.BlockSpec((None, CONV_HIST, width), lambda i, g, c: (i, 0, off + g))

    def par(width, off):
        return pl.BlockSpec((None, 8, width), lambda i, g, c: (layer, 0, off + g))

    state_spec = pl.BlockSpec((None, gps, gw, ns), lambda i, g, c: (i, g, 0, 0))
    kernel = functools.partial(_ssd_kernel, t_valid=t_valid, n_chunks=n_chunks, hg=hg, gps=gps)
    return pl.pallas_call(
        kernel,
        grid=(b, SSD_GROUPS // gps, n_chunks),
        in_specs=[seq(xw, zb), seq(xw, xb), seq(nw, bb), seq(nw, cb_),
                  pl.BlockSpec((None, lc, LANES), lambda i, g, c: (i, c, 0)),
                  prev(xw, cxb), prev(nw, cbb), prev(nw, ccb),
                  par(xw, 0), par(nw, 0), par(nw, SSD_GROUPS // gps),
                  pl.BlockSpec((None, 8, LANES), lambda i, g, c: (layer, 0, 0)),
                  pl.BlockSpec((gps, LANES, gw), lambda i, g, c: (g, 0, 0)),
                  state_spec],
        out_specs=[pl.BlockSpec((None, lc, xw), lambda i, g, c: (i, c, g)), state_spec],
        out_shape=[jax.ShapeDtypeStruct((b, tp, inner), BF16),
                   jax.ShapeDtypeStruct((b, SSD_GROUPS, gw, ns), F32)],
        scratch_shapes=[pltpu.VMEM((CONV_HIST, xw), F32),
                        pltpu.VMEM((CONV_HIST, nw), F32),
                        pltpu.VMEM((CONV_HIST, nw), F32),
                        pltpu.VMEM((gps, ns, gw), F32)],
        compiler_params=_cparams("parallel", "parallel", "arbitrary"),
        name="ssd_mixer",
    )(proj3, proj3, proj3, proj3, dt3, conv_prev, conv_prev, conv_prev, px, pbc, pbc, pdt, expand, h0)


def _top_k_mask(gate, valid, ids, axis):
    big = jnp.int32(2 ** 30)
    remaining = valid
    sel = jnp.zeros(gate.shape, F32)
    for _ in range(MOBA_TOPK):
        gm = jnp.where(remaining > 0.5, gate, NEG_INF)
        m = jnp.max(gm, axis=axis, keepdims=True)
        cand = jnp.where(gm == m, remaining, 0.0)
        first = jnp.min(jnp.where(cand > 0.5, ids, big), axis=axis, keepdims=True)
        pick = jnp.where(ids == first, 1.0, 0.0)
        sel = sel + pick
        remaining = remaining - pick
    return sel


def _moba_prompt_kernel(q_ref, k_ref, v_ref, o_ref, kb, vtb, km, sel_ref, qs_ref, acc_ref, ml_ref, sa_ref, sb_ref,
                        *, n_blocks, scale):
    i = pl.program_id(2)
    blk = MOBA_BLOCK
    hd = LANES
    nq = ATTN_GROUP * blk
    c2 = scale * LOG2E

    @pl.when(i == 0)
    def _():
        km[...] = jnp.zeros(km.shape, F32)
        for j in range(n_blocks):
            kj = k_ref[j * blk:(j + 1) * blk, :]
            kb[j] = kj.astype(BF16)
            km[j:j + 1, :] = jnp.mean(kj, axis=0, keepdims=True)
            vtb[j] = v_ref[j * blk:(j + 1) * blk, :].T.astype(BF16)

    grp = PAST_GROUP

    def block_at(pos):
        return jnp.where(pos == 0, i, jnp.minimum(pos - 1, n_blocks - 1))

    def scores(s_ref, p0, own_first=False):
        for k in range(grp):
            s = _dot_nt(kb[block_at(p0 + k)], qs_ref[...])
            if own_first and k == 0:
                key_r = lax.broadcasted_iota(jnp.int32, (blk, nq), 0)
                qry_c = lax.broadcasted_iota(jnp.int32, (blk, nq), 1) & (blk - 1)
                s = jnp.where(key_r <= qry_c, s, NEG_INF)
            s_ref[k] = s

    q = q_ref[...]
    qs_ref[...] = jnp.concatenate([q[:, g * hd:(g + 1) * hd] for g in range(ATTN_GROUP)], axis=0).astype(BF16)
    scores(sa_ref, 0, own_first=True)
    gate_t = _dot_nt(km[...].astype(BF16), qs_ref[...])
    ids = lax.broadcasted_iota(jnp.int32, (km.shape[0], nq), 0)
    valid = jnp.where(ids < i, 1.0, 0.0)
    sel_ref[...] = _top_k_mask(gate_t, valid, ids, axis=0)
    ml_ref[0:1, :] = jnp.full((1, nq), NEG_INF, F32)
    ml_ref[1:2, :] = jnp.zeros((1, nq), F32)
    acc_ref[...] = jnp.zeros(acc_ref.shape, F32)

    def update(s_ref, p0):
        m_old, l_old = ml_ref[0:1, :], ml_ref[1:2, :]
        picked = [sel_ref[pl.ds(jnp.maximum(p0 + k - 1, 0), 1), :] + jnp.where(p0 + k == 0, 1.0, 0.0) > 0.5
                  for k in range(grp)]
        m_new = m_old
        for k in range(grp):
            m_new = jnp.maximum(m_new, jnp.where(picked[k], jnp.max(s_ref[k], axis=0, keepdims=True), NEG_INF))
        alpha = jnp.exp2((m_old - m_new) * c2)
        l_new = alpha * l_old
        pv = None
        for k in range(grp):
            p = jnp.exp2((s_ref[k] - jnp.where(picked[k], m_new, jnp.inf)) * c2)
            l_new = l_new + jnp.sum(p, axis=0, keepdims=True)
            d = _dot(vtb[block_at(p0 + k)], p.astype(BF16))
            pv = d if pv is None else pv + d
        acc_ref[...] = alpha * acc_ref[...] + pv
        ml_ref[0:1, :] = m_new
        ml_ref[1:2, :] = l_new

    n_groups = (i + grp) // grp

    def pair_body(u, carry):
        p0 = 2 * u * grp
        scores(sb_ref, p0 + grp)
        update(sa_ref, p0)
        scores(sa_ref, p0 + 2 * grp)
        update(sb_ref, p0 + grp)
        return carry

    lax.fori_loop(0, n_groups // 2, pair_body, 0)

    @pl.when((n_groups & 1) != 0)
    def _():
        update(sa_ref, (n_groups - 1) * grp)

    o = (acc_ref[...] / ml_ref[1:2, :]).T
    o_ref[...] = jnp.concatenate([o[g * blk:(g + 1) * blk] for g in range(ATTN_GROUP)], axis=1).astype(BF16)


def _moba_prompt(proj3, col_q, col_k, col_v):
    b, t, _ = proj3.shape
    hd = LANES
    blk = MOBA_BLOCK
    n_blocks = t // blk
    nb_pad = -(-(n_blocks + PAST_GROUP) // 16) * 16
    qw = ATTN_GROUP * hd
    qb, kb0, vb0 = col_q // qw, col_k // hd, col_v // hd
    kernel = functools.partial(_moba_prompt_kernel, n_blocks=n_blocks, scale=hd ** -0.5)
    return pl.pallas_call(
        kernel,
        grid=(b, ATTN_KV_HEADS, n_blocks),
        in_specs=[pl.BlockSpec((None, blk, qw), lambda bi, h, i: (bi, i, qb + h)),
                  pl.BlockSpec((None, t, hd), lambda bi, h, i: (bi, 0, kb0 + h)),
                  pl.BlockSpec((None, t, hd), lambda bi, h, i: (bi, 0, vb0 + h))],
        out_specs=pl.BlockSpec((None, blk, qw), lambda bi, h, i: (bi, i, h)),
        out_shape=jax.ShapeDtypeStruct((b, t, ATTN_HEADS * hd), BF16),
        scratch_shapes=[pltpu.VMEM((n_blocks, blk, hd), BF16), pltpu.VMEM((n_blocks, hd, blk), BF16),
                        pltpu.VMEM((nb_pad, hd), F32), pltpu.VMEM((nb_pad, ATTN_GROUP * blk), F32),
                        pltpu.VMEM((ATTN_GROUP * blk, hd), BF16), pltpu.VMEM((hd, ATTN_GROUP * blk), F32),
                        pltpu.VMEM((8, ATTN_GROUP * blk), F32),
                        pltpu.VMEM((PAST_GROUP, blk, ATTN_GROUP * blk), F32),
                        pltpu.VMEM((PAST_GROUP, blk, ATTN_GROUP * blk), F32)],
        compiler_params=_cparams("parallel", "parallel", "arbitrary"),
        name="moba_prompt",
    )(proj3, proj3, proj3)


def _moba_scores_kernel(pt_ref, wq_ref, *refs, pps, page):
    del pt_ref
    k_refs, o_ref = refs[:pps], refs[pps]
    hd = LANES
    for r in range(pps):
        acc = None
        for h in range(ATTN_KV_HEADS):
            kh = k_refs[r][pl.ds(h, page, stride=ATTN_KV_HEADS), :].astype(BF16)
            part = _dot(kh, wq_ref[h * hd:(h + 1) * hd, :])
            acc = part if acc is None else acc + part
        o_ref[r * page:(r + 1) * page, :] = acc


def _moba_scores(cache4, layer, page_table, wq_t, pps):
    b, n_pages = page_table.shape
    rows, hd = cache4.shape[2], cache4.shape[3]
    page, width = rows // ATTN_KV_HEADS, hd * ATTN_KV_HEADS

    def page_spec(r):
        return pl.BlockSpec((None, None, rows, hd), lambda bi, p, pt: (layer, pt[bi, p * pps + r], 0, 0))

    grid_spec = pltpu.PrefetchScalarGridSpec(
        num_scalar_prefetch=1,
        grid=(b, n_pages // pps),
        in_specs=[pl.BlockSpec((None, width, LANES), lambda bi, p, pt: (bi, 0, 0))]
                 + [page_spec(r) for r in range(pps)],
        out_specs=pl.BlockSpec((None, pps * page, LANES), lambda bi, p, pt: (bi, p, 0)),
    )
    return pl.pallas_call(
        functools.partial(_moba_scores_kernel, pps=pps, page=page),
        grid_spec=grid_spec,
        out_shape=jax.ShapeDtypeStruct((b, n_pages * page, LANES), F32),
        compiler_params=_cparams("parallel", "arbitrary"),
        name="moba_sample_scores",
    )(page_table, wq_t, *([cache4] * pps))


def _moba_softmax_kernel(st_ref, kn_ref, wq_ref, p_ref, gate_ref, sel_ref, *, n_blocks, tq, scale):
    blk = MOBA_BLOCK
    past = n_blocks * blk
    gate_ref[...] = jnp.zeros(gate_ref.shape, F32)

    def gate_body(j, carry):
        off = pl.multiple_of(j * blk, blk)
        gate_ref[pl.ds(j, 1), :] = jnp.mean(st_ref[pl.ds(off, blk), :], axis=0, keepdims=True)
        return carry

    lax.fori_loop(0, n_blocks, gate_body, 0)
    nrow = gate_ref.shape[0]
    ids = lax.broadcasted_iota(jnp.int32, (nrow, LANES), 0)
    valid = jnp.where(ids < n_blocks, 1.0, 0.0)
    sel_ref[...] = _top_k_mask(gate_ref[...], valid, ids, axis=0)

    s_own = _dot(kn_ref[...].astype(BF16), wq_ref[...]) * scale
    key_t = lax.broadcasted_iota(jnp.int32, (tq, LANES), 0)
    qry_t = lax.broadcasted_iota(jnp.int32, (tq, LANES), 1) & (tq - 1)
    s_own = jnp.where(key_t <= qry_t, s_own, NEG_INF)
    m0 = jnp.max(s_own, axis=0, keepdims=True)

    def max_body(j, m):
        off = pl.multiple_of(j * blk, blk)
        bm = jnp.max(st_ref[pl.ds(off, blk), :], axis=0, keepdims=True) * scale
        return jnp.maximum(m, jnp.where(sel_ref[pl.ds(j, 1), :] > 0.5, bm, NEG_INF))

    m = lax.fori_loop(0, n_blocks, max_body, m0)
    p_own = jnp.exp(s_own - m)
    l0 = jnp.sum(p_own, axis=0, keepdims=True)

    def exp_body(j, l):
        off = pl.multiple_of(j * blk, blk)
        s = st_ref[pl.ds(off, blk), :] * scale
        p = jnp.where(sel_ref[pl.ds(j, 1), :] > 0.5, jnp.exp(s - m), 0.0)
        p_ref[pl.ds(off, blk), :] = p
        return l + jnp.sum(p, axis=0, keepdims=True)

    l = lax.fori_loop(0, n_blocks, exp_body, l0)
    inv = 1.0 / l

    def norm_body(j, carry):
        off = pl.multiple_of(j * blk, blk)
        p_ref[pl.ds(off, blk), :] = p_ref[pl.ds(off, blk), :] * inv
        return carry

    lax.fori_loop(0, n_blocks, norm_body, 0)
    p_ref[past:past + LANES, :] = jnp.zeros((LANES, LANES), F32)
    p_ref[past:past + tq, :] = p_own * inv


def _moba_softmax(scores, k_new, wq_t, tq):
    b, past, _ = scores.shape
    width = k_new.shape[2]
    n_blocks = past // MOBA_BLOCK
    nrow = -(-n_blocks // 8) * 8
    kernel = functools.partial(_moba_softmax_kernel, n_blocks=n_blocks, tq=tq, scale=LANES ** -0.5)
    return pl.pallas_call(
        kernel,
        grid=(b,),
        in_specs=[pl.BlockSpec((None, past, LANES), lambda bi: (bi, 0, 0)),
                  pl.BlockSpec((None, tq, width), lambda bi: (bi, 0, 0)),
                  pl.BlockSpec((None, width, LANES), lambda bi: (bi, 0, 0))],
        out_specs=pl.BlockSpec((None, past + LANES, LANES), lambda bi: (bi, 0, 0)),
        out_shape=jax.ShapeDtypeStruct((b, past + LANES, LANES), F32),
        scratch_shapes=[pltpu.VMEM((nrow, LANES), F32), pltpu.VMEM((nrow, LANES), F32)],
        compiler_params=_cparams("parallel"),
        name="moba_sample_softmax",
    )(scores, k_new, wq_t)


def _moba_pv_kernel(pt_ref, p_ref, pown_ref, vnew_ref, *refs, pps, page):
    del pt_ref
    v_refs, o_ref = refs[:pps], refs[pps]

    @pl.when(pl.program_id(1) == 0)
    def _():
        o_ref[...] = _dot_tn(pown_ref[...].astype(BF16), vnew_ref[...].astype(BF16))

    hd = LANES
    for h in range(ATTN_KV_HEADS):
        acc = o_ref[:, h * hd:(h + 1) * hd]
        for r in range(pps):
            vh = v_refs[r][pl.ds(h, page, stride=ATTN_KV_HEADS), :].astype(BF16)
            acc = acc + _dot_tn(p_ref[r * page:(r + 1) * page, :].astype(BF16), vh)
        o_ref[:, h * hd:(h + 1) * hd] = acc


def _moba_pv(cache4, layer, page_table, probs, v_new_pad, pps):
    b, n_pages = page_table.shape
    rows, hd = cache4.shape[2], cache4.shape[3]
    page, width = rows // ATTN_KV_HEADS, hd * ATTN_KV_HEADS
    own_blk = (n_pages * page) // LANES

    def page_spec(r):
        return pl.BlockSpec((None, None, rows, hd), lambda bi, p, pt: (layer, pt[bi, p * pps + r], 0, 0))

    grid_spec = pltpu.PrefetchScalarGridSpec(
        num_scalar_prefetch=1,
        grid=(b, n_pages // pps),
        in_specs=[pl.BlockSpec((None, pps * page, LANES), lambda bi, p, pt: (bi, p, 0)),
                  pl.BlockSpec((None, LANES, LANES), lambda bi, p, pt: (bi, own_blk, 0)),
                  pl.BlockSpec((None, LANES, width), lambda bi, p, pt: (bi, 0, 0))]
                 + [page_spec(r) for r in range(pps)],
        out_specs=pl.BlockSpec((None, LANES, width), lambda bi, p, pt: (bi, 0, 0)),
    )
    return pl.pallas_call(
        functools.partial(_moba_pv_kernel, pps=pps, page=page),
        grid_spec=grid_spec,
        out_shape=jax.ShapeDtypeStruct((b, LANES, width), F32),
        compiler_params=_cparams("parallel", "arbitrary"),
        name="moba_sample_pv",
    )(page_table, probs, probs, v_new_pad, *([cache4] * pps))


def _merge_kernel(ap_ref, as_ref, aa_ref, wp_ref, ws_ref, wa_ref, g0_ref, g1_ref, g2_ref, o_ref):
    acc = _sigmoid(g0_ref[...]) * _dot(ap_ref[...], wp_ref[...])
    acc = acc + _sigmoid(g1_ref[...]) * _dot(as_ref[...], ws_ref[...])
    acc = acc + _sigmoid(g2_ref[...]) * _dot(aa_ref[...], wa_ref[...])
    o_ref[...] = acc.astype(BF16)


def _merge(pool_o, ssd_o, attn_o, w_pool, w_ssd, w_attn, proj, col_gate, layer):
    m, d = pool_o.shape
    tm = _pick(m, 512)
    tn = _pick(d, 512)
    gb = col_gate // tn
    nb = d // tn

    def act(width):
        return pl.BlockSpec((tm, width), lambda j, i: (i, 0))

    def wgt(kdim):
        return pl.BlockSpec((None, kdim, tn), lambda j, i: (layer, 0, j))

    def gate(which):
        return pl.BlockSpec((tm, tn), lambda j, i: (i, gb + which * nb + j))

    return pl.pallas_call(
        _merge_kernel,
        grid=(d // tn, m // tm),
        in_specs=[act(pool_o.shape[1]), act(ssd_o.shape[1]), act(attn_o.shape[1]),
                  wgt(w_pool.shape[1]), wgt(w_ssd.shape[1]), wgt(w_attn.shape[1]),
                  gate(0), gate(1), gate(2)],
        out_specs=pl.BlockSpec((tm, tn), lambda j, i: (i, j)),
        out_shape=jax.ShapeDtypeStruct((m, d), BF16),
        compiler_params=_cparams("parallel", "parallel"),
        name="gated_merge",
    )(pool_o, ssd_o, attn_o, w_pool, w_ssd, w_attn, proj, proj, proj)


def _mm_res_kernel(a_ref, w_ref, x_ref, gt_ref, gpost_ref, gpre_ref, sc_ref, sh_ref, xo_ref, u_ref, *, nk, n_sub):
    tm = xo_ref.shape[0]
    slab = tm // n_sub

    def rows_of(ref, sl):
        return ref[...] if ref.shape[0] == 1 else ref[sl, :]

    def last_step(first):
        slabs = [slice(r * slab, (r + 1) * slab) for r in range(n_sub)]
        parts = [_dot(a_ref[sl, :], w_ref[...]) for sl in slabs]
        for sl, part in zip(slabs, parts):
            f = part if first else xo_ref[sl, :] + part
            xn = x_ref[sl, :] + rows_of(gt_ref, sl) * (_rms(f) * gpost_ref[...])
            xo_ref[sl, :] = xn
            y = _rms(xn) * gpre_ref[...]
            u_ref[sl, :] = (y * (1.0 + rows_of(sc_ref, sl)) + rows_of(sh_ref, sl)).astype(BF16)

    if nk == 1:
        last_step(True)
        return
    k = pl.program_id(2)

    @pl.when(k == 0)
    def _():
        xo_ref[...] = _dot(a_ref[...], w_ref[...])

    @pl.when((k > 0) & (k < nk - 1))
    def _():
        xo_ref[...] += _dot(a_ref[...], w_ref[...])

    @pl.when(k == nk - 1)
    def _():
        last_step(False)


def _pick_k(kdim, cap):
    for nk in range(1, kdim // LANES + 1):
        if kdim % nk == 0 and (kdim // nk) % LANES == 0 and kdim // nk <= cap:
            return kdim // nk
    return kdim


def _mm_res(a3, w, layer, x3, gt, g_post, g_pre, sc, sh):
    b, t, kdim = a3.shape
    d = x3.shape[2]
    r = gt.shape[1]
    tm = _pick(t, 512)
    tk = _pick_k(kdim, 2048)
    nk = kdim // tk
    mod_spec = pl.BlockSpec((None, r if r == 1 else tm, d),
                            (lambda bi, i, k: (bi, i, 0)) if r == t else (lambda bi, i, k: (bi, 0, 0)))
    vec_spec = pl.BlockSpec((1, d), lambda bi, i, k: (0, 0))
    row_spec = pl.BlockSpec((None, tm, d), lambda bi, i, k: (bi, i, 0))
    n_sub = 2 if tm % 512 == 0 else 1
    return pl.pallas_call(
        functools.partial(_mm_res_kernel, nk=nk, n_sub=n_sub),
        grid=(b, t // tm, nk),
        in_specs=[pl.BlockSpec((None, tm, tk), lambda bi, i, k: (bi, i, k)),
                  pl.BlockSpec((None, tk, d), lambda bi, i, k: (layer, k, 0)),
                  row_spec, mod_spec, vec_spec, vec_spec, mod_spec, mod_spec],
        out_specs=[row_spec, row_spec],
        out_shape=[jax.ShapeDtypeStruct((b, t, d), F32), jax.ShapeDtypeStruct((b, t, d), BF16)],
        compiler_params=_cparams("parallel", "parallel", "arbitrary"),
        name="matmul_residual_norm",
    )(a3, w, x3, gt, g_post, g_pre, sc, sh)


def _ffn_kernel(u_ref, u2_ref, wg_ref, wu_ref, o_ref, o2_ref, wg16_ref, wu16_ref, *, n_main):
    i = pl.program_id(1)

    @pl.when(i == 0)
    def _():
        wg16_ref[...] = wg_ref[...].astype(BF16)
        wu16_ref[...] = wu_ref[...].astype(BF16)

    def hidden(u):
        return (_silu(_dot(u, wg16_ref[...])) * _dot(u, wu16_ref[...])).astype(BF16)

    @pl.when(i < n_main)
    def _():
        o_ref[...] = hidden(u_ref[...])

    @pl.when(i == n_main)
    def _():
        o2_ref[...] = hidden(u2_ref[...])


def _ffn_hidden(u, u2, w_gate, w_up, layer):
    m, k = u.shape
    m2 = u2.shape[0]
    n = w_gate.shape[2]
    tm = _pick(m, 1024)
    tn = _pick(n, 512)
    n_main = m // tm
    w_spec = pl.BlockSpec((None, k, tn), lambda j, i: (layer, 0, j))
    return pl.pallas_call(
        functools.partial(_ffn_kernel, n_main=n_main),
        grid=(n // tn, n_main + 1),
        in_specs=[pl.BlockSpec((tm, k), lambda j, i: (jnp.minimum(i, n_main - 1), 0)),
                  pl.BlockSpec((m2, k), lambda j, i: (0, 0)), w_spec, w_spec],
        out_specs=[pl.BlockSpec((tm, tn), lambda j, i: (jnp.minimum(i, n_main - 1), j)),
                   pl.BlockSpec((m2, tn), lambda j, i: (0, j))],
        out_shape=[jax.ShapeDtypeStruct((m, n), BF16), jax.ShapeDtypeStruct((m2, n), BF16)],
        scratch_shapes=[pltpu.VMEM((k, tn), BF16), pltpu.VMEM((k, tn), BF16)],
        compiler_params=_cparams("parallel", "arbitrary"),
        name="ffn_hidden",
    )(u, u2, w_gate, w_up)


def kernel(x_prompt, x_sample, cache_k, cache_v, state_ssm, state_conv, state_pool, page_table, c_prompt, c_sample, w_ada, b_ada, g_pre_mix, g_post_mix, g_pre_ffn, g_post_ffn, w_in, w_pool_grp, pool_scale, conv_w, conv_b, dt_bias, a_log, d_skip, ssd_norm, w_pool_br, w_ssd_br, w_attn_br, w_out, w_gate, w_up, w_down):
    bp, tp, d = x_prompt.shape
    bs, ts, _ = x_sample.shape
    depth = w_ada.shape[0]
    hd = d // ATTN_HEADS
    assert hd == LANES and ATTN_KV_HEADS * ATTN_GROUP * ts == LANES
    inner = ssd_norm.shape[1]
    n_heads = inner // SSD_HEADDIM
    conv_dim = conv_w.shape[2]
    attn_w = ATTN_HEADS * hd
    kv_w = ATTN_KV_HEADS * hd
    page = cache_k.shape[2]
    n_pages = page_table.shape[1]
    past = n_pages * page
    assert tp % MOBA_BLOCK == 0 and past % MOBA_BLOCK == 0 and n_heads <= LANES

    w_t = jnp.transpose(w_in, (0, 2, 1))
    col_z = d
    col_xbc = col_z + inner
    col_q = col_xbc + conv_dim
    col_k = col_q + attn_w
    col_v = col_k + kv_w
    col_gate = col_v + kv_w
    n_main = w_in.shape[2] - n_heads

    def project(u_p, u_s, layer):
        proj_p, proj_s = _matmul_wt(u_p, u_s, w_t, layer, 0, n_main, col_q, n_heads)
        dt_p, dt_s = _matmul_wt(u_p, u_s, w_t, layer, col_q, LANES, LANES, 0)
        return proj_p, proj_s, dt_p, dt_s

    w_grp16 = w_pool_grp.astype(BF16)
    w_pool16, w_ssd16, w_attn16 = w_pool_br.astype(BF16), w_ssd_br.astype(BF16), w_attn_br.astype(BF16)
    w_out16, w_down16 = w_out.astype(BF16), w_down.astype(BF16)

    ssd_par = _ssd_params(conv_w, conv_b, dt_bias, a_log, d_skip, ssd_norm)

    n_c = bp + bs
    c_all = jnp.pad(jnp.concatenate([c_prompt, c_sample], axis=0), ((0, -n_c % 8), (0, 0)))
    mod = _ada(c_all, w_ada, b_ada)

    def mods(layer, which):
        m = mod[layer, :, which * d:(which + 1) * d]
        m_s = jnp.broadcast_to(m[bp:n_c, None, :], (bs, ts, d)).reshape(1, bs * ts, d)
        return m[:bp, None, :], m_s

    hg_w = (n_heads // SSD_GROUPS) * SSD_HEADDIM
    ts_pad = SSD_CHUNK
    pps = _pick(n_pages, 16)
    eye_kv = jnp.eye(ATTN_KV_HEADS, dtype=F32)

    xp, xs_ = x_prompt, x_sample.reshape(1, bs * ts, d)
    sh1 = mods(0, 0)
    sc1 = mods(0, 1)
    up = _norm_mod(xp, g_pre_mix[0:1], sc1[0], sh1[0])
    us = _norm_mod(xs_, g_pre_mix[0:1], sc1[1], sh1[1])

    outs = {n: [] for n in ("kp", "vp", "hp", "cp", "pp", "ks", "vs", "hs", "cs", "ps")}
    for layer in range(depth):
        gt1, sh2, sc2, gt2 = mods(layer, 2), mods(layer, 3), mods(layer, 4), mods(layer, 5)
        nxt = (layer + 1) % depth
        sh1n, sc1n = mods(nxt, 0), mods(nxt, 1)

        proj2, proj2_s, dt_p, dt_s = project(up.reshape(bp * tp, d), us.reshape(bs * ts, d), layer)

        proj = proj2.reshape(bp, tp, -1)
        k_p, v_p = proj[:, :, col_k:col_v], proj[:, :, col_v:col_gate]
        pool_o = _pool(proj, jnp.zeros((bp, POOL_HIST, d), F32), w_grp16, pool_scale[layer:layer + 1], layer, 0)
        ssd_o, h_p = _ssd(proj, dt_p.reshape(bp, tp, LANES), jnp.zeros((bp, CONV_HIST, conv_dim), F32),
                          jnp.zeros((bp, SSD_GROUPS, hg_w, SSD_STATE), F32), ssd_par, layer, col_z, col_xbc, tp)
        attn_o = _moba_prompt(proj, col_q, col_k, col_v)
        merged = _merge(pool_o.reshape(bp * tp, d), ssd_o.reshape(bp * tp, inner), attn_o.reshape(bp * tp, attn_w),
                        w_pool16, w_ssd16, w_attn16, proj2, col_gate, layer)
        xp, u_ffn_p = _mm_res(merged.reshape(bp, tp, d), w_out16, layer, xp, gt1[0], g_post_mix[layer:layer + 1],
                              g_pre_ffn[layer:layer + 1], sc2[0], sh2[0])
        outs["kp"].append(k_p.reshape(bp, tp, ATTN_KV_HEADS, hd))
        outs["vp"].append(v_p.reshape(bp, tp, ATTN_KV_HEADS, hd))
        outs["hp"].append(h_p.reshape(bp, n_heads, SSD_HEADDIM, SSD_STATE))
        outs["cp"].append(proj[:, tp - (SSD_CONV - 1):, col_xbc:col_xbc + conv_dim])
        outs["pp"].append(proj[:, tp - (POOL_HIST - 1):, :d])

        proj2 = proj2_s
        proj = proj2.reshape(bs, ts, -1)
        k_s, v_s, dt_s = proj[:, :, col_k:col_v], proj[:, :, col_v:col_gate], dt_s.reshape(bs, ts, LANES)
        pool_buf = jnp.pad(state_pool[layer], ((0, 0), (1, 0), (0, 0)))
        pool_o = _pool(proj, pool_buf, w_grp16, pool_scale[layer:layer + 1], layer, past)
        tpad = ((0, 0), (0, ts_pad - ts), (0, 0))
        conv_prev = jnp.pad(state_conv[layer], ((0, 0), (CONV_HIST - (SSD_CONV - 1), 0), (0, 0)))
        ssd_o, h_s = _ssd(jnp.pad(proj[:, :, :col_q], tpad), jnp.pad(dt_s, tpad), conv_prev,
                          state_ssm[layer].reshape(bs, SSD_GROUPS, hg_w, SSD_STATE), ssd_par, layer,
                          col_z, col_xbc, ts)
        ssd_o = ssd_o[:, :ts]
        q5 = proj[:, :, col_q:col_k].reshape(bs, ts, ATTN_KV_HEADS, ATTN_GROUP, hd)
        wq_t = jnp.einsum("btkgd,kj->bjdkgt", q5, eye_kv).reshape(bs, kv_w, LANES).astype(BF16)
        cache_k4 = cache_k.reshape(depth, cache_k.shape[1], page * ATTN_KV_HEADS, hd)
        cache_v4 = cache_v.reshape(depth, cache_v.shape[1], page * ATTN_KV_HEADS, hd)
        scores = _moba_scores(cache_k4, layer, page_table, wq_t, pps)
        probs = _moba_softmax(scores, k_s, wq_t, ts)
        o_full = _moba_pv(cache_v4, layer, page_table, probs, jnp.pad(v_s, ((0, 0), (0, LANES - ts), (0, 0))), pps)
        o6 = o_full.reshape(bs, ATTN_KV_HEADS, ATTN_GROUP, ts, ATTN_KV_HEADS, hd)
        attn_o = jnp.einsum("bkgtjd,kj->btkgd", o6, eye_kv).reshape(bs * ts, attn_w).astype(BF16)
        merged = _merge(pool_o.reshape(bs * ts, d), ssd_o.reshape(bs * ts, inner), attn_o,
                        w_pool16, w_ssd16, w_attn16, proj2, col_gate, layer)
        xs_, u_ffn_s = _mm_res(merged.reshape(1, bs * ts, d), w_out16, layer, xs_, gt1[1], g_post_mix[layer:layer + 1],
                               g_pre_ffn[layer:layer + 1], sc2[1], sh2[1])

        hid_p, hid_s = _ffn_hidden(u_ffn_p.reshape(bp * tp, d), u_ffn_s.reshape(bs * ts, d), w_gate, w_up, layer)
        xp, up = _mm_res(hid_p.reshape(bp, tp, -1), w_down16, layer, xp, gt2[0], g_post_ffn[layer:layer + 1],
                         g_pre_mix[nxt:nxt + 1], sc1n[0], sh1n[0])
        xs_, us = _mm_res(hid_s.reshape(1, bs * ts, -1), w_down16, layer, xs_, gt2[1], g_post_ffn[layer:layer + 1],
                          g_pre_mix[nxt:nxt + 1], sc1n[1], sh1n[1])
        outs["ks"].append(k_s.reshape(bs, ts, ATTN_KV_HEADS, hd))
        outs["vs"].append(v_s.reshape(bs, ts, ATTN_KV_HEADS, hd))
        outs["hs"].append(h_s.reshape(bs, n_heads, SSD_HEADDIM, SSD_STATE))
        conv_ext = jnp.concatenate([state_conv[layer], proj[:, :, col_xbc:col_xbc + conv_dim]], axis=1)
        outs["cs"].append(conv_ext[:, ts:])
        pool_ext = jnp.concatenate([state_pool[layer], proj[:, :, :d]], axis=1)
        outs["ps"].append(pool_ext[:, ts:])

    st = {n: jnp.stack(v) for n, v in outs.items()}
    return (xp, xs_.reshape(bs, ts, d), st["kp"], st["vp"], st["hp"], st["cp"], st["pp"],
            st["ks"], st["vs"], st["hs"], st["cs"], st["ps"])
```

```python
import functools

import jax
import jax.numpy as jnp
from jax import lax
from jax.experimental import pallas as pl
from jax.experimental.pallas import tpu as pltpu

F32 = jnp.float32
BF16 = jnp.bfloat16

NORM_EPS = 1e-6
POOL_WINDOWS = (2, 4, 8, 16)
POOL_HIST = 16
SSD_HEADDIM = 64
SSD_GROUPS = 8
SSD_STATE = 128
SSD_CONV = 4
SSD_CHUNK = 128
SSD_GROUPS_PER_STEP = 4
CONV_HIST = 8
ATTN_HEADS = 16
ATTN_KV_HEADS = 8
ATTN_GROUP = ATTN_HEADS // ATTN_KV_HEADS
MOBA_BLOCK = 256
MOBA_TOPK = 3
PAST_GROUP = 2
LANES = 128
VMEM_LIMIT_BYTES = 56 * 1024 * 1024
NEG_INF = float("-inf")
LOG2E = 1.4426950408889634


def _cparams(*sem):
    return pltpu.CompilerParams(dimension_semantics=sem, vmem_limit_bytes=VMEM_LIMIT_BYTES)


def _sigmoid(x):
    return 1.0 / (1.0 + jnp.exp(-x))


def _silu(x):
    return x * _sigmoid(x)


def _dot(a, b):
    return jnp.dot(a, b, preferred_element_type=F32)


def _dot_nt(a, b):
    return lax.dot_general(a, b, (((1,), (1,)), ((), ())), preferred_element_type=F32)


def _dot_tn(a, b):
    return lax.dot_general(a, b, (((0,), (0,)), ((), ())), preferred_element_type=F32)


def _rms(x):
    return x * lax.rsqrt(jnp.mean(x * x, axis=-1, keepdims=True) + NORM_EPS)


def _pick(total, pref):
    if total <= pref:
        return total
    t = pref
    while total % t:
        t //= 2
    return t


def _ada_kernel(c_ref, w_ref, b_ref, o_ref):
    a = _silu(c_ref[...]).astype(BF16)
    o_ref[...] = _dot(a, w_ref[...].astype(BF16)) + b_ref[...]


def _ada(c_all, w_ada, b_ada):
    depth, d, n = w_ada.shape
    rows = c_all.shape[0]
    tn = _pick(n, 1024)
    return pl.pallas_call(
        _ada_kernel,
        grid=(depth, n // tn),
        in_specs=[pl.BlockSpec((rows, d), lambda l, j: (0, 0)),
                  pl.BlockSpec((None, d, tn), lambda l, j: (l, 0, j)),
                  pl.BlockSpec((None, 1, tn), lambda l, j: (l, 0, j))],
        out_specs=pl.BlockSpec((None, rows, tn), lambda l, j: (l, 0, j)),
        out_shape=jax.ShapeDtypeStruct((depth, rows, n), F32),
        compiler_params=_cparams("parallel", "parallel"),
        name="ada_mod",
    )(c_all, w_ada, b_ada.reshape(depth, 1, n))


def _norm_mod_kernel(x_ref, g_ref, sc_ref, sh_ref, u_ref):
    y = _rms(x_ref[...]) * g_ref[...]
    u_ref[...] = (y * (1.0 + sc_ref[...]) + sh_ref[...]).astype(BF16)


def _norm_mod(x, g, sc, sh):
    b, t, d = x.shape
    r = sc.shape[1]
    tt = _pick(t, 512)
    mod_spec = pl.BlockSpec((None, r if r == 1 else tt, d),
                            (lambda i, j: (i, j, 0)) if r == t else (lambda i, j: (i, 0, 0)))
    return pl.pallas_call(
        _norm_mod_kernel,
        grid=(b, t // tt),
        in_specs=[pl.BlockSpec((None, tt, d), lambda i, j: (i, j, 0)),
                  pl.BlockSpec((1, d), lambda i, j: (0, 0)),
                  mod_spec, mod_spec],
        out_specs=pl.BlockSpec((None, tt, d), lambda i, j: (i, j, 0)),
        out_shape=jax.ShapeDtypeStruct((b, t, d), BF16),
        compiler_params=_cparams("parallel", "parallel"),
        name="norm_mod",
    )(x, g, sc, sh)


def _mm_kernel(a_ref, w_ref, o_ref, w16_ref):
    @pl.when(pl.program_id(1) == 0)
    def _():
        w16_ref[...] = w_ref[0].astype(BF16)

    o_ref[...] = _dot_nt(a_ref[...], w16_ref[...])


def _matmul_wt(a, w_t, layer, row0, n, skip_at, skip):
    m, k = a.shape
    tm = _pick(m, 1024)
    tn = _pick(n, 1024)
    assert skip_at % tn == 0
    hole = skip_at // tn

    def w_index(j, i):
        return layer, pl.multiple_of(row0 + j * tn + jnp.where(j >= hole, skip, 0), 8), 0

    return pl.pallas_call(
        _mm_kernel,
        grid=(n // tn, m // tm),
        in_specs=[pl.BlockSpec((tm, k), lambda j, i: (i, 0)),
                  pl.BlockSpec((pl.Element(1), pl.Element(tn), pl.Element(k)), w_index)],
        out_specs=pl.BlockSpec((tm, tn), lambda j, i: (i, j)),
        out_shape=jax.ShapeDtypeStruct((m, n), F32),
        scratch_shapes=[pltpu.VMEM((tn, k), BF16)],
        compiler_params=_cparams("parallel", "arbitrary"),
        name="proj_matmul",
    )(a, w_t)


def _pool_kernel(p_ref, buf_ref, w_ref, sc_ref, o_ref, ext_ref, *, tt, start, pg):
    t = pl.program_id(1)

    @pl.when(t == 0)
    def _():
        ext_ref[0:POOL_HIST, :] = buf_ref[...]

    ext_ref[POOL_HIST:POOL_HIST + tt, :] = p_ref[...]
    n_valid = lax.broadcasted_iota(jnp.int32, (tt, 1), 0) + (t * tt + start + 1)
    for gi, w in enumerate(POOL_WINDOWS):
        c0 = gi * pg
        e = ext_ref[:, c0:c0 + pg]
        x = e[POOL_HIST:]
        s = e
        k = 1
        while k < w:
            s = s + pltpu.roll(s, k, axis=0)
            k *= 2
        s = s[POOL_HIST:]
        cnt = jnp.minimum(n_valid, w).astype(F32)
        mixed = (s / cnt - x).astype(BF16)
        y = _dot(mixed, w_ref[gi]) * sc_ref[:, c0:c0 + pg]
        o_ref[:, c0:c0 + pg] = y.astype(BF16)
    ext_ref[0:POOL_HIST, :] = ext_ref[tt:tt + POOL_HIST, :]


def _pool(proj3, buf, w_grp, scale, layer, start):
    b, t, _ = proj3.shape
    c = scale.shape[1]
    pg = c // len(POOL_WINDOWS)
    tt = _pick(t, 256)
    return pl.pallas_call(
        functools.partial(_pool_kernel, tt=tt, start=start, pg=pg),
        grid=(b, t // tt),
        in_specs=[pl.BlockSpec((None, tt, c), lambda i, j: (i, j, 0)),
                  pl.BlockSpec((None, POOL_HIST, c), lambda i, j: (i, 0, 0)),
                  pl.BlockSpec((None, len(POOL_WINDOWS), pg, pg), lambda i, j: (layer, 0, 0, 0)),
                  pl.BlockSpec((1, c), lambda i, j: (0, 0))],
        out_specs=pl.BlockSpec((None, tt, c), lambda i, j: (i, j, 0)),
        out_shape=jax.ShapeDtypeStruct((b, t, c), BF16),
        scratch_shapes=[pltpu.VMEM((POOL_HIST + tt, c), F32)],
        compiler_params=_cparams("parallel", "arbitrary"),
        name="pool_mixer",
    )(proj3, buf, w_grp, scale)


def _ssd_kernel(z_ref, x_ref, b_ref, c_ref, dt_ref, cpx_ref, cpb_ref, cpc_ref,
                px_ref, pb_ref, pc_ref, pdt_ref, exp_ref, h0_ref,
                y_ref, hout_ref, histx, histb, histc, ht, *, t_valid, n_chunks, hg, gps):
    gq = pl.program_id(1)
    c = pl.program_id(2)
    lc = SSD_CHUNK
    gw = hg * SSD_HEADDIM
    ns = SSD_STATE

    @pl.when(c == 0)
    def _():
        histx[...] = cpx_ref[...]
        histb[...] = cpb_ref[...]
        histc[...] = cpc_ref[...]
        for s in range(gps):
            ht[s] = h0_ref[s].T

    def conv(hist, raw_ref, par_ref):
        x = raw_ref[...]
        prev = hist[...]
        hist[...] = x[lc - CONV_HIST:, :]
        row8 = lax.broadcasted_iota(jnp.int32, prev.shape, 0)
        acc = par_ref[SSD_CONV:SSD_CONV + 1, :] + x * par_ref[SSD_CONV - 1:SSD_CONV, :]
        for k in range(1, SSD_CONV):
            r = pltpu.roll(x, k, axis=0)
            first = jnp.where(row8 < k, pltpu.roll(prev, k, axis=0), r[0:CONV_HIST])
            shifted = jnp.concatenate([first, r[CONV_HIST:]], axis=0)
            acc = acc + shifted * par_ref[SSD_CONV - 1 - k:SSD_CONV - k, :]
        return _silu(acc)

    xs_all = conv(histx, x_ref, px_ref)
    bm_all = conv(histb, b_ref, pb_ref)
    cm_all = conv(histc, c_ref, pc_ref)

    row = lax.broadcasted_iota(jnp.int32, (lc, LANES), 0)
    col = lax.broadcasted_iota(jnp.int32, (lc, LANES), 1)
    xdt = dt_ref[...] + pdt_ref[0:1, :]
    dt = jnp.maximum(xdt, 0.0) + jnp.log1p(jnp.exp(-jnp.abs(xdt)))
    dt = jnp.where(c * lc + row < t_valid, dt, 0.0)
    dta = dt * (-jnp.exp(pdt_ref[1:2, :]))
    acs = dta
    k = 1
    while k < lc:
        acs = acs + jnp.where(row >= k, pltpu.roll(acs, k, axis=0), 0.0)
        k *= 2
    a_last = acs[lc - 1:lc, :]
    w_end = dt * jnp.exp(a_last - acs)
    exp_a = jnp.exp(acs)

    cd = jnp.exp(a_last)
    cd_hi = cd.astype(BF16).astype(F32)
    cd_mid = (cd - cd_hi).astype(BF16).astype(F32)
    cd_lo = cd - cd_hi - cd_mid
    pieces = [jnp.broadcast_to(p, (16, LANES)) for p in (cd_hi, cd_mid, cd_lo)]
    stacked = jnp.concatenate([dt, w_end, exp_a] + pieces, axis=0).astype(BF16)
    acs2 = acs * LOG2E
    causal = row >= col
    heads_here = LANES // SSD_HEADDIM

    for s in range(gps):
        xl, nl = slice(s * gw, (s + 1) * gw), slice(s * ns, (s + 1) * ns)
        xs, bm, cm = xs_all[:, xl], bm_all[:, nl], cm_all[:, nl]
        wide = _dot(stacked, exp_ref[s])
        dt_e, w_e, ea_e = wide[0:lc], wide[lc:2 * lc], wide[2 * lc:3 * lc]
        r0 = 3 * lc
        cd_e = wide[r0:r0 + 1] + wide[r0 + 16:r0 + 17] + wide[r0 + 32:r0 + 33]

        acs_g = pltpu.roll(acs2, (LANES - (gq * gps + s) * hg) % LANES, axis=1)
        acs_gt = acs_g.T

        bm16 = bm.astype(BF16)
        cm16 = cm.astype(BF16)
        cb = jnp.where(causal, _dot_nt(cm16, bm16), 0.0)
        xdt_e = xs * dt_e
        y_parts = []
        for pr in range(gw // LANES):
            xp = xdt_e[:, pr * LANES:(pr + 1) * LANES]
            lhs, rhs = [], []
            for hh in range(heads_here):
                h = pr * heads_here + hh
                seg = acs_g[:, h:h + 1] - acs_gt[h:h + 1, :]
                lhs.append((cb * jnp.exp2(jnp.minimum(seg, 0.0))).astype(BF16))
                in_head = (col >= hh * SSD_HEADDIM) & (col < (hh + 1) * SSD_HEADDIM)
                rhs.append(jnp.where(in_head, xp, 0.0).astype(BF16))
            y_parts.append(_dot(jnp.concatenate(lhs, axis=1), jnp.concatenate(rhs, axis=0)))
        y = jnp.concatenate(y_parts, axis=1)

        h_in = ht[s]
        y = y + _dot(cm16, h_in.astype(BF16)) * ea_e
        h_new = h_in * cd_e + _dot(bm.T.astype(BF16), (xs * w_e).astype(BF16))
        ht[s] = h_new

        y = y + px_ref[SSD_CONV + 1:SSD_CONV + 2, xl] * xs
        y = y * _silu(z_ref[:, xl])
        y_ref[:, xl] = (_rms(y) * px_ref[SSD_CONV + 2:SSD_CONV + 3, xl]).astype(BF16)

    @pl.when(c == n_chunks - 1)
    def _():
        for s in range(gps):
            hout_ref[s] = ht[s].T


def _ssd_params(conv_w, conv_b, dt_bias, a_log, d_skip, ssd_norm):
    depth, inner = ssd_norm.shape
    n_heads = dt_bias.shape[1]
    rows = lambda v: v[:, None, :]
    px = jnp.concatenate([conv_w[:, :, :inner], rows(conv_b[:, :inner]), rows(jnp.repeat(d_skip, SSD_HEADDIM, axis=1)),
                          rows(ssd_norm), jnp.zeros((depth, 8 - SSD_CONV - 3, inner), F32)], axis=1)
    pbc = jnp.concatenate([conv_w[:, :, inner:], rows(conv_b[:, inner:]),
                           jnp.zeros((depth, 8 - SSD_CONV - 1, conv_w.shape[2] - inner), F32)], axis=1)
    pdt = jnp.pad(jnp.stack([dt_bias, a_log], axis=1), ((0, 0), (0, 6), (0, LANES - n_heads)))
    return px, pbc, pdt


def _ssd(proj3, dt3, conv_prev, h0, params, layer, col_z, col_xbc, t_valid):
    b, tp, _ = proj3.shape
    px, pbc, pdt = params
    inner = px.shape[2]
    n_heads = inner // SSD_HEADDIM
    hg = n_heads // SSD_GROUPS
    gw = hg * SSD_HEADDIM
    ns = SSD_STATE
    lc = SSD_CHUNK
    n_chunks = tp // lc
    gps = SSD_GROUPS_PER_STEP
    xw, nw = gps * gw, gps * ns
    assert col_z % xw == 0 and col_xbc % xw == 0 and (col_xbc + inner) % nw == 0 and SSD_GROUPS % gps == 0
    zb, xb = col_z // xw, col_xbc // xw
    bb = (col_xbc + inner) // nw
    cb_ = (col_xbc + inner + SSD_GROUPS * ns) // nw
    cxb, cbb, ccb = 0, inner // nw, (inner + SSD_GROUPS * ns) // nw
    head_of = (jnp.arange(SSD_GROUPS)[:, None] * hg + jnp.arange(gw)[None, :] // SSD_HEADDIM)
    expand = (jnp.arange(LANES)[None, :, None] == head_of[:, None, :]).astype(BF16)

    def seq(width, off):
        return pl.BlockSpec((None, lc, width), lambda i, g, c: (i, c, off + g))

    def prev(width, off):
        return pl.BlockSpec((None, CONV_HIST, width), lambda i, g, c: (i, 0, off + g))

    def par(width, off):
        return pl.BlockSpec((None, 8, width), lambda i, g, c: (layer, 0, off + g))

    state_spec = pl.BlockSpec((None, gps, gw, ns), lambda i, g, c: (i, g, 0, 0))
    kernel = functools.partial(_ssd_kernel, t_valid=t_valid, n_chunks=n_chunks, hg=hg, gps=gps)
    return pl.pallas_call(
        kernel,
        grid=(b, SSD_GROUPS // gps, n_chunks),
        in_specs=[seq(xw, zb), seq(xw, xb), seq(nw, bb), seq(nw, cb_),
                  pl.BlockSpec((None, lc, LANES), lambda i, g, c: (i, c, 0)),
                  prev(xw, cxb), prev(nw, cbb), prev(nw, ccb),
                  par(xw, 0), par(nw, 0), par(nw, SSD_GROUPS // gps),
                  pl.BlockSpec((None, 8, LANES), lambda i, g, c: (layer, 0, 0)),
                  pl.BlockSpec((gps, LANES, gw), lambda i, g, c: (g, 0, 0)),
                  state_spec],
        out_specs=[pl.BlockSpec((None, lc, xw), lambda i, g, c: (i, c, g)), state_spec],
        out_shape=[jax.ShapeDtypeStruct((b, tp, inner), BF16),
                   jax.ShapeDtypeStruct((b, SSD_GROUPS, gw, ns), F32)],
        scratch_shapes=[pltpu.VMEM((CONV_HIST, xw), F32),
                        pltpu.VMEM((CONV_HIST, nw), F32),
                        pltpu.VMEM((CONV_HIST, nw), F32),
                        pltpu.VMEM((gps, ns, gw), F32)],
        compiler_params=_cparams("parallel", "parallel", "arbitrary"),
        name="ssd_mixer",
    )(proj3, proj3, proj3, proj3, dt3, conv_prev, conv_prev, conv_prev, px, pbc, pbc, pdt, expand, h0)


def _top_k_mask(gate, valid, ids, axis):
    big = jnp.int32(2 ** 30)
    remaining = valid
    sel = jnp.zeros(gate.shape, F32)
    for _ in range(MOBA_TOPK):
        gm = jnp.where(remaining > 0.5, gate, NEG_INF)
        m = jnp.max(gm, axis=axis, keepdims=True)
        cand = jnp.where(gm == m, remaining, 0.0)
        first = jnp.min(jnp.where(cand > 0.5, ids, big), axis=axis, keepdims=True)
        pick = jnp.where(ids == first, 1.0, 0.0)
        sel = sel + pick
        remaining = remaining - pick
    return sel


def _moba_prompt_kernel(q_ref, k_ref, v_ref, o_ref, kb, vtb, km, sel_ref, qs_ref, acc_ref, ml_ref, sa_ref, sb_ref,
                        *, n_blocks, scale):
    i = pl.program_id(2)
    blk = MOBA_BLOCK
    hd = LANES
    nq = ATTN_GROUP * blk
    c2 = scale * LOG2E

    @pl.when(i == 0)
    def _():
        km[...] = jnp.zeros(km.shape, F32)
        for j in range(n_blocks):
            kj = k_ref[j * blk:(j + 1) * blk, :]
            kb[j] = kj.astype(BF16)
            km[j:j + 1, :] = jnp.mean(kj, axis=0, keepdims=True)
            vtb[j] = v_ref[j * blk:(j + 1) * blk, :].T.astype(BF16)

    grp = PAST_GROUP

    def block_at(pos):
        return jnp.where(pos == 0, i, jnp.minimum(pos - 1, n_blocks - 1))

    def scores(s_ref, p0, own_first=False):
        for k in range(grp):
            s = _dot_nt(kb[block_at(p0 + k)], qs_ref[...])
            if own_first and k == 0:
                key_r = lax.broadcasted_iota(jnp.int32, (blk, nq), 0)
                qry_c = lax.broadcasted_iota(jnp.int32, (blk, nq), 1) & (blk - 1)
                s = jnp.where(key_r <= qry_c, s, NEG_INF)
            s_ref[k] = s

    q = q_ref[...]
    qs_ref[...] = jnp.concatenate([q[:, g * hd:(g + 1) * hd] for g in range(ATTN_GROUP)], axis=0).astype(BF16)
    scores(sa_ref, 0, own_first=True)
    gate_t = _dot_nt(km[...].astype(BF16), qs_ref[...])
    ids = lax.broadcasted_iota(jnp.int32, (km.shape[0], nq), 0)
    valid = jnp.where(ids < i, 1.0, 0.0)
    sel_ref[...] = _top_k_mask(gate_t, valid, ids, axis=0)
    ml_ref[0:1, :] = jnp.full((1, nq), NEG_INF, F32)
    ml_ref[1:2, :] = jnp.zeros((1, nq), F32)
    acc_ref[...] = jnp.zeros(acc_ref.shape, F32)

    def update(s_ref, p0):
        m_old, l_old = ml_ref[0:1, :], ml_ref[1:2, :]
        picked = [sel_ref[pl.ds(jnp.maximum(p0 + k - 1, 0), 1), :] + jnp.where(p0 + k == 0, 1.0, 0.0) > 0.5
                  for k in range(grp)]
        m_new = m_old
        for k in range(grp):
            m_new = jnp.maximum(m_new, jnp.where(picked[k], jnp.max(s_ref[k], axis=0, keepdims=True), NEG_INF))
        alpha = jnp.exp2((m_old - m_new) * c2)
        l_new = alpha * l_old
        pv = None
        for k in range(grp):
            p = jnp.exp2((s_ref[k] - jnp.where(picked[k], m_new, jnp.inf)) * c2)
            l_new = l_new + jnp.sum(p, axis=0, keepdims=True)
            d = _dot(vtb[block_at(p0 + k)], p.astype(BF16))
            pv = d if pv is None else pv + d
        acc_ref[...] = alpha * acc_ref[...] + pv
        ml_ref[0:1, :] = m_new
        ml_ref[1:2, :] = l_new

    n_groups = (i + grp) // grp

    def pair_body(u, carry):
        p0 = 2 * u * grp
        scores(sb_ref, p0 + grp)
        update(sa_ref, p0)
        scores(sa_ref, p0 + 2 * grp)
        update(sb_ref, p0 + grp)
        return carry

    lax.fori_loop(0, n_groups // 2, pair_body, 0)

    @pl.when((n_groups & 1) != 0)
    def _():
        update(sa_ref, (n_groups - 1) * grp)

    o = (acc_ref[...] / ml_ref[1:2, :]).T
    o_ref[...] = jnp.concatenate([o[g * blk:(g + 1) * blk] for g in range(ATTN_GROUP)], axis=1).astype(BF16)


def _moba_prompt(proj3, col_q, col_k, col_v):
    b, t, _ = proj3.shape
    hd = LANES
    blk = MOBA_BLOCK
    n_blocks = t // blk
    nb_pad = -(-(n_blocks + PAST_GROUP) // 16) * 16
    qw = ATTN_GROUP * hd
    qb, kb0, vb0 = col_q // qw, col_k // hd, col_v // hd
    kernel = functools.partial(_moba_prompt_kernel, n_blocks=n_blocks, scale=hd ** -0.5)
    return pl.pallas_call(
        kernel,
        grid=(b, ATTN_KV_HEADS, n_blocks),
        in_specs=[pl.BlockSpec((None, blk, qw), lambda bi, h, i: (bi, i, qb + h)),
                  pl.BlockSpec((None, t, hd), lambda bi, h, i: (bi, 0, kb0 + h)),
                  pl.BlockSpec((None, t, hd), lambda bi, h, i: (bi, 0, vb0 + h))],
        out_specs=pl.BlockSpec((None, blk, qw), lambda bi, h, i: (bi, i, h)),
        out_shape=jax.ShapeDtypeStruct((b, t, ATTN_HEADS * hd), BF16),
        scratch_shapes=[pltpu.VMEM((n_blocks, blk, hd), BF16), pltpu.VMEM((n_blocks, hd, blk), BF16),
                        pltpu.VMEM((nb_pad, hd), F32), pltpu.VMEM((nb_pad, ATTN_GROUP * blk), F32),
                        pltpu.VMEM((ATTN_GROUP * blk, hd), BF16), pltpu.VMEM((hd, ATTN_GROUP * blk), F32),
                        pltpu.VMEM((8, ATTN_GROUP * blk), F32),
                        pltpu.VMEM((PAST_GROUP, blk, ATTN_GROUP * blk), F32),
                        pltpu.VMEM((PAST_GROUP, blk, ATTN_GROUP * blk), F32)],
        compiler_params=_cparams("parallel", "parallel", "arbitrary"),
        name="moba_prompt",
    )(proj3, proj3, proj3)


def _moba_scores_kernel(pt_ref, wq_ref, *refs, pps, page):
    del pt_ref
    k_refs, o_ref = refs[:pps], refs[pps]
    hd = LANES
    for r in range(pps):
        acc = None
        for h in range(ATTN_KV_HEADS):
            kh = k_refs[r][pl.ds(h, page, stride=ATTN_KV_HEADS), :].astype(BF16)
            part = _dot(kh, wq_ref[h * hd:(h + 1) * hd, :])
            acc = part if acc is None else acc + part
        o_ref[r * page:(r + 1) * page, :] = acc


def _moba_scores(cache4, layer, page_table, wq_t, pps):
    b, n_pages = page_table.shape
    rows, hd = cache4.shape[2], cache4.shape[3]
    page, width = rows // ATTN_KV_HEADS, hd * ATTN_KV_HEADS

    def page_spec(r):
        return pl.BlockSpec((None, None, rows, hd), lambda bi, p, pt: (layer, pt[bi, p * pps + r], 0, 0))

    grid_spec = pltpu.PrefetchScalarGridSpec(
        num_scalar_prefetch=1,
        grid=(b, n_pages // pps),
        in_specs=[pl.BlockSpec((None, width, LANES), lambda bi, p, pt: (bi, 0, 0))]
                 + [page_spec(r) for r in range(pps)],
        out_specs=pl.BlockSpec((None, pps * page, LANES), lambda bi, p, pt: (bi, p, 0)),
    )
    return pl.pallas_call(
        functools.partial(_moba_scores_kernel, pps=pps, page=page),
        grid_spec=grid_spec,
        out_shape=jax.ShapeDtypeStruct((b, n_pages * page, LANES), F32),
        compiler_params=_cparams("parallel", "arbitrary"),
        name="moba_sample_scores",
    )(page_table, wq_t, *([cache4] * pps))


def _moba_softmax_kernel(st_ref, kn_ref, wq_ref, p_ref, gate_ref, sel_ref, *, n_blocks, tq, scale):
    blk = MOBA_BLOCK
    past = n_blocks * blk
    gate_ref[...] = jnp.zeros(gate_ref.shape, F32)

    def gate_body(j, carry):
        off = pl.multiple_of(j * blk, blk)
        gate_ref[pl.ds(j, 1), :] = jnp.mean(st_ref[pl.ds(off, blk), :], axis=0, keepdims=True)
        return carry

    lax.fori_loop(0, n_blocks, gate_body, 0)
    nrow = gate_ref.shape[0]
    ids = lax.broadcasted_iota(jnp.int32, (nrow, LANES), 0)
    valid = jnp.where(ids < n_blocks, 1.0, 0.0)
    sel_ref[...] = _top_k_mask(gate_ref[...], valid, ids, axis=0)

    s_own = _dot(kn_ref[...].astype(BF16), wq_ref[...]) * scale
    key_t = lax.broadcasted_iota(jnp.int32, (tq, LANES), 0)
    qry_t = lax.broadcasted_iota(jnp.int32, (tq, LANES), 1) & (tq - 1)
    s_own = jnp.where(key_t <= qry_t, s_own, NEG_INF)
    m0 = jnp.max(s_own, axis=0, keepdims=True)

    def max_body(j, m):
        off = pl.multiple_of(j * blk, blk)
        bm = jnp.max(st_ref[pl.ds(off, blk), :], axis=0, keepdims=True) * scale
        return jnp.maximum(m, jnp.where(sel_ref[pl.ds(j, 1), :] > 0.5, bm, NEG_INF))

    m = lax.fori_loop(0, n_blocks, max_body, m0)
    p_own = jnp.exp(s_own - m)
    l0 = jnp.sum(p_own, axis=0, keepdims=True)

    def exp_body(j, l):
        off = pl.multiple_of(j * blk, blk)
        s = st_ref[pl.ds(off, blk), :] * scale
        p = jnp.where(sel_ref[pl.ds(j, 1), :] > 0.5, jnp.exp(s - m), 0.0)
        p_ref[pl.ds(off, blk), :] = p
        return l + jnp.sum(p, axis=0, keepdims=True)

    l = lax.fori_loop(0, n_blocks, exp_body, l0)
    inv = 1.0 / l

    def norm_body(j, carry):
        off = pl.multiple_of(j * blk, blk)
        p_ref[pl.ds(off, blk), :] = p_ref[pl.ds(off, blk), :] * inv
        return carry

    lax.fori_loop(0, n_blocks, norm_body, 0)
    p_ref[past:past + LANES, :] = jnp.zeros((LANES, LANES), F32)
    p_ref[past:past + tq, :] = p_own * inv


def _moba_softmax(scores, k_new, wq_t, tq):
    b, past, _ = scores.shape
    width = k_new.shape[2]
    n_blocks = past // MOBA_BLOCK
    nrow = -(-n_blocks // 8) * 8
    kernel = functools.partial(_moba_softmax_kernel, n_blocks=n_blocks, tq=tq, scale=LANES ** -0.5)
    return pl.pallas_call(
        kernel,
        grid=(b,),
        in_specs=[pl.BlockSpec((None, past, LANES), lambda bi: (bi, 0, 0)),
                  pl.BlockSpec((None, tq, width), lambda bi: (bi, 0, 0)),
                  pl.BlockSpec((None, width, LANES), lambda bi: (bi, 0, 0))],
        out_specs=pl.BlockSpec((None, past + LANES, LANES), lambda bi: (bi, 0, 0)),
        out_shape=jax.ShapeDtypeStruct((b, past + LANES, LANES), F32),
        scratch_shapes=[pltpu.VMEM((nrow, LANES), F32), pltpu.VMEM((nrow, LANES), F32)],
        compiler_params=_cparams("parallel"),
        name="moba_sample_softmax",
    )(scores, k_new, wq_t)


def _moba_pv_kernel(pt_ref, p_ref, pown_ref, vnew_ref, *refs, pps, page):
    del pt_ref
    v_refs, o_ref = refs[:pps], refs[pps]

    @pl.when(pl.program_id(1) == 0)
    def _():
        o_ref[...] = _dot_tn(pown_ref[...].astype(BF16), vnew_ref[...].astype(BF16))

    hd = LANES
    for h in range(ATTN_KV_HEADS):
        acc = o_ref[:, h * hd:(h + 1) * hd]
        for r in range(pps):
            vh = v_refs[r][pl.ds(h, page, stride=ATTN_KV_HEADS), :].astype(BF16)
            acc = acc + _dot_tn(p_ref[r * page:(r + 1) * page, :].astype(BF16), vh)
        o_ref[:, h * hd:(h + 1) * hd] = acc


def _moba_pv(cache4, layer, page_table, probs, v_new_pad, pps):
    b, n_pages = page_table.shape
    rows, hd = cache4.shape[2], cache4.shape[3]
    page, width = rows // ATTN_KV_HEADS, hd * ATTN_KV_HEADS
    own_blk = (n_pages * page) // LANES

    def page_spec(r):
        return pl.BlockSpec((None, None, rows, hd), lambda bi, p, pt: (layer, pt[bi, p * pps + r], 0, 0))

    grid_spec = pltpu.PrefetchScalarGridSpec(
        num_scalar_prefetch=1,
        grid=(b, n_pages // pps),
        in_specs=[pl.BlockSpec((None, pps * page, LANES), lambda bi, p, pt: (bi, p, 0)),
                  pl.BlockSpec((None, LANES, LANES), lambda bi, p, pt: (bi, own_blk, 0)),
                  pl.BlockSpec((None, LANES, width), lambda bi, p, pt: (bi, 0, 0))]
                 + [page_spec(r) for r in range(pps)],
        out_specs=pl.BlockSpec((None, LANES, width), lambda bi, p, pt: (bi, 0, 0)),
    )
    return pl.pallas_call(
        functools.partial(_moba_pv_kernel, pps=pps, page=page),
        grid_spec=grid_spec,
        out_shape=jax.ShapeDtypeStruct((b, LANES, width), F32),
        compiler_params=_cparams("parallel", "arbitrary"),
        name="moba_sample_pv",
    )(page_table, probs, probs, v_new_pad, *([cache4] * pps))


def _merge_kernel(ap_ref, as_ref, aa_ref, wp_ref, ws_ref, wa_ref, g0_ref, g1_ref, g2_ref, o_ref):
    acc = _sigmoid(g0_ref[...]) * _dot(ap_ref[...], wp_ref[...])
    acc = acc + _sigmoid(g1_ref[...]) * _dot(as_ref[...], ws_ref[...])
    acc = acc + _sigmoid(g2_ref[...]) * _dot(aa_ref[...], wa_ref[...])
    o_ref[...] = acc.astype(BF16)


def _merge(pool_o, ssd_o, attn_o, w_pool, w_ssd, w_attn, proj, col_gate, layer):
    m, d = pool_o.shape
    tm = _pick(m, 512)
    tn = _pick(d, 512)
    gb = col_gate // tn
    nb = d // tn

    def act(width):
        return pl.BlockSpec((tm, width), lambda j, i: (i, 0))

    def wgt(kdim):
        return pl.BlockSpec((None, kdim, tn), lambda j, i: (layer, 0, j))

    def gate(which):
        return pl.BlockSpec((tm, tn), lambda j, i: (i, gb + which * nb + j))

    return pl.pallas_call(
        _merge_kernel,
        grid=(d // tn, m // tm),
        in_specs=[act(pool_o.shape[1]), act(ssd_o.shape[1]), act(attn_o.shape[1]),
                  wgt(w_pool.shape[1]), wgt(w_ssd.shape[1]), wgt(w_attn.shape[1]),
                  gate(0), gate(1), gate(2)],
        out_specs=pl.BlockSpec((tm, tn), lambda j, i: (i, j)),
        out_shape=jax.ShapeDtypeStruct((m, d), BF16),
        compiler_params=_cparams("parallel", "parallel"),
        name="gated_merge",
    )(pool_o, ssd_o, attn_o, w_pool, w_ssd, w_attn, proj, proj, proj)


def _mm_res_kernel(a_ref, w_ref, x_ref, gt_ref, gpost_ref, gpre_ref, sc_ref, sh_ref, xo_ref, u_ref, *, nk, n_sub):
    tm = xo_ref.shape[0]
    slab = tm // n_sub

    def rows_of(ref, sl):
        return ref[...] if ref.shape[0] == 1 else ref[sl, :]

    def last_step(first):
        slabs = [slice(r * slab, (r + 1) * slab) for r in range(n_sub)]
        parts = [_dot(a_ref[sl, :], w_ref[...]) for sl in slabs]
        for sl, part in zip(slabs, parts):
            f = part if first else xo_ref[sl, :] + part
            xn = x_ref[sl, :] + rows_of(gt_ref, sl) * (_rms(f) * gpost_ref[...])
            xo_ref[sl, :] = xn
            y = _rms(xn) * gpre_ref[...]
            u_ref[sl, :] = (y * (1.0 + rows_of(sc_ref, sl)) + rows_of(sh_ref, sl)).astype(BF16)

    if nk == 1:
        last_step(True)
        return
    k = pl.program_id(2)

    @pl.when(k == 0)
    def _():
        xo_ref[...] = _dot(a_ref[...], w_ref[...])

    @pl.when((k > 0) & (k < nk - 1))
    def _():
        xo_ref[...] += _dot(a_ref[...], w_ref[...])

    @pl.when(k == nk - 1)
    def _():
        last_step(False)


def _pick_k(kdim, cap):
    for nk in range(1, kdim // LANES + 1):
        if kdim % nk == 0 and (kdim // nk) % LANES == 0 and kdim // nk <= cap:
            return kdim // nk
    return kdim


def _mm_res(a3, w, layer, x3, gt, g_post, g_pre, sc, sh):
    b, t, kdim = a3.shape
    d = x3.shape[2]
    r = gt.shape[1]
    tm = _pick(t, 512)
    tk = _pick_k(kdim, 2048)
    nk = kdim // tk
    mod_spec = pl.BlockSpec((None, r if r == 1 else tm, d),
                            (lambda bi, i, k: (bi, i, 0)) if r == t else (lambda bi, i, k: (bi, 0, 0)))
    vec_spec = pl.BlockSpec((1, d), lambda bi, i, k: (0, 0))
    row_spec = pl.BlockSpec((None, tm, d), lambda bi, i, k: (bi, i, 0))
    n_sub = 2 if tm % 512 == 0 else 1
    return pl.pallas_call(
        functools.partial(_mm_res_kernel, nk=nk, n_sub=n_sub),
        grid=(b, t // tm, nk),
        in_specs=[pl.BlockSpec((None, tm, tk), lambda bi, i, k: (bi, i, k)),
                  pl.BlockSpec((None, tk, d), lambda bi, i, k: (layer, k, 0)),
                  row_spec, mod_spec, vec_spec, vec_spec, mod_spec, mod_spec],
        out_specs=[row_spec, row_spec],
        out_shape=[jax.ShapeDtypeStruct((b, t, d), F32), jax.ShapeDtypeStruct((b, t, d), BF16)],
        compiler_params=_cparams("parallel", "parallel", "arbitrary"),
        name="matmul_residual_norm",
    )(a3, w, x3, gt, g_post, g_pre, sc, sh)


def _ffn_kernel(u_ref, wg_ref, wu_ref, o_ref, wg16_ref, wu16_ref):
    @pl.when(pl.program_id(1) == 0)
    def _():
        wg16_ref[...] = wg_ref[...].astype(BF16)
        wu16_ref[...] = wu_ref[...].astype(BF16)

    u = u_ref[...]
    o_ref[...] = (_silu(_dot(u, wg16_ref[...])) * _dot(u, wu16_ref[...])).astype(BF16)


def _ffn_hidden(u, w_gate, w_up, layer):
    m, k = u.shape
    n = w_gate.shape[2]
    tm = _pick(m, 1024)
    tn = _pick(n, 512)
    w_spec = pl.BlockSpec((None, k, tn), lambda j, i: (layer, 0, j))
    return pl.pallas_call(
        _ffn_kernel,
        grid=(n // tn, m // tm),
        in_specs=[pl.BlockSpec((tm, k), lambda j, i: (i, 0)), w_spec, w_spec],
        out_specs=pl.BlockSpec((tm, tn), lambda j, i: (i, j)),
        out_shape=jax.ShapeDtypeStruct((m, n), BF16),
        scratch_shapes=[pltpu.VMEM((k, tn), BF16), pltpu.VMEM((k, tn), BF16)],
        compiler_params=_cparams("parallel", "arbitrary"),
        name="ffn_hidden",
    )(u, w_gate, w_up)


def kernel(x_prompt, x_sample, cache_k, cache_v, state_ssm, state_conv, state_pool, page_table, c_prompt, c_sample, w_ada, b_ada, g_pre_mix, g_post_mix, g_pre_ffn, g_post_ffn, w_in, w_pool_grp, pool_scale, conv_w, conv_b, dt_bias, a_log, d_skip, ssd_norm, w_pool_br, w_ssd_br, w_attn_br, w_out, w_gate, w_up, w_down):
    bp, tp, d = x_prompt.shape
    bs, ts, _ = x_sample.shape
    depth = w_ada.shape[0]
    hd = d // ATTN_HEADS
    assert hd == LANES and ATTN_KV_HEADS * ATTN_GROUP * ts == LANES
    inner = ssd_norm.shape[1]
    n_heads = inner // SSD_HEADDIM
    conv_dim = conv_w.shape[2]
    attn_w = ATTN_HEADS * hd
    kv_w = ATTN_KV_HEADS * hd
    page = cache_k.shape[2]
    n_pages = page_table.shape[1]
    past = n_pages * page
    assert tp % MOBA_BLOCK == 0 and past % MOBA_BLOCK == 0 and n_heads <= LANES

    w_t = jnp.transpose(w_in, (0, 2, 1))
    col_z = d
    col_xbc = col_z + inner
    col_q = col_xbc + conv_dim
    col_k = col_q + attn_w
    col_v = col_k + kv_w
    col_gate = col_v + kv_w
    n_main = w_in.shape[2] - n_heads

    def project(u_p, u_s, layer):
        proj_p = _matmul_wt(u_p, w_t, layer, 0, n_main, col_q, n_heads)
        proj_s = _matmul_wt(u_s, w_t, layer, 0, n_main, col_q, n_heads)
        dt_p = _matmul_wt(u_p, w_t, layer, col_q, LANES, LANES, 0)
        dt_s = _matmul_wt(u_s, w_t, layer, col_q, LANES, LANES, 0)
        return proj_p, proj_s, dt_p, dt_s

    w_grp16 = w_pool_grp.astype(BF16)
    w_pool16, w_ssd16, w_attn16 = w_pool_br.astype(BF16), w_ssd_br.astype(BF16), w_attn_br.astype(BF16)
    w_out16, w_down16 = w_out.astype(BF16), w_down.astype(BF16)

    ssd_par = _ssd_params(conv_w, conv_b, dt_bias, a_log, d_skip, ssd_norm)

    n_c = bp + bs
    c_all = jnp.pad(jnp.concatenate([c_prompt, c_sample], axis=0), ((0, -n_c % 8), (0, 0)))
    mod = _ada(c_all, w_ada, b_ada)

    def mods(layer, which):
        m = mod[layer, :, which * d:(which + 1) * d]
        m_s = jnp.broadcast_to(m[bp:n_c, None, :], (bs, ts, d)).reshape(1, bs * ts, d)
        return m[:bp, None, :], m_s

    hg_w = (n_heads // SSD_GROUPS) * SSD_HEADDIM
    ts_pad = SSD_CHUNK
    pps = _pick(n_pages, 16)
    eye_kv = jnp.eye(ATTN_KV_HEADS, dtype=F32)

    xp, xs_ = x_prompt, x_sample.reshape(1, bs * ts, d)
    sh1 = mods(0, 0)
    sc1 = mods(0, 1)
    up = _norm_mod(xp, g_pre_mix[0:1], sc1[0], sh1[0])
    us = _norm_mod(xs_, g_pre_mix[0:1], sc1[1], sh1[1])

    outs = {n: [] for n in ("kp", "vp", "hp", "cp", "pp", "ks", "vs", "hs", "cs", "ps")}
    for layer in range(depth):
        gt1, sh2, sc2, gt2 = mods(layer, 2), mods(layer, 3), mods(layer, 4), mods(layer, 5)
        nxt = (layer + 1) % depth
        sh1n, sc1n = mods(nxt, 0), mods(nxt, 1)

        proj2, proj2_s, dt_p, dt_s = project(up.reshape(bp * tp, d), us.reshape(bs * ts, d), layer)

        proj = proj2.reshape(bp, tp, -1)
        k_p, v_p = proj[:, :, col_k:col_v], proj[:, :, col_v:col_gate]
        pool_o = _pool(proj, jnp.zeros((bp, POOL_HIST, d), F32), w_grp16, pool_scale[layer:layer + 1], layer, 0)
        ssd_o, h_p = _ssd(proj, dt_p.reshape(bp, tp, LANES), jnp.zeros((bp, CONV_HIST, conv_dim), F32),
                          jnp.zeros((bp, SSD_GROUPS, hg_w, SSD_STATE), F32), ssd_par, layer, col_z, col_xbc, tp)
        attn_o = _moba_prompt(proj, col_q, col_k, col_v)
        merged = _merge(pool_o.reshape(bp * tp, d), ssd_o.reshape(bp * tp, inner), attn_o.reshape(bp * tp, attn_w),
                        w_pool16, w_ssd16, w_attn16, proj2, col_gate, layer)
        xp, u_ffn_p = _mm_res(merged.reshape(bp, tp, d), w_out16, layer, xp, gt1[0], g_post_mix[layer:layer + 1],
                              g_pre_ffn[layer:layer + 1], sc2[0], sh2[0])
        outs["kp"].append(k_p.reshape(bp, tp, ATTN_KV_HEADS, hd))
        outs["vp"].append(v_p.reshape(bp, tp, ATTN_KV_HEADS, hd))
        outs["hp"].append(h_p.reshape(bp, n_heads, SSD_HEADDIM, SSD_STATE))
        outs["cp"].append(proj[:, tp - (SSD_CONV - 1):, col_xbc:col_xbc + conv_dim])
        outs["pp"].append(proj[:, tp - (POOL_HIST - 1):, :d])

        proj2 = proj2_s
        proj = proj2.reshape(bs, ts, -1)
        k_s, v_s, dt_s = proj[:, :, col_k:col_v], proj[:, :, col_v:col_gate], dt_s.reshape(bs, ts, LANES)
        pool_buf = jnp.pad(state_pool[layer], ((0, 0), (1, 0), (0, 0)))
        pool_o = _pool(proj, pool_buf, w_grp16, pool_scale[layer:layer + 1], layer, past)
        tpad = ((0, 0), (0, ts_pad - ts), (0, 0))
        conv_prev = jnp.pad(state_conv[layer], ((0, 0), (CONV_HIST - (SSD_CONV - 1), 0), (0, 0)))
        ssd_o, h_s = _ssd(jnp.pad(proj[:, :, :col_q], tpad), jnp.pad(dt_s, tpad), conv_prev,
                          state_ssm[layer].reshape(bs, SSD_GROUPS, hg_w, SSD_STATE), ssd_par, layer,
                          col_z, col_xbc, ts)
        ssd_o = ssd_o[:, :ts]
        q5 = proj[:, :, col_q:col_k].reshape(bs, ts, ATTN_KV_HEADS, ATTN_GROUP, hd)
        wq_t = jnp.einsum("btkgd,kj->bjdkgt", q5, eye_kv).reshape(bs, kv_w, LANES).astype(BF16)
        cache_k4 = cache_k.reshape(depth, cache_k.shape[1], page * ATTN_KV_HEADS, hd)
        cache_v4 = cache_v.reshape(depth, cache_v.shape[1], page * ATTN_KV_HEADS, hd)
        scores = _moba_scores(cache_k4, layer, page_table, wq_t, pps)
        probs = _moba_softmax(scores, k_s, wq_t, ts)
        o_full = _moba_pv(cache_v4, layer, page_table, probs, jnp.pad(v_s, ((0, 0), (0, LANES - ts), (0, 0))), pps)
        o6 = o_full.reshape(bs, ATTN_KV_HEADS, ATTN_GROUP, ts, ATTN_KV_HEADS, hd)
        attn_o = jnp.einsum("bkgtjd,kj->btkgd", o6, eye_kv).reshape(bs * ts, attn_w).astype(BF16)
        merged = _merge(pool_o.reshape(bs * ts, d), ssd_o.reshape(bs * ts, inner), attn_o,
                        w_pool16, w_ssd16, w_attn16, proj2, col_gate, layer)
        xs_, u_ffn_s = _mm_res(merged.reshape(1, bs * ts, d), w_out16, layer, xs_, gt1[1], g_post_mix[layer:layer + 1],
                               g_pre_ffn[layer:layer + 1], sc2[1], sh2[1])

        hid_p = _ffn_hidden(u_ffn_p.reshape(bp * tp, d), w_gate, w_up, layer)
        hid_s = _ffn_hidden(u_ffn_s.reshape(bs * ts, d), w_gate, w_up, layer)
        xp, up = _mm_res(hid_p.reshape(bp, tp, -1), w_down16, layer, xp, gt2[0], g_post_ffn[layer:layer + 1],
                         g_pre_mix[nxt:nxt + 1], sc1n[0], sh1n[0])
        xs_, us = _mm_res(hid_s.reshape(1, bs * ts, -1), w_down16, layer, xs_, gt2[1], g_post_ffn[layer:layer + 1],
                          g_pre_mix[nxt:nxt + 1], sc1n[1], sh1n[1])
        outs["ks"].append(k_s.reshape(bs, ts, ATTN_KV_HEADS, hd))
        outs["vs"].append(v_s.reshape(bs, ts, ATTN_KV_HEADS, hd))
        outs["hs"].append(h_s.reshape(bs, n_heads, SSD_HEADDIM, SSD_STATE))
        conv_ext = jnp.concatenate([state_conv[layer], proj[:, :, col_xbc:col_xbc + conv_dim]], axis=1)
        outs["cs"].append(conv_ext[:, ts:])
        pool_ext = jnp.concatenate([state_pool[layer], proj[:, :, :d]], axis=1)
        outs["ps"].append(pool_ext[:, ts:])

    st = {n: jnp.stack(v) for n, v in outs.items()}
    return (xp, xs_.reshape(bs, ts, d), st["kp"], st["vp"], st["hp"], st["cp"], st["pp"],
            st["ks"], st["vs"], st["hs"], st["cs"], st["ps"])
```

```python
import functools

import jax
import jax.numpy as jnp
from jax import lax
from jax.experimental import pallas as pl
from jax.experimental.pallas import tpu as pltpu

F32 = jnp.float32
BF16 = jnp.bfloat16

NORM_EPS = 1e-6
POOL_WINDOWS = (2, 4, 8, 16)
POOL_HIST = 16
SSD_HEADDIM = 64
SSD_GROUPS = 8
SSD_STATE = 128
SSD_CONV = 4
SSD_CHUNK = 128
SSD_GROUPS_PER_STEP = 4
CONV_HIST = 8
ATTN_HEADS = 16
ATTN_KV_HEADS = 8
ATTN_GROUP = ATTN_HEADS // ATTN_KV_HEADS
MOBA_BLOCK = 256
MOBA_TOPK = 3
PAST_GROUP = 2
LANES = 128
VMEM_LIMIT_BYTES = 56 * 1024 * 1024
NEG_INF = float("-inf")
LOG2E = 1.4426950408889634


def _cparams(*sem):
    return pltpu.CompilerParams(dimension_semantics=sem, vmem_limit_bytes=VMEM_LIMIT_BYTES)


def _sigmoid(x):
    return 0.5 * jnp.tanh(0.5 * x) + 0.5


def _silu(x):
    return x * _sigmoid(x)


def _dot(a, b):
    return jnp.dot(a, b, preferred_element_type=F32)


def _dot_nt(a, b):
    return lax.dot_general(a, b, (((1,), (1,)), ((), ())), preferred_element_type=F32)


def _dot_tn(a, b):
    return lax.dot_general(a, b, (((0,), (0,)), ((), ())), preferred_element_type=F32)


def _rms(x):
    return x * lax.rsqrt(jnp.mean(x * x, axis=-1, keepdims=True) + NORM_EPS)


def _pick(total, pref):
    if total <= pref:
        return total
    t = pref
    while total % t:
        t //= 2
    return t


def _ada_kernel(c_ref, w_ref, b_ref, o_ref):
    a = _silu(c_ref[...]).astype(BF16)
    o_ref[...] = _dot(a, w_ref[...].astype(BF16)) + b_ref[...]


def _ada(c_all, w_ada, b_ada):
    depth, d, n = w_ada.shape
    rows = c_all.shape[0]
    tn = _pick(n, 1024)
    return pl.pallas_call(
        _ada_kernel,
        grid=(depth, n // tn),
        in_specs=[pl.BlockSpec((rows, d), lambda l, j: (0, 0)),
                  pl.BlockSpec((None, d, tn), lambda l, j: (l, 0, j)),
                  pl.BlockSpec((None, 1, tn), lambda l, j: (l, 0, j))],
        out_specs=pl.BlockSpec((None, rows, tn), lambda l, j: (l, 0, j)),
        out_shape=jax.ShapeDtypeStruct((depth, rows, n), F32),
        compiler_params=_cparams("parallel", "parallel"),
        name="ada_mod",
    )(c_all, w_ada, b_ada.reshape(depth, 1, n))


def _norm_mod_kernel(x_ref, g_ref, sc_ref, sh_ref, u_ref):
    y = _rms(x_ref[...]) * g_ref[...]
    u_ref[...] = (y * (1.0 + sc_ref[...]) + sh_ref[...]).astype(BF16)


def _norm_mod(x, g, sc, sh):
    b, t, d = x.shape
    r = sc.shape[1]
    tt = _pick(t, 512)
    mod_spec = pl.BlockSpec((None, r if r == 1 else tt, d),
                            (lambda i, j: (i, j, 0)) if r == t else (lambda i, j: (i, 0, 0)))
    return pl.pallas_call(
        _norm_mod_kernel,
        grid=(b, t // tt),
        in_specs=[pl.BlockSpec((None, tt, d), lambda i, j: (i, j, 0)),
                  pl.BlockSpec((1, d), lambda i, j: (0, 0)),
                  mod_spec, mod_spec],
        out_specs=pl.BlockSpec((None, tt, d), lambda i, j: (i, j, 0)),
        out_shape=jax.ShapeDtypeStruct((b, t, d), BF16),
        compiler_params=_cparams("parallel", "parallel"),
        name="norm_mod",
    )(x, g, sc, sh)


def _mm_kernel(a_ref, w_ref, o_ref, w16_ref):
    @pl.when(pl.program_id(1) == 0)
    def _():
        w16_ref[...] = w_ref[0].astype(BF16)

    o_ref[...] = _dot_nt(a_ref[...], w16_ref[...])


def _matmul_wt(a, w_t, layer, row0, n, skip_at, skip):
    m, k = a.shape
    tm = _pick(m, 1024)
    tn = _pick(n, 1024)
    assert skip_at % tn == 0
    hole = skip_at // tn

    def w_index(j, i):
        return layer, pl.multiple_of(row0 + j * tn + jnp.where(j >= hole, skip, 0), 8), 0

    return pl.pallas_call(
        _mm_kernel,
        grid=(n // tn, m // tm),
        in_specs=[pl.BlockSpec((tm, k), lambda j, i: (i, 0)),
                  pl.BlockSpec((pl.Element(1), pl.Element(tn), pl.Element(k)), w_index)],
        out_specs=pl.BlockSpec((tm, tn), lambda j, i: (i, j)),
        out_shape=jax.ShapeDtypeStruct((m, n), F32),
        scratch_shapes=[pltpu.VMEM((tn, k), BF16)],
        compiler_params=_cparams("parallel", "arbitrary"),
        name="proj_matmul",
    )(a, w_t)


def _pool_kernel(p_ref, buf_ref, w_ref, sc_ref, o_ref, ext_ref, *, tt, start, pg):
    t = pl.program_id(1)

    @pl.when(t == 0)
    def _():
        ext_ref[0:POOL_HIST, :] = buf_ref[...]

    ext_ref[POOL_HIST:POOL_HIST + tt, :] = p_ref[...]
    n_valid = lax.broadcasted_iota(jnp.int32, (tt, 1), 0) + (t * tt + start + 1)
    for gi, w in enumerate(POOL_WINDOWS):
        c0 = gi * pg
        e = ext_ref[:, c0:c0 + pg]
        x = e[POOL_HIST:]
        s = e
        k = 1
        while k < w:
            s = s + pltpu.roll(s, k, axis=0)
            k *= 2
        s = s[POOL_HIST:]
        cnt = jnp.minimum(n_valid, w).astype(F32)
        mixed = (s / cnt - x).astype(BF16)
        y = _dot(mixed, w_ref[gi]) * sc_ref[:, c0:c0 + pg]
        o_ref[:, c0:c0 + pg] = y.astype(BF16)
    ext_ref[0:POOL_HIST, :] = ext_ref[tt:tt + POOL_HIST, :]


def _pool(proj3, buf, w_grp, scale, layer, start):
    b, t, _ = proj3.shape
    c = scale.shape[1]
    pg = c // len(POOL_WINDOWS)
    tt = _pick(t, 256)
    return pl.pallas_call(
        functools.partial(_pool_kernel, tt=tt, start=start, pg=pg),
        grid=(b, t // tt),
        in_specs=[pl.BlockSpec((None, tt, c), lambda i, j: (i, j, 0)),
                  pl.BlockSpec((None, POOL_HIST, c), lambda i, j: (i, 0, 0)),
                  pl.BlockSpec((None, len(POOL_WINDOWS), pg, pg), lambda i, j: (layer, 0, 0, 0)),
                  pl.BlockSpec((1, c), lambda i, j: (0, 0))],
        out_specs=pl.BlockSpec((None, tt, c), lambda i, j: (i, j, 0)),
        out_shape=jax.ShapeDtypeStruct((b, t, c), BF16),
        scratch_shapes=[pltpu.VMEM((POOL_HIST + tt, c), F32)],
        compiler_params=_cparams("parallel", "arbitrary"),
        name="pool_mixer",
    )(proj3, buf, w_grp, scale)


def _ssd_kernel(z_ref, x_ref, b_ref, c_ref, dt_ref, cpx_ref, cpb_ref, cpc_ref,
                px_ref, pb_ref, pc_ref, pdt_ref, exp_ref, h0_ref,
                y_ref, hout_ref, histx, histb, histc, ht, *, t_valid, n_chunks, hg, gps):
    gq = pl.program_id(1)
    c = pl.program_id(2)
    lc = SSD_CHUNK
    gw = hg * SSD_HEADDIM
    ns = SSD_STATE

    @pl.when(c == 0)
    def _():
        histx[...] = cpx_ref[...]
        histb[...] = cpb_ref[...]
        histc[...] = cpc_ref[...]
        for s in range(gps):
            ht[s] = h0_ref[s].T

    def conv(hist, raw_ref, par_ref):
        x = raw_ref[...]
        prev = hist[...]
        hist[...] = x[lc - CONV_HIST:, :]
        row8 = lax.broadcasted_iota(jnp.int32, prev.shape, 0)
        acc = par_ref[SSD_CONV:SSD_CONV + 1, :] + x * par_ref[SSD_CONV - 1:SSD_CONV, :]
        for k in range(1, SSD_CONV):
            r = pltpu.roll(x, k, axis=0)
            first = jnp.where(row8 < k, pltpu.roll(prev, k, axis=0), r[0:CONV_HIST])
            shifted = jnp.concatenate([first, r[CONV_HIST:]], axis=0)
            acc = acc + shifted * par_ref[SSD_CONV - 1 - k:SSD_CONV - k, :]
        return _silu(acc)

    xs_all = conv(histx, x_ref, px_ref)
    bm_all = conv(histb, b_ref, pb_ref)
    cm_all = conv(histc, c_ref, pc_ref)

    row = lax.broadcasted_iota(jnp.int32, (lc, LANES), 0)
    col = lax.broadcasted_iota(jnp.int32, (lc, LANES), 1)
    xdt = dt_ref[...] + pdt_ref[0:1, :]
    dt = jnp.maximum(xdt, 0.0) + jnp.log1p(jnp.exp(-jnp.abs(xdt)))
    dt = jnp.where(c * lc + row < t_valid, dt, 0.0)
    dta = dt * (-jnp.exp(pdt_ref[1:2, :]))
    acs = dta
    k = 1
    while k < lc:
        acs = acs + jnp.where(row >= k, pltpu.roll(acs, k, axis=0), 0.0)
        k *= 2
    a_last = acs[lc - 1:lc, :]
    w_end = dt * jnp.exp(a_last - acs)
    exp_a = jnp.exp(acs)

    cd = jnp.exp(a_last)
    cd_hi = cd.astype(BF16).astype(F32)
    cd_mid = (cd - cd_hi).astype(BF16).astype(F32)
    cd_lo = cd - cd_hi - cd_mid
    pieces = [jnp.broadcast_to(p, (16, LANES)) for p in (cd_hi, cd_mid, cd_lo)]
    stacked = jnp.concatenate([dt, w_end, exp_a] + pieces, axis=0).astype(BF16)
    acs2 = acs * LOG2E
    causal = row >= col
    heads_here = LANES // SSD_HEADDIM

    for s in range(gps):
        xl, nl = slice(s * gw, (s + 1) * gw), slice(s * ns, (s + 1) * ns)
        xs, bm, cm = xs_all[:, xl], bm_all[:, nl], cm_all[:, nl]
        wide = _dot(stacked, exp_ref[s])
        dt_e, w_e, ea_e = wide[0:lc], wide[lc:2 * lc], wide[2 * lc:3 * lc]
        r0 = 3 * lc
        cd_e = wide[r0:r0 + 1] + wide[r0 + 16:r0 + 17] + wide[r0 + 32:r0 + 33]

        acs_g = pltpu.roll(acs2, (LANES - (gq * gps + s) * hg) % LANES, axis=1)
        acs_gt = acs_g.T

        bm16 = bm.astype(BF16)
        cm16 = cm.astype(BF16)
        cb = jnp.where(causal, _dot_nt(cm16, bm16), 0.0)
        xdt_e = xs * dt_e
        y_parts = []
        for pr in range(gw // LANES):
            xp = xdt_e[:, pr * LANES:(pr + 1) * LANES]
            lhs, rhs = [], []
            for hh in range(heads_here):
                h = pr * heads_here + hh
                seg = acs_g[:, h:h + 1] - acs_gt[h:h + 1, :]
                lhs.append((cb * jnp.exp2(jnp.minimum(seg, 0.0))).astype(BF16))
                in_head = (col >= hh * SSD_HEADDIM) & (col < (hh + 1) * SSD_HEADDIM)
                rhs.append(jnp.where(in_head, xp, 0.0).astype(BF16))
            y_parts.append(_dot(jnp.concatenate(lhs, axis=1), jnp.concatenate(rhs, axis=0)))
        y = jnp.concatenate(y_parts, axis=1)

        h_in = ht[s]
        y = y + _dot(cm16, h_in.astype(BF16)) * ea_e
        h_new = h_in * cd_e + _dot(bm.T.astype(BF16), (xs * w_e).astype(BF16))
        ht[s] = h_new

        y = y + px_ref[SSD_CONV + 1:SSD_CONV + 2, xl] * xs
        y = y * _silu(z_ref[:, xl])
        y_ref[:, xl] = (_rms(y) * px_ref[SSD_CONV + 2:SSD_CONV + 3, xl]).astype(BF16)

    @pl.when(c == n_chunks - 1)
    def _():
        for s in range(gps):
            hout_ref[s] = ht[s].T


def _ssd_params(conv_w, conv_b, dt_bias, a_log, d_skip, ssd_norm):
    depth, inner = ssd_norm.shape
    n_heads = dt_bias.shape[1]
    rows = lambda v: v[:, None, :]
    px = jnp.concatenate([conv_w[:, :, :inner], rows(conv_b[:, :inner]), rows(jnp.repeat(d_skip, SSD_HEADDIM, axis=1)),
                          rows(ssd_norm), jnp.zeros((depth, 8 - SSD_CONV - 3, inner), F32)], axis=1)
    pbc = jnp.concatenate([conv_w[:, :, inner:], rows(conv_b[:, inner:]),
                           jnp.zeros((depth, 8 - SSD_CONV - 1, conv_w.shape[2] - inner), F32)], axis=1)
    pdt = jnp.pad(jnp.stack([dt_bias, a_log], axis=1), ((0, 0), (0, 6), (0, LANES - n_heads)))
    return px, pbc, pdt


def _ssd(proj3, dt3, conv_prev, h0, params, layer, col_z, col_xbc, t_valid):
    b, tp, _ = proj3.shape
    px, pbc, pdt = params
    inner = px.shape[2]
    n_heads = inner // SSD_HEADDIM
    hg = n_heads // SSD_GROUPS
    gw = hg * SSD_HEADDIM
    ns = SSD_STATE
    lc = SSD_CHUNK
    n_chunks = tp // lc
    gps = SSD_GROUPS_PER_STEP
    xw, nw = gps * gw, gps * ns
    assert col_z % xw == 0 and col_xbc % xw == 0 and (col_xbc + inner) % nw == 0 and SSD_GROUPS % gps == 0
    zb, xb = col_z // xw, col_xbc // xw
    bb = (col_xbc + inner) // nw
    cb_ = (col_xbc + inner + SSD_GROUPS * ns) // nw
    cxb, cbb, ccb = 0, inner // nw, (inner + SSD_GROUPS * ns) // nw
    head_of = (jnp.arange(SSD_GROUPS)[:, None] * hg + jnp.arange(gw)[None, :] // SSD_HEADDIM)
    expand = (jnp.arange(LANES)[None, :, None] == head_of[:, None, :]).astype(BF16)

    def seq(width, off):
        return pl.BlockSpec((None, lc, width), lambda i, g, c: (i, c, off + g))

    def prev(width, off):
        return pl.BlockSpec((None, CONV_HIST, width), lambda i, g, c: (i, 0, off + g))

    def par(width, off):
        return pl.BlockSpec((None, 8, width), lambda i, g, c: (layer, 0, off + g))

    state_spec = pl.BlockSpec((None, gps, gw, ns), lambda i, g, c: (i, g, 0, 0))
    kernel = functools.partial(_ssd_kernel, t_valid=t_valid, n_chunks=n_chunks, hg=hg, gps=gps)
    return pl.pallas_call(
        kernel,
        grid=(b, SSD_GROUPS // gps, n_chunks),
        in_specs=[seq(xw, zb), seq(xw, xb), seq(nw, bb), seq(nw, cb_),
                  pl.BlockSpec((None, lc, LANES), lambda i, g, c: (i, c, 0)),
                  prev(xw, cxb), prev(nw, cbb), prev(nw, ccb),
                  par(xw, 0), par(nw, 0), par(nw, SSD_GROUPS // gps),
                  pl.BlockSpec((None, 8, LANES), lambda i, g, c: (layer, 0, 0)),
                  pl.BlockSpec((gps, LANES, gw), lambda i, g, c: (g, 0, 0)),
                  state_spec],
        out_specs=[pl.BlockSpec((None, lc, xw), lambda i, g, c: (i, c, g)), state_spec],
        out_shape=[jax.ShapeDtypeStruct((b, tp, inner), BF16),
                   jax.ShapeDtypeStruct((b, SSD_GROUPS, gw, ns), F32)],
        scratch_shapes=[pltpu.VMEM((CONV_HIST, xw), F32),
                        pltpu.VMEM((CONV_HIST, nw), F32),
                        pltpu.VMEM((CONV_HIST, nw), F32),
                        pltpu.VMEM((gps, ns, gw), F32)],
        compiler_params=_cparams("parallel", "parallel", "arbitrary"),
        name="ssd_mixer",
    )(proj3, proj3, proj3, proj3, dt3, conv_prev, conv_prev, conv_prev, px, pbc, pbc, pdt, expand, h0)


def _top_k_mask(gate, valid, ids, axis):
    big = jnp.int32(2 ** 30)
    remaining = valid
    sel = jnp.zeros(gate.shape, F32)
    for _ in range(MOBA_TOPK):
        gm = jnp.where(remaining > 0.5, gate, NEG_INF)
        m = jnp.max(gm, axis=axis, keepdims=True)
        cand = jnp.where(gm == m, remaining, 0.0)
        first = jnp.min(jnp.where(cand > 0.5, ids, big), axis=axis, keepdims=True)
        pick = jnp.where(ids == first, 1.0, 0.0)
        sel = sel + pick
        remaining = remaining - pick
    return sel


def _moba_prompt_kernel(q_ref, k_ref, v_ref, o_ref, kb, vtb, km, sel_ref, qs_ref, acc_ref, ml_ref, sa_ref, sb_ref,
                        *, n_blocks, scale):
    i = pl.program_id(2)
    blk = MOBA_BLOCK
    hd = LANES
    nq = ATTN_GROUP * blk
    c2 = scale * LOG2E

    @pl.when(i == 0)
    def _():
        km[...] = jnp.zeros(km.shape, F32)
        for j in range(n_blocks):
            kj = k_ref[j * blk:(j + 1) * blk, :]
            kb[j] = kj.astype(BF16)
            km[j:j + 1, :] = jnp.mean(kj, axis=0, keepdims=True)
            vtb[j] = v_ref[j * blk:(j + 1) * blk, :].T.astype(BF16)

    grp = PAST_GROUP

    def block_at(pos):
        return jnp.where(pos == 0, i, jnp.minimum(pos - 1, n_blocks - 1))

    def scores(s_ref, p0, own_first=False):
        for k in range(grp):
            s = _dot_nt(kb[block_at(p0 + k)], qs_ref[...])
            if own_first and k == 0:
                key_r = lax.broadcasted_iota(jnp.int32, (blk, nq), 0)
                qry_c = lax.broadcasted_iota(jnp.int32, (blk, nq), 1) & (blk - 1)
                s = jnp.where(key_r <= qry_c, s, NEG_INF)
            s_ref[k] = s

    q = q_ref[...]
    qs_ref[...] = jnp.concatenate([q[:, g * hd:(g + 1) * hd] for g in range(ATTN_GROUP)], axis=0).astype(BF16)
    scores(sa_ref, 0, own_first=True)
    gate_t = _dot_nt(km[...].astype(BF16), qs_ref[...])
    ids = lax.broadcasted_iota(jnp.int32, (km.shape[0], nq), 0)
    valid = jnp.where(ids < i, 1.0, 0.0)
    sel_ref[...] = _top_k_mask(gate_t, valid, ids, axis=0)
    ml_ref[0:1, :] = jnp.full((1, nq), NEG_INF, F32)
    ml_ref[1:2, :] = jnp.zeros((1, nq), F32)
    acc_ref[...] = jnp.zeros(acc_ref.shape, F32)

    def update(s_ref, p0):
        m_old, l_old = ml_ref[0:1, :], ml_ref[1:2, :]
        picked = [sel_ref[pl.ds(jnp.maximum(p0 + k - 1, 0), 1), :] + jnp.where(p0 + k == 0, 1.0, 0.0) > 0.5
                  for k in range(grp)]
        m_new = m_old
        for k in range(grp):
            m_new = jnp.maximum(m_new, jnp.where(picked[k], jnp.max(s_ref[k], axis=0, keepdims=True), NEG_INF))
        alpha = jnp.exp2((m_old - m_new) * c2)
        l_new = alpha * l_old
        pv = None
        for k in range(grp):
            p = jnp.exp2((s_ref[k] - jnp.where(picked[k], m_new, jnp.inf)) * c2)
            l_new = l_new + jnp.sum(p, axis=0, keepdims=True)
            d = _dot(vtb[block_at(p0 + k)], p.astype(BF16))
            pv = d if pv is None else pv + d
        acc_ref[...] = alpha * acc_ref[...] + pv
        ml_ref[0:1, :] = m_new
        ml_ref[1:2, :] = l_new

    n_groups = (i + grp) // grp

    def pair_body(u, carry):
        p0 = 2 * u * grp
        scores(sb_ref, p0 + grp)
        update(sa_ref, p0)
        scores(sa_ref, p0 + 2 * grp)
        update(sb_ref, p0 + grp)
        return carry

    lax.fori_loop(0, n_groups // 2, pair_body, 0)

    @pl.when((n_groups & 1) != 0)
    def _():
        update(sa_ref, (n_groups - 1) * grp)

    o = (acc_ref[...] / ml_ref[1:2, :]).T
    o_ref[...] = jnp.concatenate([o[g * blk:(g + 1) * blk] for g in range(ATTN_GROUP)], axis=1).astype(BF16)


def _moba_prompt(proj3, col_q, col_k, col_v):
    b, t, _ = proj3.shape
    hd = LANES
    blk = MOBA_BLOCK
    n_blocks = t // blk
    nb_pad = -(-(n_blocks + PAST_GROUP) // 16) * 16
    qw = ATTN_GROUP * hd
    qb, kb0, vb0 = col_q // qw, col_k // hd, col_v // hd
    kernel = functools.partial(_moba_prompt_kernel, n_blocks=n_blocks, scale=hd ** -0.5)
    return pl.pallas_call(
        kernel,
        grid=(b, ATTN_KV_HEADS, n_blocks),
        in_specs=[pl.BlockSpec((None, blk, qw), lambda bi, h, i: (bi, i, qb + h)),
                  pl.BlockSpec((None, t, hd), lambda bi, h, i: (bi, 0, kb0 + h)),
                  pl.BlockSpec((None, t, hd), lambda bi, h, i: (bi, 0, vb0 + h))],
        out_specs=pl.BlockSpec((None, blk, qw), lambda bi, h, i: (bi, i, h)),
        out_shape=jax.ShapeDtypeStruct((b, t, ATTN_HEADS * hd), BF16),
        scratch_shapes=[pltpu.VMEM((n_blocks, blk, hd), BF16), pltpu.VMEM((n_blocks, hd, blk), BF16),
                        pltpu.VMEM((nb_pad, hd), F32), pltpu.VMEM((nb_pad, ATTN_GROUP * blk), F32),
                        pltpu.VMEM((ATTN_GROUP * blk, hd), BF16), pltpu.VMEM((hd, ATTN_GROUP * blk), F32),
                        pltpu.VMEM((8, ATTN_GROUP * blk), F32),
                        pltpu.VMEM((PAST_GROUP, blk, ATTN_GROUP * blk), F32),
                        pltpu.VMEM((PAST_GROUP, blk, ATTN_GROUP * blk), F32)],
        compiler_params=_cparams("parallel", "parallel", "arbitrary"),
        name="moba_prompt",
    )(proj3, proj3, proj3)


def _moba_scores_kernel(pt_ref, wq_ref, *refs, pps, page):
    del pt_ref
    k_refs, o_ref = refs[:pps], refs[pps]
    hd = LANES
    for r in range(pps):
        acc = None
        for h in range(ATTN_KV_HEADS):
            kh = k_refs[r][pl.ds(h, page, stride=ATTN_KV_HEADS), :].astype(BF16)
            part = _dot(kh, wq_ref[h * hd:(h + 1) * hd, :])
            acc = part if acc is None else acc + part
        o_ref[r * page:(r + 1) * page, :] = acc


def _moba_scores(cache4, layer, page_table, wq_t, pps):
    b, n_pages = page_table.shape
    rows, hd = cache4.shape[2], cache4.shape[3]
    page, width = rows // ATTN_KV_HEADS, hd * ATTN_KV_HEADS

    def page_spec(r):
        return pl.BlockSpec((None, None, rows, hd), lambda bi, p, pt: (layer, pt[bi, p * pps + r], 0, 0))

    grid_spec = pltpu.PrefetchScalarGridSpec(
        num_scalar_prefetch=1,
        grid=(b, n_pages // pps),
        in_specs=[pl.BlockSpec((None, width, LANES), lambda bi, p, pt: (bi, 0, 0))]
                 + [page_spec(r) for r in range(pps)],
        out_specs=pl.BlockSpec((None, pps * page, LANES), lambda bi, p, pt: (bi, p, 0)),
    )
    return pl.pallas_call(
        functools.partial(_moba_scores_kernel, pps=pps, page=page),
        grid_spec=grid_spec,
        out_shape=jax.ShapeDtypeStruct((b, n_pages * page, LANES), F32),
        compiler_params=_cparams("parallel", "arbitrary"),
        name="moba_sample_scores",
    )(page_table, wq_t, *([cache4] * pps))


def _moba_softmax_kernel(st_ref, kn_ref, wq_ref, p_ref, gate_ref, sel_ref, *, n_blocks, tq, scale):
    blk = MOBA_BLOCK
    past = n_blocks * blk
    gate_ref[...] = jnp.zeros(gate_ref.shape, F32)

    def gate_body(j, carry):
        off = pl.multiple_of(j * blk, blk)
        gate_ref[pl.ds(j, 1), :] = jnp.mean(st_ref[pl.ds(off, blk), :], axis=0, keepdims=True)
        return carry

    lax.fori_loop(0, n_blocks, gate_body, 0)
    nrow = gate_ref.shape[0]
    ids = lax.broadcasted_iota(jnp.int32, (nrow, LANES), 0)
    valid = jnp.where(ids < n_blocks, 1.0, 0.0)
    sel_ref[...] = _top_k_mask(gate_ref[...], valid, ids, axis=0)

    s_own = _dot(kn_ref[...].astype(BF16), wq_ref[...]) * scale
    key_t = lax.broadcasted_iota(jnp.int32, (tq, LANES), 0)
    qry_t = lax.broadcasted_iota(jnp.int32, (tq, LANES), 1) & (tq - 1)
    s_own = jnp.where(key_t <= qry_t, s_own, NEG_INF)
    m0 = jnp.max(s_own, axis=0, keepdims=True)

    def max_body(j, m):
        off = pl.multiple_of(j * blk, blk)
        bm = jnp.max(st_ref[pl.ds(off, blk), :], axis=0, keepdims=True) * scale
        return jnp.maximum(m, jnp.where(sel_ref[pl.ds(j, 1), :] > 0.5, bm, NEG_INF))

    m = lax.fori_loop(0, n_blocks, max_body, m0)
    p_own = jnp.exp(s_own - m)
    l0 = jnp.sum(p_own, axis=0, keepdims=True)

    def exp_body(j, l):
        off = pl.multiple_of(j * blk, blk)
        s = st_ref[pl.ds(off, blk), :] * scale
        p = jnp.where(sel_ref[pl.ds(j, 1), :] > 0.5, jnp.exp(s - m), 0.0)
        p_ref[pl.ds(off, blk), :] = p
        return l + jnp.sum(p, axis=0, keepdims=True)

    l = lax.fori_loop(0, n_blocks, exp_body, l0)
    inv = 1.0 / l

    def norm_body(j, carry):
        off = pl.multiple_of(j * blk, blk)
        p_ref[pl.ds(off, blk), :] = p_ref[pl.ds(off, blk), :] * inv
        return carry

    lax.fori_loop(0, n_blocks, norm_body, 0)
    p_ref[past:past + LANES, :] = jnp.zeros((LANES, LANES), F32)
    p_ref[past:past + tq, :] = p_own * inv


def _moba_softmax(scores, k_new, wq_t, tq):
    b, past, _ = scores.shape
    width = k_new.shape[2]
    n_blocks = past // MOBA_BLOCK
    nrow = -(-n_blocks // 8) * 8
    kernel = functools.partial(_moba_softmax_kernel, n_blocks=n_blocks, tq=tq, scale=LANES ** -0.5)
    return pl.pallas_call(
        kernel,
        grid=(b,),
        in_specs=[pl.BlockSpec((None, past, LANES), lambda bi: (bi, 0, 0)),
                  pl.BlockSpec((None, tq, width), lambda bi: (bi, 0, 0)),
                  pl.BlockSpec((None, width, LANES), lambda bi: (bi, 0, 0))],
        out_specs=pl.BlockSpec((None, past + LANES, LANES), lambda bi: (bi, 0, 0)),
        out_shape=jax.ShapeDtypeStruct((b, past + LANES, LANES), F32),
        scratch_shapes=[pltpu.VMEM((nrow, LANES), F32), pltpu.VMEM((nrow, LANES), F32)],
        compiler_params=_cparams("parallel"),
        name="moba_sample_softmax",
    )(scores, k_new, wq_t)


def _moba_pv_kernel(pt_ref, p_ref, pown_ref, vnew_ref, *refs, pps, page):
    del pt_ref
    v_refs, o_ref = refs[:pps], refs[pps]

    @pl.when(pl.program_id(1) == 0)
    def _():
        o_ref[...] = _dot_tn(pown_ref[...].astype(BF16), vnew_ref[...].astype(BF16))

    hd = LANES
    for h in range(ATTN_KV_HEADS):
        acc = o_ref[:, h * hd:(h + 1) * hd]
        for r in range(pps):
            vh = v_refs[r][pl.ds(h, page, stride=ATTN_KV_HEADS), :].astype(BF16)
            acc = acc + _dot_tn(p_ref[r * page:(r + 1) * page, :].astype(BF16), vh)
        o_ref[:, h * hd:(h + 1) * hd] = acc


def _moba_pv(cache4, layer, page_table, probs, v_new_pad, pps):
    b, n_pages = page_table.shape
    rows, hd = cache4.shape[2], cache4.shape[3]
    page, width = rows // ATTN_KV_HEADS, hd * ATTN_KV_HEADS
    own_blk = (n_pages * page) // LANES

    def page_spec(r):
        return pl.BlockSpec((None, None, rows, hd), lambda bi, p, pt: (layer, pt[bi, p * pps + r], 0, 0))

    grid_spec = pltpu.PrefetchScalarGridSpec(
        num_scalar_prefetch=1,
        grid=(b, n_pages // pps),
        in_specs=[pl.BlockSpec((None, pps * page, LANES), lambda bi, p, pt: (bi, p, 0)),
                  pl.BlockSpec((None, LANES, LANES), lambda bi, p, pt: (bi, own_blk, 0)),
                  pl.BlockSpec((None, LANES, width), lambda bi, p, pt: (bi, 0, 0))]
                 + [page_spec(r) for r in range(pps)],
        out_specs=pl.BlockSpec((None, LANES, width), lambda bi, p, pt: (bi, 0, 0)),
    )
    return pl.pallas_call(
        functools.partial(_moba_pv_kernel, pps=pps, page=page),
        grid_spec=grid_spec,
        out_shape=jax.ShapeDtypeStruct((b, LANES, width), F32),
        compiler_params=_cparams("parallel", "arbitrary"),
        name="moba_sample_pv",
    )(page_table, probs, probs, v_new_pad, *([cache4] * pps))


def _merge_kernel(ap_ref, as_ref, aa_ref, wp_ref, ws_ref, wa_ref, g0_ref, g1_ref, g2_ref, o_ref):
    acc = _sigmoid(g0_ref[...]) * _dot(ap_ref[...], wp_ref[...])
    acc = acc + _sigmoid(g1_ref[...]) * _dot(as_ref[...], ws_ref[...])
    acc = acc + _sigmoid(g2_ref[...]) * _dot(aa_ref[...], wa_ref[...])
    o_ref[...] = acc.astype(BF16)


def _merge(pool_o, ssd_o, attn_o, w_pool, w_ssd, w_attn, proj, col_gate, layer):
    m, d = pool_o.shape
    tm = _pick(m, 512)
    tn = _pick(d, 512)
    gb = col_gate // tn
    nb = d // tn

    def act(width):
        return pl.BlockSpec((tm, width), lambda j, i: (i, 0))

    def wgt(kdim):
        return pl.BlockSpec((None, kdim, tn), lambda j, i: (layer, 0, j))

    def gate(which):
        return pl.BlockSpec((tm, tn), lambda j, i: (i, gb + which * nb + j))

    return pl.pallas_call(
        _merge_kernel,
        grid=(d // tn, m // tm),
        in_specs=[act(pool_o.shape[1]), act(ssd_o.shape[1]), act(attn_o.shape[1]),
                  wgt(w_pool.shape[1]), wgt(w_ssd.shape[1]), wgt(w_attn.shape[1]),
                  gate(0), gate(1), gate(2)],
        out_specs=pl.BlockSpec((tm, tn), lambda j, i: (i, j)),
        out_shape=jax.ShapeDtypeStruct((m, d), BF16),
        compiler_params=_cparams("parallel", "parallel"),
        name="gated_merge",
    )(pool_o, ssd_o, attn_o, w_pool, w_ssd, w_attn, proj, proj, proj)


def _mm_res_kernel(a_ref, w_ref, x_ref, gt_ref, gpost_ref, gpre_ref, sc_ref, sh_ref, xo_ref, u_ref, *, nk, n_sub):
    tm = xo_ref.shape[0]
    slab = tm // n_sub

    def rows_of(ref, sl):
        return ref[...] if ref.shape[0] == 1 else ref[sl, :]

    def last_step(first):
        slabs = [slice(r * slab, (r + 1) * slab) for r in range(n_sub)]
        parts = [_dot(a_ref[sl, :], w_ref[...]) for sl in slabs]
        for sl, part in zip(slabs, parts):
            f = part if first else xo_ref[sl, :] + part
            xn = x_ref[sl, :] + rows_of(gt_ref, sl) * (_rms(f) * gpost_ref[...])
            xo_ref[sl, :] = xn
            y = _rms(xn) * gpre_ref[...]
            u_ref[sl, :] = (y * (1.0 + rows_of(sc_ref, sl)) + rows_of(sh_ref, sl)).astype(BF16)

    if nk == 1:
        last_step(True)
        return
    k = pl.program_id(2)

    @pl.when(k == 0)
    def _():
        xo_ref[...] = _dot(a_ref[...], w_ref[...])

    @pl.when((k > 0) & (k < nk - 1))
    def _():
        xo_ref[...] += _dot(a_ref[...], w_ref[...])

    @pl.when(k == nk - 1)
    def _():
        last_step(False)


def _pick_k(kdim, cap):
    for nk in range(1, kdim // LANES + 1):
        if kdim % nk == 0 and (kdim // nk) % LANES == 0 and kdim // nk <= cap:
            return kdim // nk
    return kdim


def _mm_res(a3, w, layer, x3, gt, g_post, g_pre, sc, sh):
    b, t, kdim = a3.shape
    d = x3.shape[2]
    r = gt.shape[1]
    tm = _pick(t, 512)
    tk = _pick_k(kdim, 2048)
    nk = kdim // tk
    mod_spec = pl.BlockSpec((None, r if r == 1 else tm, d),
                            (lambda bi, i, k: (bi, i, 0)) if r == t else (lambda bi, i, k: (bi, 0, 0)))
    vec_spec = pl.BlockSpec((1, d), lambda bi, i, k: (0, 0))
    row_spec = pl.BlockSpec((None, tm, d), lambda bi, i, k: (bi, i, 0))
    n_sub = 2 if tm % 512 == 0 else 1
    return pl.pallas_call(
        functools.partial(_mm_res_kernel, nk=nk, n_sub=n_sub),
        grid=(b, t // tm, nk),
        in_specs=[pl.BlockSpec((None, tm, tk), lambda bi, i, k: (bi, i, k)),
                  pl.BlockSpec((None, tk, d), lambda bi, i, k: (layer, k, 0)),
                  row_spec, mod_spec, vec_spec, vec_spec, mod_spec, mod_spec],
        out_specs=[row_spec, row_spec],
        out_shape=[jax.ShapeDtypeStruct((b, t, d), F32), jax.ShapeDtypeStruct((b, t, d), BF16)],
        compiler_params=_cparams("parallel", "parallel", "arbitrary"),
        name="matmul_residual_norm",
    )(a3, w, x3, gt, g_post, g_pre, sc, sh)


def _ffn_kernel(u_ref, wg_ref, wu_ref, o_ref, wg16_ref, wu16_ref):
    @pl.when(pl.program_id(1) == 0)
    def _():
        wg16_ref[...] = wg_ref[...].astype(BF16)
        wu16_ref[...] = wu_ref[...].astype(BF16)

    u = u_ref[...]
    o_ref[...] = (_silu(_dot(u, wg16_ref[...])) * _dot(u, wu16_ref[...])).astype(BF16)


def _ffn_hidden(u, w_gate, w_up, layer):
    m, k = u.shape
    n = w_gate.shape[2]
    tm = _pick(m, 1024)
    tn = _pick(n, 512)
    w_spec = pl.BlockSpec((None, k, tn), lambda j, i: (layer, 0, j))
    return pl.pallas_call(
        _ffn_kernel,
        grid=(n // tn, m // tm),
        in_specs=[pl.BlockSpec((tm, k), lambda j, i: (i, 0)), w_spec, w_spec],
        out_specs=pl.BlockSpec((tm, tn), lambda j, i: (i, j)),
        out_shape=jax.ShapeDtypeStruct((m, n), BF16),
        scratch_shapes=[pltpu.VMEM((k, tn), BF16), pltpu.VMEM((k, tn), BF16)],
        compiler_params=_cparams("parallel", "arbitrary"),
        name="ffn_hidden",
    )(u, w_gate, w_up)


def kernel(x_prompt, x_sample, cache_k, cache_v, state_ssm, state_conv, state_pool, page_table, c_prompt, c_sample, w_ada, b_ada, g_pre_mix, g_post_mix, g_pre_ffn, g_post_ffn, w_in, w_pool_grp, pool_scale, conv_w, conv_b, dt_bias, a_log, d_skip, ssd_norm, w_pool_br, w_ssd_br, w_attn_br, w_out, w_gate, w_up, w_down):
    bp, tp, d = x_prompt.shape
    bs, ts, _ = x_sample.shape
    depth = w_ada.shape[0]
    hd = d // ATTN_HEADS
    assert hd == LANES and ATTN_KV_HEADS * ATTN_GROUP * ts == LANES
    inner = ssd_norm.shape[1]
    n_heads = inner // SSD_HEADDIM
    conv_dim = conv_w.shape[2]
    attn_w = ATTN_HEADS * hd
    kv_w = ATTN_KV_HEADS * hd
    page = cache_k.shape[2]
    n_pages = page_table.shape[1]
    past = n_pages * page
    assert tp % MOBA_BLOCK == 0 and past % MOBA_BLOCK == 0 and n_heads <= LANES

    w_t = jnp.transpose(w_in, (0, 2, 1))
    col_z = d
    col_xbc = col_z + inner
    col_q = col_xbc + conv_dim
    col_k = col_q + attn_w
    col_v = col_k + kv_w
    col_gate = col_v + kv_w
    n_main = w_in.shape[2] - n_heads

    def project(u_p, u_s, layer):
        proj_p = _matmul_wt(u_p, w_t, layer, 0, n_main, col_q, n_heads)
        proj_s = _matmul_wt(u_s, w_t, layer, 0, n_main, col_q, n_heads)
        dt_p = _matmul_wt(u_p, w_t, layer, col_q, LANES, LANES, 0)
        dt_s = _matmul_wt(u_s, w_t, layer, col_q, LANES, LANES, 0)
        return proj_p, proj_s, dt_p, dt_s

    w_grp16 = w_pool_grp.astype(BF16)
    w_pool16, w_ssd16, w_attn16 = w_pool_br.astype(BF16), w_ssd_br.astype(BF16), w_attn_br.astype(BF16)
    w_out16, w_down16 = w_out.astype(BF16), w_down.astype(BF16)

    ssd_par = _ssd_params(conv_w, conv_b, dt_bias, a_log, d_skip, ssd_norm)

    n_c = bp + bs
    c_all = jnp.pad(jnp.concatenate([c_prompt, c_sample], axis=0), ((0, -n_c % 8), (0, 0)))
    mod = _ada(c_all, w_ada, b_ada)

    def mods(layer, which):
        m = mod[layer, :, which * d:(which + 1) * d]
        m_s = jnp.broadcast_to(m[bp:n_c, None, :], (bs, ts, d)).reshape(1, bs * ts, d)
        return m[:bp, None, :], m_s

    hg_w = (n_heads // SSD_GROUPS) * SSD_HEADDIM
    ts_pad = SSD_CHUNK
    pps = _pick(n_pages, 32)
    eye_kv = jnp.eye(ATTN_KV_HEADS, dtype=F32)

    xp, xs_ = x_prompt, x_sample.reshape(1, bs * ts, d)
    sh1 = mods(0, 0)
    sc1 = mods(0, 1)
    up = _norm_mod(xp, g_pre_mix[0:1], sc1[0], sh1[0])
    us = _norm_mod(xs_, g_pre_mix[0:1], sc1[1], sh1[1])

    outs = {n: [] for n in ("kp", "vp", "hp", "cp", "pp", "ks", "vs", "hs", "cs", "ps")}
    for layer in range(depth):
        gt1, sh2, sc2, gt2 = mods(layer, 2), mods(layer, 3), mods(layer, 4), mods(layer, 5)
        nxt = (layer + 1) % depth
        sh1n, sc1n = mods(nxt, 0), mods(nxt, 1)

        proj2, proj2_s, dt_p, dt_s = project(up.reshape(bp * tp, d), us.reshape(bs * ts, d), layer)

        proj = proj2.reshape(bp, tp, -1)
        k_p, v_p = proj[:, :, col_k:col_v], proj[:, :, col_v:col_gate]
        pool_o = _pool(proj, jnp.zeros((bp, POOL_HIST, d), F32), w_grp16, pool_scale[layer:layer + 1], layer, 0)
        ssd_o, h_p = _ssd(proj, dt_p.reshape(bp, tp, LANES), jnp.zeros((bp, CONV_HIST, conv_dim), F32),
                          jnp.zeros((bp, SSD_GROUPS, hg_w, SSD_STATE), F32), ssd_par, layer, col_z, col_xbc, tp)
        attn_o = _moba_prompt(proj, col_q, col_k, col_v)
        merged = _merge(pool_o.reshape(bp * tp, d), ssd_o.reshape(bp * tp, inner), attn_o.reshape(bp * tp, attn_w),
                        w_pool16, w_ssd16, w_attn16, proj2, col_gate, layer)
        xp, u_ffn_p = _mm_res(merged.reshape(bp, tp, d), w_out16, layer, xp, gt1[0], g_post_mix[layer:layer + 1],
                              g_pre_ffn[layer:layer + 1], sc2[0], sh2[0])
        outs["kp"].append(k_p.reshape(bp, tp, ATTN_KV_HEADS, hd))
        outs["vp"].append(v_p.reshape(bp, tp, ATTN_KV_HEADS, hd))
        outs["hp"].append(h_p.reshape(bp, n_heads, SSD_HEADDIM, SSD_STATE))
        outs["cp"].append(proj[:, tp - (SSD_CONV - 1):, col_xbc:col_xbc + conv_dim])
        outs["pp"].append(proj[:, tp - (POOL_HIST - 1):, :d])

        proj2 = proj2_s
        proj = proj2.reshape(bs, ts, -1)
        k_s, v_s, dt_s = proj[:, :, col_k:col_v], proj[:, :, col_v:col_gate], dt_s.reshape(bs, ts, LANES)
        pool_buf = jnp.pad(state_pool[layer], ((0, 0), (1, 0), (0, 0)))
        pool_o = _pool(proj, pool_buf, w_grp16, pool_scale[layer:layer + 1], layer, past)
        tpad = ((0, 0), (0, ts_pad - ts), (0, 0))
        conv_prev = jnp.pad(state_conv[layer], ((0, 0), (CONV_HIST - (SSD_CONV - 1), 0), (0, 0)))
        ssd_o, h_s = _ssd(jnp.pad(proj[:, :, :col_q], tpad), jnp.pad(dt_s, tpad), conv_prev,
                          state_ssm[layer].reshape(bs, SSD_GROUPS, hg_w, SSD_STATE), ssd_par, layer,
                          col_z, col_xbc, ts)
        ssd_o = ssd_o[:, :ts]
        q5 = proj[:, :, col_q:col_k].reshape(bs, ts, ATTN_KV_HEADS, ATTN_GROUP, hd)
        wq_t = jnp.einsum("btkgd,kj->bjdkgt", q5, eye_kv).reshape(bs, kv_w, LANES).astype(BF16)
        cache_k4 = cache_k.reshape(depth, cache_k.shape[1], page * ATTN_KV_HEADS, hd)
        cache_v4 = cache_v.reshape(depth, cache_v.shape[1], page * ATTN_KV_HEADS, hd)
        scores = _moba_scores(cache_k4, layer, page_table, wq_t, pps)
        probs = _moba_softmax(scores, k_s, wq_t, ts)
        o_full = _moba_pv(cache_v4, layer, page_table, probs, jnp.pad(v_s, ((0, 0), (0, LANES - ts), (0, 0))), pps)
        o6 = o_full.reshape(bs, ATTN_KV_HEADS, ATTN_GROUP, ts, ATTN_KV_HEADS, hd)
        attn_o = jnp.einsum("bkgtjd,kj->btkgd", o6, eye_kv).reshape(bs * ts, attn_w).astype(BF16)
        merged = _merge(pool_o.reshape(bs * ts, d), ssd_o.reshape(bs * ts, inner), attn_o,
                        w_pool16, w_ssd16, w_attn16, proj2, col_gate, layer)
        xs_, u_ffn_s = _mm_res(merged.reshape(1, bs * ts, d), w_out16, layer, xs_, gt1[1], g_post_mix[layer:layer + 1],
                               g_pre_ffn[layer:layer + 1], sc2[1], sh2[1])

        hid_p = _ffn_hidden(u_ffn_p.reshape(bp * tp, d), w_gate, w_up, layer)
        hid_s = _ffn_hidden(u_ffn_s.reshape(bs * ts, d), w_gate, w_up, layer)
        xp, up = _mm_res(hid_p.reshape(bp, tp, -1), w_down16, layer, xp, gt2[0], g_post_ffn[layer:layer + 1],
                         g_pre_mix[nxt:nxt + 1], sc1n[0], sh1n[0])
        xs_, us = _mm_res(hid_s.reshape(1, bs * ts, -1), w_down16, layer, xs_, gt2[1], g_post_ffn[layer:layer + 1],
                          g_pre_mix[nxt:nxt + 1], sc1n[1], sh1n[1])
        outs["ks"].append(k_s.reshape(bs, ts, ATTN_KV_HEADS, hd))
        outs["vs"].append(v_s.reshape(bs, ts, ATTN_KV_HEADS, hd))
        outs["hs"].append(h_s.reshape(bs, n_heads, SSD_HEADDIM, SSD_STATE))
        conv_ext = jnp.concatenate([state_conv[layer], proj[:, :, col_xbc:col_xbc + conv_dim]], axis=1)
        outs["cs"].append(conv_ext[:, ts:])
        pool_ext = jnp.concatenate([state_pool[layer], proj[:, :, :d]], axis=1)
        outs["ps"].append(pool_ext[:, ts:])

    st = {n: jnp.stack(v) for n, v in outs.items()}
    return (xp, xs_.reshape(bs, ts, d), st["kp"], st["vp"], st["hp"], st["cp"], st["pp"],
            st["ks"], st["vs"], st["hs"], st["cs"], st["ps"])
```

```python
import functools

import jax
import jax.numpy as jnp
from jax import lax
from jax.experimental import pallas as pl
from jax.experimental.pallas import tpu as pltpu

F32 = jnp.float32
BF16 = jnp.bfloat16

NORM_EPS = 1e-6
POOL_WINDOWS = (2, 4, 8, 16)
POOL_HIST = 16
SSD_HEADDIM = 64
SSD_GROUPS = 8
SSD_STATE = 128
SSD_CONV = 4
SSD_CHUNK = 128
SSD_GROUPS_PER_STEP = 4
CONV_HIST = 8
ATTN_HEADS = 16
ATTN_KV_HEADS = 8
ATTN_GROUP = ATTN_HEADS // ATTN_KV_HEADS
MOBA_BLOCK = 256
MOBA_TOPK = 3
PAST_GROUP = 2
LANES = 128
VMEM_LIMIT_BYTES = 56 * 1024 * 1024
NEG_INF = float("-inf")
LOG2E = 1.4426950408889634


def _cparams(*sem):
    return pltpu.CompilerParams(dimension_semantics=sem, vmem_limit_bytes=VMEM_LIMIT_BYTES)


def _sigmoid(x):
    return 0.5 * jnp.tanh(0.5 * x) + 0.5


def _silu(x):
    return x * _sigmoid(x)


def _dot(a, b):
    return jnp.dot(a, b, preferred_element_type=F32)


def _dot_nt(a, b):
    return lax.dot_general(a, b, (((1,), (1,)), ((), ())), preferred_element_type=F32)


def _dot_tn(a, b):
    return lax.dot_general(a, b, (((0,), (0,)), ((), ())), preferred_element_type=F32)


def _rms(x):
    return x * lax.rsqrt(jnp.mean(x * x, axis=-1, keepdims=True) + NORM_EPS)


def _pick(total, pref):
    if total <= pref:
        return total
    t = pref
    while total % t:
        t //= 2
    return t


def _ada_kernel(c_ref, w_ref, b_ref, o_ref):
    a = _silu(c_ref[...]).astype(BF16)
    o_ref[...] = _dot(a, w_ref[...].astype(BF16)) + b_ref[...]


def _ada(c_all, w_ada, b_ada):
    depth, d, n = w_ada.shape
    rows = c_all.shape[0]
    tn = _pick(n, 1024)
    return pl.pallas_call(
        _ada_kernel,
        grid=(depth, n // tn),
        in_specs=[pl.BlockSpec((rows, d), lambda l, j: (0, 0)),
                  pl.BlockSpec((None, d, tn), lambda l, j: (l, 0, j)),
                  pl.BlockSpec((None, 1, tn), lambda l, j: (l, 0, j))],
        out_specs=pl.BlockSpec((None, rows, tn), lambda l, j: (l, 0, j)),
        out_shape=jax.ShapeDtypeStruct((depth, rows, n), F32),
        compiler_params=_cparams("parallel", "parallel"),
        name="ada_mod",
    )(c_all, w_ada, b_ada.reshape(depth, 1, n))


def _norm_mod_kernel(x_ref, g_ref, sc_ref, sh_ref, u_ref):
    y = _rms(x_ref[...]) * g_ref[...]
    u_ref[...] = (y * (1.0 + sc_ref[...]) + sh_ref[...]).astype(BF16)


def _norm_mod(x, g, sc, sh):
    b, t, d = x.shape
    r = sc.shape[1]
    tt = _pick(t, 512)
    mod_spec = pl.BlockSpec((None, r if r == 1 else tt, d),
                            (lambda i, j: (i, j, 0)) if r == t else (lambda i, j: (i, 0, 0)))
    return pl.pallas_call(
        _norm_mod_kernel,
        grid=(b, t // tt),
        in_specs=[pl.BlockSpec((None, tt, d), lambda i, j: (i, j, 0)),
                  pl.BlockSpec((1, d), lambda i, j: (0, 0)),
                  mod_spec, mod_spec],
        out_specs=pl.BlockSpec((None, tt, d), lambda i, j: (i, j, 0)),
        out_shape=jax.ShapeDtypeStruct((b, t, d), BF16),
        compiler_params=_cparams("parallel", "parallel"),
        name="norm_mod",
    )(x, g, sc, sh)


def _mm_kernel(a_ref, w_ref, o_ref, w16_ref):
    @pl.when(pl.program_id(1) == 0)
    def _():
        w16_ref[...] = w_ref[0].astype(BF16)

    o_ref[...] = _dot_nt(a_ref[...], w16_ref[...])


def _matmul_wt(a, w_t, layer, row0, n, skip_at, skip):
    m, k = a.shape
    tm = _pick(m, 1024)
    tn = _pick(n, 1024)
    assert skip_at % tn == 0
    hole = skip_at // tn

    def w_index(j, i):
        return layer, pl.multiple_of(row0 + j * tn + jnp.where(j >= hole, skip, 0), 8), 0

    return pl.pallas_call(
        _mm_kernel,
        grid=(n // tn, m // tm),
        in_specs=[pl.BlockSpec((tm, k), lambda j, i: (i, 0)),
                  pl.BlockSpec((pl.Element(1), pl.Element(tn), pl.Element(k)), w_index)],
        out_specs=pl.BlockSpec((tm, tn), lambda j, i: (i, j)),
        out_shape=jax.ShapeDtypeStruct((m, n), F32),
        scratch_shapes=[pltpu.VMEM((tn, k), BF16)],
        compiler_params=_cparams("parallel", "arbitrary"),
        name="proj_matmul",
    )(a, w_t)


def _pool_kernel(p_ref, buf_ref, w_ref, sc_ref, o_ref, ext_ref, *, tt, start, pg):
    t = pl.program_id(1)

    @pl.when(t == 0)
    def _():
        ext_ref[0:POOL_HIST, :] = buf_ref[...]

    ext_ref[POOL_HIST:POOL_HIST + tt, :] = p_ref[...]
    n_valid = lax.broadcasted_iota(jnp.int32, (tt, 1), 0) + (t * tt + start + 1)
    for gi, w in enumerate(POOL_WINDOWS):
        c0 = gi * pg
        e = ext_ref[:, c0:c0 + pg]
        x = e[POOL_HIST:]
        s = e
        k = 1
        while k < w:
            s = s + pltpu.roll(s, k, axis=0)
            k *= 2
        s = s[POOL_HIST:]
        cnt = jnp.minimum(n_valid, w).astype(F32)
        mixed = (s / cnt - x).astype(BF16)
        y = _dot(mixed, w_ref[gi]) * sc_ref[:, c0:c0 + pg]
        o_ref[:, c0:c0 + pg] = y.astype(BF16)
    ext_ref[0:POOL_HIST, :] = ext_ref[tt:tt + POOL_HIST, :]


def _pool(proj3, buf, w_grp, scale, layer, start):
    b, t, _ = proj3.shape
    c = scale.shape[1]
    pg = c // len(POOL_WINDOWS)
    tt = _pick(t, 256)
    return pl.pallas_call(
        functools.partial(_pool_kernel, tt=tt, start=start, pg=pg),
        grid=(b, t // tt),
        in_specs=[pl.BlockSpec((None, tt, c), lambda i, j: (i, j, 0)),
                  pl.BlockSpec((None, POOL_HIST, c), lambda i, j: (i, 0, 0)),
                  pl.BlockSpec((None, len(POOL_WINDOWS), pg, pg), lambda i, j: (layer, 0, 0, 0)),
                  pl.BlockSpec((1, c), lambda i, j: (0, 0))],
        out_specs=pl.BlockSpec((None, tt, c), lambda i, j: (i, j, 0)),
        out_shape=jax.ShapeDtypeStruct((b, t, c), BF16),
        scratch_shapes=[pltpu.VMEM((POOL_HIST + tt, c), F32)],
        compiler_params=_cparams("parallel", "arbitrary"),
        name="pool_mixer",
    )(proj3, buf, w_grp, scale)


def _ssd_kernel(z_ref, x_ref, b_ref, c_ref, dt_ref, cpx_ref, cpb_ref, cpc_ref,
                px_ref, pb_ref, pc_ref, pdt_ref, exp_ref, h0_ref,
                y_ref, hout_ref, histx, histb, histc, ht, *, t_valid, n_chunks, hg, gps):
    gq = pl.program_id(1)
    c = pl.program_id(2)
    lc = SSD_CHUNK
    gw = hg * SSD_HEADDIM
    ns = SSD_STATE

    @pl.when(c == 0)
    def _():
        histx[...] = cpx_ref[...]
        histb[...] = cpb_ref[...]
        histc[...] = cpc_ref[...]
        for s in range(gps):
            ht[s] = h0_ref[s].T

    def conv(hist, raw_ref, par_ref):
        x = raw_ref[...]
        prev = hist[...]
        hist[...] = x[lc - CONV_HIST:, :]
        row8 = lax.broadcasted_iota(jnp.int32, prev.shape, 0)
        acc = par_ref[SSD_CONV:SSD_CONV + 1, :] + x * par_ref[SSD_CONV - 1:SSD_CONV, :]
        for k in range(1, SSD_CONV):
            r = pltpu.roll(x, k, axis=0)
            first = jnp.where(row8 < k, pltpu.roll(prev, k, axis=0), r[0:CONV_HIST])
            shifted = jnp.concatenate([first, r[CONV_HIST:]], axis=0)
            acc = acc + shifted * par_ref[SSD_CONV - 1 - k:SSD_CONV - k, :]
        return _silu(acc)

    xs_all = conv(histx, x_ref, px_ref)
    bm_all = conv(histb, b_ref, pb_ref)
    cm_all = conv(histc, c_ref, pc_ref)

    row = lax.broadcasted_iota(jnp.int32, (lc, LANES), 0)
    col = lax.broadcasted_iota(jnp.int32, (lc, LANES), 1)
    xdt = dt_ref[...] + pdt_ref[0:1, :]
    dt = jnp.maximum(xdt, 0.0) + jnp.log1p(jnp.exp(-jnp.abs(xdt)))
    dt = jnp.where(c * lc + row < t_valid, dt, 0.0)
    dta = dt * (-jnp.exp(pdt_ref[1:2, :]))
    acs = dta
    k = 1
    while k < lc:
        acs = acs + jnp.where(row >= k, pltpu.roll(acs, k, axis=0), 0.0)
        k *= 2
    a_last = acs[lc - 1:lc, :]
    w_end = dt * jnp.exp(a_last - acs)
    exp_a = jnp.exp(acs)

    cd = jnp.exp(a_last)
    cd_hi = cd.astype(BF16).astype(F32)
    cd_mid = (cd - cd_hi).astype(BF16).astype(F32)
    cd_lo = cd - cd_hi - cd_mid
    pieces = [jnp.broadcast_to(p, (16, LANES)) for p in (cd_hi, cd_mid, cd_lo)]
    stacked = jnp.concatenate([dt, w_end, exp_a] + pieces, axis=0).astype(BF16)
    acs2 = acs * LOG2E
    causal = row >= col
    heads_here = LANES // SSD_HEADDIM

    for s in range(gps):
        xl, nl = slice(s * gw, (s + 1) * gw), slice(s * ns, (s + 1) * ns)
        xs, bm, cm = xs_all[:, xl], bm_all[:, nl], cm_all[:, nl]
        wide = _dot(stacked, exp_ref[s])
        dt_e, w_e, ea_e = wide[0:lc], wide[lc:2 * lc], wide[2 * lc:3 * lc]
        r0 = 3 * lc
        cd_e = wide[r0:r0 + 1] + wide[r0 + 16:r0 + 17] + wide[r0 + 32:r0 + 33]

        acs_g = pltpu.roll(acs2, (LANES - (gq * gps + s) * hg) % LANES, axis=1)
        acs_gt = acs_g.T

        bm16 = bm.astype(BF16)
        cm16 = cm.astype(BF16)
        cb = jnp.where(causal, _dot_nt(cm16, bm16), 0.0)
        xdt_e = xs * dt_e
        y_parts = []
        for pr in range(gw // LANES):
            xp = xdt_e[:, pr * LANES:(pr + 1) * LANES]
            lhs, rhs = [], []
            for hh in range(heads_here):
                h = pr * heads_here + hh
                seg = acs_g[:, h:h + 1] - acs_gt[h:h + 1, :]
                lhs.append((cb * jnp.exp2(jnp.minimum(seg, 0.0))).astype(BF16))
                in_head = (col >= hh * SSD_HEADDIM) & (col < (hh + 1) * SSD_HEADDIM)
                rhs.append(jnp.where(in_head, xp, 0.0).astype(BF16))
            y_parts.append(_dot(jnp.concatenate(lhs, axis=1), jnp.concatenate(rhs, axis=0)))
        y = jnp.concatenate(y_parts, axis=1)

        h_in = ht[s]
        y = y + _dot(cm16, h_in.astype(BF16)) * ea_e
        h_new = h_in * cd_e + _dot(bm.T.astype(BF16), (xs * w_e).astype(BF16))
        ht[s] = h_new

        y = y + px_ref[SSD_CONV + 1:SSD_CONV + 2, xl] * xs
        y = y * _silu(z_ref[:, xl])
        y_ref[:, xl] = (_rms(y) * px_ref[SSD_CONV + 2:SSD_CONV + 3, xl]).astype(BF16)

    @pl.when(c == n_chunks - 1)
    def _():
        for s in range(gps):
            hout_ref[s] = ht[s].T


def _ssd_params(conv_w, conv_b, dt_bias, a_log, d_skip, ssd_norm):
    depth, inner = ssd_norm.shape
    n_heads = dt_bias.shape[1]
    rows = lambda v: v[:, None, :]
    px = jnp.concatenate([conv_w[:, :, :inner], rows(conv_b[:, :inner]), rows(jnp.repeat(d_skip, SSD_HEADDIM, axis=1)),
                          rows(ssd_norm), jnp.zeros((depth, 8 - SSD_CONV - 3, inner), F32)], axis=1)
    pbc = jnp.concatenate([conv_w[:, :, inner:], rows(conv_b[:, inner:]),
                           jnp.zeros((depth, 8 - SSD_CONV - 1, conv_w.shape[2] - inner), F32)], axis=1)
    pdt = jnp.pad(jnp.stack([dt_bias, a_log], axis=1), ((0, 0), (0, 6), (0, LANES - n_heads)))
    return px, pbc, pdt


def _ssd(proj3, dt3, conv_prev, h0, params, layer, col_z, col_xbc, t_valid):
    b, tp, _ = proj3.shape
    px, pbc, pdt = params
    inner = px.shape[2]
    n_heads = inner // SSD_HEADDIM
    hg = n_heads // SSD_GROUPS
    gw = hg * SSD_HEADDIM
    ns = SSD_STATE
    lc = SSD_CHUNK
    n_chunks = tp // lc
    gps = SSD_GROUPS_PER_STEP
    xw, nw = gps * gw, gps * ns
    assert col_z % xw == 0 and col_xbc % xw == 0 and (col_xbc + inner) % nw == 0 and SSD_GROUPS % gps == 0
    zb, xb = col_z // xw, col_xbc // xw
    bb = (col_xbc + inner) // nw
    cb_ = (col_xbc + inner + SSD_GROUPS * ns) // nw
    cxb, cbb, ccb = 0, inner // nw, (inner + SSD_GROUPS * ns) // nw
    head_of = (jnp.arange(SSD_GROUPS)[:, None] * hg + jnp.arange(gw)[None, :] // SSD_HEADDIM)
    expand = (jnp.arange(LANES)[None, :, None] == head_of[:, None, :]).astype(BF16)

    def seq(width, off):
        return pl.BlockSpec((None, lc, width), lambda i, g, c: (i, c, off + g))

    def prev(width, off):
        return pl.BlockSpec((None, CONV_HIST, width), lambda i, g, c: (i, 0, off + g))

    def par(width, off):
        return pl.BlockSpec((None, 8, width), lambda i, g, c: (layer, 0, off + g))

    state_spec = pl.BlockSpec((None, gps, gw, ns), lambda i, g, c: (i, g, 0, 0))
    kernel = functools.partial(_ssd_kernel, t_valid=t_valid, n_chunks=n_chunks, hg=hg, gps=gps)
    return pl.pallas_call(
        kernel,
        grid=(b, SSD_GROUPS // gps, n_chunks),
        in_specs=[seq(xw, zb), seq(xw, xb), seq(nw, bb), seq(nw, cb_),
                  pl.BlockSpec((None, lc, LANES), lambda i, g, c: (i, c, 0)),
                  prev(xw, cxb), prev(nw, cbb), prev(nw, ccb),
                  par(xw, 0), par(nw, 0), par(nw, SSD_GROUPS // gps),
                  pl.BlockSpec((None, 8, LANES), lambda i, g, c: (layer, 0, 0)),
                  pl.BlockSpec((gps, LANES, gw), lambda i, g, c: (g, 0, 0)),
                  state_spec],
        out_specs=[pl.BlockSpec((None, lc, xw), lambda i, g, c: (i, c, g)), state_spec],
        out_shape=[jax.ShapeDtypeStruct((b, tp, inner), BF16),
                   jax.ShapeDtypeStruct((b, SSD_GROUPS, gw, ns), F32)],
        scratch_shapes=[pltpu.VMEM((CONV_HIST, xw), F32),
                        pltpu.VMEM((CONV_HIST, nw), F32),
                        pltpu.VMEM((CONV_HIST, nw), F32),
                        pltpu.VMEM((gps, ns, gw), F32)],
        compiler_params=_cparams("parallel", "parallel", "arbitrary"),
        name="ssd_mixer",
    )(proj3, proj3, proj3, proj3, dt3, conv_prev, conv_prev, conv_prev, px, pbc, pbc, pdt, expand, h0)


def _top_k_mask(gate, valid, ids, axis):
    big = jnp.int32(2 ** 30)
    remaining = valid
    sel = jnp.zeros(gate.shape, F32)
    for _ in range(MOBA_TOPK):
        gm = jnp.where(remaining > 0.5, gate, NEG_INF)
        m = jnp.max(gm, axis=axis, keepdims=True)
        cand = jnp.where(gm == m, remaining, 0.0)
        first = jnp.min(jnp.where(cand > 0.5, ids, big), axis=axis, keepdims=True)
        pick = jnp.where(ids == first, 1.0, 0.0)
        sel = sel + pick
        remaining = remaining - pick
    return sel


def _moba_prompt_kernel(q_ref, k_ref, v_ref, o_ref, kb, vtb, km, sel_ref, qs_ref, acc_ref, ml_ref, sa_ref, sb_ref,
                        *, n_blocks, scale):
    i = pl.program_id(2)
    blk = MOBA_BLOCK
    hd = LANES
    nq = ATTN_GROUP * blk
    c2 = scale * LOG2E

    @pl.when(i == 0)
    def _():
        km[...] = jnp.zeros(km.shape, F32)
        for j in range(n_blocks):
            kj = k_ref[j * blk:(j + 1) * blk, :]
            kb[j] = kj.astype(BF16)
            km[j:j + 1, :] = jnp.mean(kj, axis=0, keepdims=True)
            vtb[j] = v_ref[j * blk:(j + 1) * blk, :].T.astype(BF16)

    grp = PAST_GROUP

    def block_at(pos):
        return jnp.where(pos == 0, i, jnp.minimum(pos - 1, n_blocks - 1))

    def scores(s_ref, p0, own_first=False):
        for k in range(grp):
            s = _dot_nt(kb[block_at(p0 + k)], qs_ref[...])
            if own_first and k == 0:
                key_r = lax.broadcasted_iota(jnp.int32, (blk, nq), 0)
                qry_c = lax.broadcasted_iota(jnp.int32, (blk, nq), 1) & (blk - 1)
                s = jnp.where(key_r <= qry_c, s, NEG_INF)
            s_ref[k] = s

    q = q_ref[...]
    q = jnp.concatenate([q[:, g * hd:(g + 1) * hd] for g in range(ATTN_GROUP)], axis=0)
    qs_ref[...] = (q * c2).astype(BF16)
    scores(sa_ref, 0, own_first=True)
    gate_t = _dot_nt(km[...].astype(BF16), q.astype(BF16))
    ids = lax.broadcasted_iota(jnp.int32, (km.shape[0], nq), 0)
    valid = jnp.where(ids < i, 1.0, 0.0)
    sel_ref[...] = _top_k_mask(gate_t, valid, ids, axis=0)
    ml_ref[0:1, :] = jnp.full((1, nq), NEG_INF, F32)
    ml_ref[1:2, :] = jnp.zeros((1, nq), F32)
    acc_ref[...] = jnp.zeros(acc_ref.shape, F32)

    def update(s_ref, p0):
        m_old, l_old = ml_ref[0:1, :], ml_ref[1:2, :]
        picked = [sel_ref[pl.ds(jnp.maximum(p0 + k - 1, 0), 1), :] + jnp.where(p0 + k == 0, 1.0, 0.0) > 0.5
                  for k in range(grp)]
        m_new = m_old
        for k in range(grp):
            m_new = jnp.maximum(m_new, jnp.where(picked[k], jnp.max(s_ref[k], axis=0, keepdims=True), NEG_INF))
        alpha = jnp.exp2(m_old - m_new)
        l_new = alpha * l_old
        pv = None
        for k in range(grp):
            p = jnp.exp2(s_ref[k] - jnp.where(picked[k], m_new, jnp.inf))
            l_new = l_new + jnp.sum(p, axis=0, keepdims=True)
            d = _dot(vtb[block_at(p0 + k)], p.astype(BF16))
            pv = d if pv is None else pv + d
        acc_ref[...] = alpha * acc_ref[...] + pv
        ml_ref[0:1, :] = m_new
        ml_ref[1:2, :] = l_new

    n_groups = (i + grp) // grp

    def pair_body(u, carry):
        p0 = 2 * u * grp
        scores(sb_ref, p0 + grp)
        update(sa_ref, p0)
        scores(sa_ref, p0 + 2 * grp)
        update(sb_ref, p0 + grp)
        return carry

    lax.fori_loop(0, n_groups // 2, pair_body, 0)

    @pl.when((n_groups & 1) != 0)
    def _():
        update(sa_ref, (n_groups - 1) * grp)

    o = (acc_ref[...] / ml_ref[1:2, :]).T
    o_ref[...] = jnp.concatenate([o[g * blk:(g + 1) * blk] for g in range(ATTN_GROUP)], axis=1).astype(BF16)


def _moba_prompt(proj3, col_q, col_k, col_v):
    b, t, _ = proj3.shape
    hd = LANES
    blk = MOBA_BLOCK
    n_blocks = t // blk
    nb_pad = -(-(n_blocks + PAST_GROUP) // 16) * 16
    qw = ATTN_GROUP * hd
    qb, kb0, vb0 = col_q // qw, col_k // hd, col_v // hd
    kernel = functools.partial(_moba_prompt_kernel, n_blocks=n_blocks, scale=hd ** -0.5)
    return pl.pallas_call(
        kernel,
        grid=(b, ATTN_KV_HEADS, n_blocks),
        in_specs=[pl.BlockSpec((None, blk, qw), lambda bi, h, i: (bi, i, qb + h)),
                  pl.BlockSpec((None, t, hd), lambda bi, h, i: (bi, 0, kb0 + h)),
                  pl.BlockSpec((None, t, hd), lambda bi, h, i: (bi, 0, vb0 + h))],
        out_specs=pl.BlockSpec((None, blk, qw), lambda bi, h, i: (bi, i, h)),
        out_shape=jax.ShapeDtypeStruct((b, t, ATTN_HEADS * hd), BF16),
        scratch_shapes=[pltpu.VMEM((n_blocks, blk, hd), BF16), pltpu.VMEM((n_blocks, hd, blk), BF16),
                        pltpu.VMEM((nb_pad, hd), F32), pltpu.VMEM((nb_pad, ATTN_GROUP * blk), F32),
                        pltpu.VMEM((ATTN_GROUP * blk, hd), BF16), pltpu.VMEM((hd, ATTN_GROUP * blk), F32),
                        pltpu.VMEM((8, ATTN_GROUP * blk), F32),
                        pltpu.VMEM((PAST_GROUP, blk, ATTN_GROUP * blk), F32),
                        pltpu.VMEM((PAST_GROUP, blk, ATTN_GROUP * blk), F32)],
        compiler_params=_cparams("parallel", "parallel", "arbitrary"),
        name="moba_prompt",
    )(proj3, proj3, proj3)


def _moba_scores_kernel(pt_ref, wq_ref, *refs, pps, page):
    del pt_ref
    k_refs, o_ref = refs[:pps], refs[pps]
    hd = LANES
    for r in range(pps):
        acc = None
        for h in range(ATTN_KV_HEADS):
            kh = k_refs[r][pl.ds(h, page, stride=ATTN_KV_HEADS), :].astype(BF16)
            part = _dot(kh, wq_ref[h * hd:(h + 1) * hd, :])
            acc = part if acc is None else acc + part
        o_ref[r * page:(r + 1) * page, :] = acc


def _moba_scores(cache4, layer, page_table, wq_t, pps):
    b, n_pages = page_table.shape
    rows, hd = cache4.shape[2], cache4.shape[3]
    page, width = rows // ATTN_KV_HEADS, hd * ATTN_KV_HEADS

    def page_spec(r):
        return pl.BlockSpec((None, None, rows, hd), lambda bi, p, pt: (layer, pt[bi, p * pps + r], 0, 0))

    grid_spec = pltpu.PrefetchScalarGridSpec(
        num_scalar_prefetch=1,
        grid=(b, n_pages // pps),
        in_specs=[pl.BlockSpec((None, width, LANES), lambda bi, p, pt: (bi, 0, 0))]
                 + [page_spec(r) for r in range(pps)],
        out_specs=pl.BlockSpec((None, pps * page, LANES), lambda bi, p, pt: (bi, p, 0)),
    )
    return pl.pallas_call(
        functools.partial(_moba_scores_kernel, pps=pps, page=page),
        grid_spec=grid_spec,
        out_shape=jax.ShapeDtypeStruct((b, n_pages * page, LANES), F32),
        compiler_params=_cparams("parallel", "arbitrary"),
        name="moba_sample_scores",
    )(page_table, wq_t, *([cache4] * pps))


def _moba_softmax_kernel(st_ref, kn_ref, wq_ref, p_ref, gate_ref, sel_ref, *, n_blocks, tq, scale):
    blk = MOBA_BLOCK
    past = n_blocks * blk
    gate_ref[...] = jnp.zeros(gate_ref.shape, F32)

    def gate_body(j, carry):
        off = pl.multiple_of(j * blk, blk)
        gate_ref[pl.ds(j, 1), :] = jnp.mean(st_ref[pl.ds(off, blk), :], axis=0, keepdims=True)
        return carry

    lax.fori_loop(0, n_blocks, gate_body, 0)
    nrow = gate_ref.shape[0]
    ids = lax.broadcasted_iota(jnp.int32, (nrow, LANES), 0)
    valid = jnp.where(ids < n_blocks, 1.0, 0.0)
    sel_ref[...] = _top_k_mask(gate_ref[...], valid, ids, axis=0)

    s_own = _dot(kn_ref[...].astype(BF16), wq_ref[...]) * scale
    key_t = lax.broadcasted_iota(jnp.int32, (tq, LANES), 0)
    qry_t = lax.broadcasted_iota(jnp.int32, (tq, LANES), 1) & (tq - 1)
    s_own = jnp.where(key_t <= qry_t, s_own, NEG_INF)
    m0 = jnp.max(s_own, axis=0, keepdims=True)

    def max_body(j, m):
        off = pl.multiple_of(j * blk, blk)
        bm = jnp.max(st_ref[pl.ds(off, blk), :], axis=0, keepdims=True) * scale
        return jnp.maximum(m, jnp.where(sel_ref[pl.ds(j, 1), :] > 0.5, bm, NEG_INF))

    m = lax.fori_loop(0, n_blocks, max_body, m0)
    p_own = jnp.exp(s_own - m)
    l0 = jnp.sum(p_own, axis=0, keepdims=True)

    def exp_body(j, l):
        off = pl.multiple_of(j * blk, blk)
        s = st_ref[pl.ds(off, blk), :] * scale
        p = jnp.where(sel_ref[pl.ds(j, 1), :] > 0.5, jnp.exp(s - m), 0.0)
        p_ref[pl.ds(off, blk), :] = p
        return l + jnp.sum(p, axis=0, keepdims=True)

    l = lax.fori_loop(0, n_blocks, exp_body, l0)
    inv = 1.0 / l

    def norm_body(j, carry):
        off = pl.multiple_of(j * blk, blk)
        p_ref[pl.ds(off, blk), :] = p_ref[pl.ds(off, blk), :] * inv
        return carry

    lax.fori_loop(0, n_blocks, norm_body, 0)
    p_ref[past:past + LANES, :] = jnp.zeros((LANES, LANES), F32)
    p_ref[past:past + tq, :] = p_own * inv


def _moba_softmax(scores, k_new, wq_t, tq):
    b, past, _ = scores.shape
    width = k_new.shape[2]
    n_blocks = past // MOBA_BLOCK
    nrow = -(-n_blocks // 8) * 8
    kernel = functools.partial(_moba_softmax_kernel, n_blocks=n_blocks, tq=tq, scale=LANES ** -0.5)
    return pl.pallas_call(
        kernel,
        grid=(b,),
        in_specs=[pl.BlockSpec((None, past, LANES), lambda bi: (bi, 0, 0)),
                  pl.BlockSpec((None, tq, width), lambda bi: (bi, 0, 0)),
                  pl.BlockSpec((None, width, LANES), lambda bi: (bi, 0, 0))],
        out_specs=pl.BlockSpec((None, past + LANES, LANES), lambda bi: (bi, 0, 0)),
        out_shape=jax.ShapeDtypeStruct((b, past + LANES, LANES), F32),
        scratch_shapes=[pltpu.VMEM((nrow, LANES), F32), pltpu.VMEM((nrow, LANES), F32)],
        compiler_params=_cparams("parallel"),
        name="moba_sample_softmax",
    )(scores, k_new, wq_t)


def _moba_pv_kernel(pt_ref, p_ref, pown_ref, vnew_ref, *refs, pps, page):
    del pt_ref
    v_refs, o_ref = refs[:pps], refs[pps]

    @pl.when(pl.program_id(1) == 0)
    def _():
        o_ref[...] = _dot_tn(pown_ref[...].astype(BF16), vnew_ref[...].astype(BF16))

    hd = LANES
    for h in range(ATTN_KV_HEADS):
        acc = o_ref[:, h * hd:(h + 1) * hd]
        for r in range(pps):
            vh = v_refs[r][pl.ds(h, page, stride=ATTN_KV_HEADS), :].astype(BF16)
            acc = acc + _dot_tn(p_ref[r * page:(r + 1) * page, :].astype(BF16), vh)
        o_ref[:, h * hd:(h + 1) * hd] = acc


def _moba_pv(cache4, layer, page_table, probs, v_new_pad, pps):
    b, n_pages = page_table.shape
    rows, hd = cache4.shape[2], cache4.shape[3]
    page, width = rows // ATTN_KV_HEADS, hd * ATTN_KV_HEADS
    own_blk = (n_pages * page) // LANES

    def page_spec(r):
        return pl.BlockSpec((None, None, rows, hd), lambda bi, p, pt: (layer, pt[bi, p * pps + r], 0, 0))

    grid_spec = pltpu.PrefetchScalarGridSpec(
        num_scalar_prefetch=1,
        grid=(b, n_pages // pps),
        in_specs=[pl.BlockSpec((None, pps * page, LANES), lambda bi, p, pt: (bi, p, 0)),
                  pl.BlockSpec((None, LANES, LANES), lambda bi, p, pt: (bi, own_blk, 0)),
                  pl.BlockSpec((None, LANES, width), lambda bi, p, pt: (bi, 0, 0))]
                 + [page_spec(r) for r in range(pps)],
        out_specs=pl.BlockSpec((None, LANES, width), lambda bi, p, pt: (bi, 0, 0)),
    )
    return pl.pallas_call(
        functools.partial(_moba_pv_kernel, pps=pps, page=page),
        grid_spec=grid_spec,
        out_shape=jax.ShapeDtypeStruct((b, LANES, width), F32),
        compiler_params=_cparams("parallel", "arbitrary"),
        name="moba_sample_pv",
    )(page_table, probs, probs, v_new_pad, *([cache4] * pps))


def _merge_kernel(ap_ref, as_ref, aa_ref, wp_ref, ws_ref, wa_ref, g0_ref, g1_ref, g2_ref, o_ref):
    acc = _sigmoid(g0_ref[...]) * _dot(ap_ref[...], wp_ref[...])
    acc = acc + _sigmoid(g1_ref[...]) * _dot(as_ref[...], ws_ref[...])
    acc = acc + _sigmoid(g2_ref[...]) * _dot(aa_ref[...], wa_ref[...])
    o_ref[...] = acc.astype(BF16)


def _merge(pool_o, ssd_o, attn_o, w_pool, w_ssd, w_attn, proj, col_gate, layer):
    m, d = pool_o.shape
    tm = _pick(m, 512)
    tn = _pick(d, 512)
    gb = col_gate // tn
    nb = d // tn

    def act(width):
        return pl.BlockSpec((tm, width), lambda j, i: (i, 0))

    def wgt(kdim):
        return pl.BlockSpec((None, kdim, tn), lambda j, i: (layer, 0, j))

    def gate(which):
        return pl.BlockSpec((tm, tn), lambda j, i: (i, gb + which * nb + j))

    return pl.pallas_call(
        _merge_kernel,
        grid=(d // tn, m // tm),
        in_specs=[act(pool_o.shape[1]), act(ssd_o.shape[1]), act(attn_o.shape[1]),
                  wgt(w_pool.shape[1]), wgt(w_ssd.shape[1]), wgt(w_attn.shape[1]),
                  gate(0), gate(1), gate(2)],
        out_specs=pl.BlockSpec((tm, tn), lambda j, i: (i, j)),
        out_shape=jax.ShapeDtypeStruct((m, d), BF16),
        compiler_params=_cparams("parallel", "parallel"),
        name="gated_merge",
    )(pool_o, ssd_o, attn_o, w_pool, w_ssd, w_attn, proj, proj, proj)


def _mm_res_kernel(a_ref, w_ref, x_ref, gt_ref, gpost_ref, gpre_ref, sc_ref, sh_ref, xo_ref, u_ref, *, nk, n_sub):
    tm = xo_ref.shape[0]
    slab = tm // n_sub

    def rows_of(ref, sl):
        return ref[...] if ref.shape[0] == 1 else ref[sl, :]

    def last_step(first):
        slabs = [slice(r * slab, (r + 1) * slab) for r in range(n_sub)]
        parts = [_dot(a_ref[sl, :], w_ref[...]) for sl in slabs]
        for sl, part in zip(slabs, parts):
            f = part if first else xo_ref[sl, :] + part
            xn = x_ref[sl, :] + rows_of(gt_ref, sl) * (_rms(f) * gpost_ref[...])
            xo_ref[sl, :] = xn
            y = _rms(xn) * gpre_ref[...]
            u_ref[sl, :] = (y * (1.0 + rows_of(sc_ref, sl)) + rows_of(sh_ref, sl)).astype(BF16)

    if nk == 1:
        last_step(True)
        return
    k = pl.program_id(2)

    @pl.when(k == 0)
    def _():
        xo_ref[...] = _dot(a_ref[...], w_ref[...])

    @pl.when((k > 0) & (k < nk - 1))
    def _():
        xo_ref[...] += _dot(a_ref[...], w_ref[...])

    @pl.when(k == nk - 1)
    def _():
        last_step(False)


def _pick_k(kdim, cap):
    for nk in range(1, kdim // LANES + 1):
        if kdim % nk == 0 and (kdim // nk) % LANES == 0 and kdim // nk <= cap:
            return kdim // nk
    return kdim


def _mm_res(a3, w, layer, x3, gt, g_post, g_pre, sc, sh):
    b, t, kdim = a3.shape
    d = x3.shape[2]
    r = gt.shape[1]
    tm = _pick(t, 512)
    tk = _pick_k(kdim, 2048)
    nk = kdim // tk
    mod_spec = pl.BlockSpec((None, r if r == 1 else tm, d),
                            (lambda bi, i, k: (bi, i, 0)) if r == t else (lambda bi, i, k: (bi, 0, 0)))
    vec_spec = pl.BlockSpec((1, d), lambda bi, i, k: (0, 0))
    row_spec = pl.BlockSpec((None, tm, d), lambda bi, i, k: (bi, i, 0))
    n_sub = 2 if tm % 512 == 0 else 1
    return pl.pallas_call(
        functools.partial(_mm_res_kernel, nk=nk, n_sub=n_sub),
        grid=(b, t // tm, nk),
        in_specs=[pl.BlockSpec((None, tm, tk), lambda bi, i, k: (bi, i, k)),
                  pl.BlockSpec((None, tk, d), lambda bi, i, k: (layer, k, 0)),
                  row_spec, mod_spec, vec_spec, vec_spec, mod_spec, mod_spec],
        out_specs=[row_spec, row_spec],
        out_shape=[jax.ShapeDtypeStruct((b, t, d), F32), jax.ShapeDtypeStruct((b, t, d), BF16)],
        compiler_params=_cparams("parallel", "parallel", "arbitrary"),
        name="matmul_residual_norm",
    )(a3, w, x3, gt, g_post, g_pre, sc, sh)


def _ffn_kernel(u_ref, wg_ref, wu_ref, o_ref, wg16_ref, wu16_ref):
    @pl.when(pl.program_id(1) == 0)
    def _():
        wg16_ref[...] = wg_ref[...].astype(BF16)
        wu16_ref[...] = wu_ref[...].astype(BF16)

    u = u_ref[...]
    o_ref[...] = (_silu(_dot(u, wg16_ref[...])) * _dot(u, wu16_ref[...])).astype(BF16)


def _ffn_hidden(u, w_gate, w_up, layer):
    m, k = u.shape
    n = w_gate.shape[2]
    tm = _pick(m, 1024)
    tn = _pick(n, 512)
    w_spec = pl.BlockSpec((None, k, tn), lambda j, i: (layer, 0, j))
    return pl.pallas_call(
        _ffn_kernel,
        grid=(n // tn, m // tm),
        in_specs=[pl.BlockSpec((tm, k), lambda j, i: (i, 0)), w_spec, w_spec],
        out_specs=pl.BlockSpec((tm, tn), lambda j, i: (i, j)),
        out_shape=jax.ShapeDtypeStruct((m, n), BF16),
        scratch_shapes=[pltpu.VMEM((k, tn), BF16), pltpu.VMEM((k, tn), BF16)],
        compiler_params=_cparams("parallel", "arbitrary"),
        name="ffn_hidden",
    )(u, w_gate, w_up)


def kernel(x_prompt, x_sample, cache_k, cache_v, state_ssm, state_conv, state_pool, page_table, c_prompt, c_sample, w_ada, b_ada, g_pre_mix, g_post_mix, g_pre_ffn, g_post_ffn, w_in, w_pool_grp, pool_scale, conv_w, conv_b, dt_bias, a_log, d_skip, ssd_norm, w_pool_br, w_ssd_br, w_attn_br, w_out, w_gate, w_up, w_down):
    bp, tp, d = x_prompt.shape
    bs, ts, _ = x_sample.shape
    depth = w_ada.shape[0]
    hd = d // ATTN_HEADS
    assert hd == LANES and ATTN_KV_HEADS * ATTN_GROUP * ts == LANES
    inner = ssd_norm.shape[1]
    n_heads = inner // SSD_HEADDIM
    conv_dim = conv_w.shape[2]
    attn_w = ATTN_HEADS * hd
    kv_w = ATTN_KV_HEADS * hd
    page = cache_k.shape[2]
    n_pages = page_table.shape[1]
    past = n_pages * page
    assert tp % MOBA_BLOCK == 0 and past % MOBA_BLOCK == 0 and n_heads <= LANES

    w_t = jnp.transpose(w_in, (0, 2, 1))
    col_z = d
    col_xbc = col_z + inner
    col_q = col_xbc + conv_dim
    col_k = col_q + attn_w
    col_v = col_k + kv_w
    col_gate = col_v + kv_w
    n_main = w_in.shape[2] - n_heads

    def project(u_p, u_s, layer):
        proj_p = _matmul_wt(u_p, w_t, layer, 0, n_main, col_q, n_heads)
        proj_s = _matmul_wt(u_s, w_t, layer, 0, n_main, col_q, n_heads)
        dt_p = _matmul_wt(u_p, w_t, layer, col_q, LANES, LANES, 0)
        dt_s = _matmul_wt(u_s, w_t, layer, col_q, LANES, LANES, 0)
        return proj_p, proj_s, dt_p, dt_s

    w_grp16 = w_pool_grp.astype(BF16)
    w_pool16, w_ssd16, w_attn16 = w_pool_br.astype(BF16), w_ssd_br.astype(BF16), w_attn_br.astype(BF16)
    w_out16, w_down16 = w_out.astype(BF16), w_down.astype(BF16)

    ssd_par = _ssd_params(conv_w, conv_b, dt_bias, a_log, d_skip, ssd_norm)

    n_c = bp + bs
    c_all = jnp.pad(jnp.concatenate([c_prompt, c_sample], axis=0), ((0, -n_c % 8), (0, 0)))
    mod = _ada(c_all, w_ada, b_ada)

    def mods(layer, which):
        m = mod[layer, :, which * d:(which + 1) * d]
        m_s = jnp.broadcast_to(m[bp:n_c, None, :], (bs, ts, d)).reshape(1, bs * ts, d)
        return m[:bp, None, :], m_s

    hg_w = (n_heads // SSD_GROUPS) * SSD_HEADDIM
    ts_pad = SSD_CHUNK
    pps = _pick(n_pages, 32)
    eye_kv = jnp.eye(ATTN_KV_HEADS, dtype=F32)

    xp, xs_ = x_prompt, x_sample.reshape(1, bs * ts, d)
    sh1 = mods(0, 0)
    sc1 = mods(0, 1)
    up = _norm_mod(xp, g_pre_mix[0:1], sc1[0], sh1[0])
    us = _norm_mod(xs_, g_pre_mix[0:1], sc1[1], sh1[1])

    outs = {n: [] for n in ("kp", "vp", "hp", "cp", "pp", "ks", "vs", "hs", "cs", "ps")}
    for layer in range(depth):
        gt1, sh2, sc2, gt2 = mods(layer, 2), mods(layer, 3), mods(layer, 4), mods(layer, 5)
        nxt = (layer + 1) % depth
        sh1n, sc1n = mods(nxt, 0), mods(nxt, 1)

        proj2, proj2_s, dt_p, dt_s = project(up.reshape(bp * tp, d), us.reshape(bs * ts, d), layer)

        proj = proj2.reshape(bp, tp, -1)
        k_p, v_p = proj[:, :, col_k:col_v], proj[:, :, col_v:col_gate]
        pool_o = _pool(proj, jnp.zeros((bp, POOL_HIST, d), F32), w_grp16, pool_scale[layer:layer + 1], layer, 0)
        ssd_o, h_p = _ssd(proj, dt_p.reshape(bp, tp, LANES), jnp.zeros((bp, CONV_HIST, conv_dim), F32),
                          jnp.zeros((bp, SSD_GROUPS, hg_w, SSD_STATE), F32), ssd_par, layer, col_z, col_xbc, tp)
        attn_o = _moba_prompt(proj, col_q, col_k, col_v)
        merged = _merge(pool_o.reshape(bp * tp, d), ssd_o.reshape(bp * tp, inner), attn_o.reshape(bp * tp, attn_w),
                        w_pool16, w_ssd16, w_attn16, proj2, col_gate, layer)
        xp, u_ffn_p = _mm_res(merged.reshape(bp, tp, d), w_out16, layer, xp, gt1[0], g_post_mix[layer:layer + 1],
                              g_pre_ffn[layer:layer + 1], sc2[0], sh2[0])
        outs["kp"].append(k_p.reshape(bp, tp, ATTN_KV_HEADS, hd))
        outs["vp"].append(v_p.reshape(bp, tp, ATTN_KV_HEADS, hd))
        outs["hp"].append(h_p.reshape(bp, n_heads, SSD_HEADDIM, SSD_STATE))
        outs["cp"].append(proj[:, tp - (SSD_CONV - 1):, col_xbc:col_xbc + conv_dim])
        outs["pp"].append(proj[:, tp - (POOL_HIST - 1):, :d])

        proj2 = proj2_s
        proj = proj2.reshape(bs, ts, -1)
        k_s, v_s, dt_s = proj[:, :, col_k:col_v], proj[:, :, col_v:col_gate], dt_s.reshape(bs, ts, LANES)
        pool_buf = jnp.pad(state_pool[layer], ((0, 0), (1, 0), (0, 0)))
        pool_o = _pool(proj, pool_buf, w_grp16, pool_scale[layer:layer + 1], layer, past)
        tpad = ((0, 0), (0, ts_pad - ts), (0, 0))
        conv_prev = jnp.pad(state_conv[layer], ((0, 0), (CONV_HIST - (SSD_CONV - 1), 0), (0, 0)))
        ssd_o, h_s = _ssd(jnp.pad(proj[:, :, :col_q], tpad), jnp.pad(dt_s, tpad), conv_prev,
                          state_ssm[layer].reshape(bs, SSD_GROUPS, hg_w, SSD_STATE), ssd_par, layer,
                          col_z, col_xbc, ts)
        ssd_o = ssd_o[:, :ts]
        q5 = proj[:, :, col_q:col_k].reshape(bs, ts, ATTN_KV_HEADS, ATTN_GROUP, hd)
        wq_t = jnp.einsum("btkgd,kj->bjdkgt", q5, eye_kv).reshape(bs, kv_w, LANES).astype(BF16)
        cache_k4 = cache_k.reshape(depth, cache_k.shape[1], page * ATTN_KV_HEADS, hd)
        cache_v4 = cache_v.reshape(depth, cache_v.shape[1], page * ATTN_KV_HEADS, hd)
        scores = _moba_scores(cache_k4, layer, page_table, wq_t, pps)
        probs = _moba_softmax(scores, k_s, wq_t, ts)
        o_full = _moba_pv(cache_v4, layer, page_table, probs, jnp.pad(v_s, ((0, 0), (0, LANES - ts), (0, 0))), pps)
        o6 = o_full.reshape(bs, ATTN_KV_HEADS, ATTN_GROUP, ts, ATTN_KV_HEADS, hd)
        attn_o = jnp.einsum("bkgtjd,kj->btkgd", o6, eye_kv).reshape(bs * ts, attn_w).astype(BF16)
        merged = _merge(pool_o.reshape(bs * ts, d), ssd_o.reshape(bs * ts, inner), attn_o,
                        w_pool16, w_ssd16, w_attn16, proj2, col_gate, layer)
        xs_, u_ffn_s = _mm_res(merged.reshape(1, bs * ts, d), w_out16, layer, xs_, gt1[1], g_post_mix[layer:layer + 1],
                               g_pre_ffn[layer:layer + 1], sc2[1], sh2[1])

        hid_p = _ffn_hidden(u_ffn_p.reshape(bp * tp, d), w_gate, w_up, layer)
        hid_s = _ffn_hidden(u_ffn_s.reshape(bs * ts, d), w_gate, w_up, layer)
        xp, up = _mm_res(hid_p.reshape(bp, tp, -1), w_down16, layer, xp, gt2[0], g_post_ffn[layer:layer + 1],
                         g_pre_mix[nxt:nxt + 1], sc1n[0], sh1n[0])
        xs_, us = _mm_res(hid_s.reshape(1, bs * ts, -1), w_down16, layer, xs_, gt2[1], g_post_ffn[layer:layer + 1],
                          g_pre_mix[nxt:nxt + 1], sc1n[1], sh1n[1])
        outs["ks"].append(k_s.reshape(bs, ts, ATTN_KV_HEADS, hd))
        outs["vs"].append(v_s.reshape(bs, ts, ATTN_KV_HEADS, hd))
        outs["hs"].append(h_s.reshape(bs, n_heads, SSD_HEADDIM, SSD_STATE))
        conv_ext = jnp.concatenate([state_conv[layer], proj[:, :, col_xbc:col_xbc + conv_dim]], axis=1)
        outs["cs"].append(conv_ext[:, ts:])
        pool_ext = jnp.concatenate([state_pool[layer], proj[:, :, :d]], axis=1)
        outs["ps"].append(pool_ext[:, ts:])

    st = {n: jnp.stack(v) for n, v in outs.items()}
    return (xp, xs_.reshape(bs, ts, d), st["kp"], st["vp"], st["hp"], st["cp"], st["pp"],
            st["ks"], st["vs"], st["hs"], st["cs"], st["ps"])
```
